```python
import math
import jax, jax.numpy as jnp
from jax import lax
import numpy as np

D_MODEL = 2048
BATCH = 4
SEQ = 4096
DEPTH = 2

N_MIXERS = 2
N_META = 16
CONV_W = 3
N_HEADS = 16
N_KV_HEADS = 4
HEAD_DIM = D_MODEL // N_HEADS
GROUP = N_HEADS // N_KV_HEADS
KV_DIM = N_KV_HEADS * HEAD_DIM
QKV_DIM = D_MODEL + 2 * KV_DIM
WINDOW = 128
BLOCK = 128
N_BUCKETS = 32
MAX_DISTANCE = 128
D_FF = 5632
EPS = 1e-6
N_A = (DEPTH + 1) // 2
N_B = DEPTH // 2

kernel_name = "hybrid_shortconv_swa_convffn_encoder"


def rmsnorm(x, gain):
    xf = x.astype(jnp.float32)
    y = xf * lax.rsqrt(jnp.mean(xf * xf, axis=-1, keepdims=True) + EPS)
    return (y * gain.astype(jnp.float32)).astype(x.dtype)


def dwconv3(h, w):
    hp = jnp.pad(h, ((0, 0), (1, 1), (0, 0)))
    return hp[:, :-2] * w[0] + hp[:, 1:-1] * w[1] + hp[:, 2:] * w[2]


def t5_bucket(rel):
    half = N_BUCKETS // 2
    max_exact = half // 2
    side = jnp.where(rel > 0, half, 0)
    n = jnp.abs(rel)
    nf = jnp.maximum(n, 1).astype(jnp.float32)
    large = max_exact + (jnp.log(nf / max_exact) / math.log(MAX_DISTANCE / max_exact)
                         * (half - max_exact)).astype(jnp.int32)
    large = jnp.minimum(large, half - 1)
    return side + jnp.where(n < max_exact, n, large)


def short_conv_mixer(x, w_in, conv_w, w_out):
    bch = x @ w_in
    b, c, h = jnp.split(bch, 3, axis=-1)
    return (b * dwconv3(c * h, conv_w)) @ w_out


def windowed_gqa(x, w_qkv, q_gain, k_gain, sink, w_o, rel_table):
    bsz, t, _ = x.shape
    lead = BLOCK - N_META
    tp = t + lead
    nb = tp // BLOCK
    scale = HEAD_DIM ** -0.5

    qkv = x @ w_qkv
    q, k, v = jnp.split(qkv, [D_MODEL, D_MODEL + KV_DIM], axis=-1)
    q = rmsnorm(q.reshape(bsz, t, N_HEADS, HEAD_DIM), q_gain)
    k = rmsnorm(k.reshape(bsz, t, N_KV_HEADS, HEAD_DIM), k_gain)
    v = v.reshape(bsz, t, N_KV_HEADS, HEAD_DIM)

    qb = jnp.pad(q, ((0, 0), (lead, 0), (0, 0), (0, 0))).reshape(
        bsz, nb, BLOCK, N_KV_HEADS, GROUP, HEAD_DIM)
    pad_kv = ((0, 0), (lead + BLOCK, BLOCK), (0, 0), (0, 0))
    kb = jnp.pad(k, pad_kv).reshape(bsz, nb + 2, BLOCK, N_KV_HEADS, HEAD_DIM)
    vb = jnp.pad(v, pad_kv).reshape(bsz, nb + 2, BLOCK, N_KV_HEADS, HEAD_DIM)
    k_band = jnp.concatenate([kb[:, :-2], kb[:, 1:-1], kb[:, 2:]], axis=2)
    v_band = jnp.concatenate([vb[:, :-2], vb[:, 1:-1], vb[:, 2:]], axis=2)
    k_meta = k[:, :N_META]
    v_meta = v[:, :N_META]

    qi = jnp.arange(BLOCK, dtype=jnp.int32)
    ki = jnp.arange(3 * BLOCK, dtype=jnp.int32)
    rel_band = ki[None, :] - BLOCK - qi[:, None]
    blk = jnp.arange(nb, dtype=jnp.int32)
    kpos = (blk[:, None] - 1) * BLOCK + ki[None, :]
    key_ok = (kpos >= BLOCK) & (kpos < tp)
    band_mask = (jnp.abs(rel_band) <= WINDOW)[None] & key_ok[:, None, :]

    band_bias = rel_table[t5_bucket(rel_band)]
    band_bias = jnp.transpose(band_bias, (2, 0, 1)).reshape(
        N_KV_HEADS, GROUP, BLOCK, 3 * BLOCK).astype(jnp.float32)
    qpos = blk[:, None] * BLOCK + qi[None, :]
    mpos = lead + jnp.arange(N_META, dtype=jnp.int32)
    meta_bias = rel_table[t5_bucket(mpos[None, None, :] - qpos[:, :, None])]
    meta_bias = jnp.transpose(meta_bias, (0, 3, 1, 2)).reshape(
        nb, N_KV_HEADS, GROUP, BLOCK, N_META).astype(jnp.float32)

    s_band = jnp.einsum('bnqgrd,bnkgd->bngrqk', qb, k_band).astype(jnp.float32) * scale
    s_band = jnp.where(band_mask[None, :, None, None], s_band + band_bias[None, None], -1e30)
    s_meta = jnp.einsum('bnqgrd,bmgd->bngrqm', qb, k_meta).astype(jnp.float32) * scale
    s_meta = s_meta + meta_bias[None]
    s_sink = jnp.broadcast_to(
        sink.astype(jnp.float32).reshape(1, 1, N_KV_HEADS, GROUP, 1, 1),
        (bsz, nb, N_KV_HEADS, GROUP, BLOCK, 1))
    p = jax.nn.softmax(jnp.concatenate([s_meta, s_band, s_sink], axis=-1), axis=-1)
    p_meta = p[..., :N_META].astype(v.dtype)
    p_band = p[..., N_META:N_META + 3 * BLOCK].astype(v.dtype)

    o = (jnp.einsum('bngrqm,bmgd->bnqgrd', p_meta, v_meta)
         + jnp.einsum('bngrqk,bnkgd->bnqgrd', p_band, v_band))
    o = o.reshape(bsz, tp, D_MODEL)[:, lead:]
    return o @ w_o


def conv_ffn(x, w_up, conv_w, conv_b, w_down):
    g, u = jnp.split(x @ w_up, 2, axis=-1)
    g = dwconv3(g, conv_w) + conv_b
    return (jax.nn.gelu(g, approximate=False) * u) @ w_down


def setup_inputs(seed: int = 0) -> dict:
    key = jax.random.key(seed)
    ks = jax.random.split(key, 18)
    nrm = jax.random.normal
    f32 = jnp.float32
    d = D_MODEL
    return {
        "x": nrm(ks[0], (BATCH, SEQ, d), f32),
        "meta_tokens": nrm(ks[1], (N_META, d), f32),
        "rel_bias_table": 0.1 * nrm(ks[2], (N_BUCKETS, N_HEADS), f32),
        "norm_mix": 1.0 + 0.01 * nrm(ks[3], (DEPTH, d), f32),
        "norm_ffn": 1.0 + 0.01 * nrm(ks[4], (DEPTH, d), f32),
        "conv_in_w": nrm(ks[5], (N_A, d, 3 * d), f32) * d ** -0.5,
        "conv_dw": nrm(ks[6], (N_A, CONV_W, d), f32) * CONV_W ** -0.5,
        "conv_out_w": nrm(ks[7], (N_A, d, d), f32) * d ** -0.5,
        "attn_qkv": nrm(ks[8], (N_B, d, QKV_DIM), f32) * d ** -0.5,
        "attn_q_gain": 1.0 + 0.01 * nrm(ks[9], (N_B, HEAD_DIM), f32),
        "attn_k_gain": 1.0 + 0.01 * nrm(ks[10], (N_B, HEAD_DIM), f32),
        "attn_sink": 0.5 * nrm(ks[11], (N_B, N_HEADS), f32),
        "attn_o": nrm(ks[12], (N_B, d, d), f32) * d ** -0.5,
        "ffn_up": nrm(ks[13], (DEPTH, d, 2 * D_FF), f32) * d ** -0.5,
        "ffn_dw": nrm(ks[14], (DEPTH, CONV_W, D_FF), f32) * CONV_W ** -0.5,
        "ffn_dw_b": 0.01 * nrm(ks[15], (DEPTH, D_FF), f32),
        "ffn_down": nrm(ks[16], (DEPTH, D_FF, d), f32) * D_FF ** -0.5,
    }


def reference(x, meta_tokens, rel_bias_table, norm_mix, norm_ffn,
              conv_in_w, conv_dw, conv_out_w,
              attn_qkv, attn_q_gain, attn_k_gain, attn_sink, attn_o,
              ffn_up, ffn_dw, ffn_dw_b, ffn_down):
    bsz = x.shape[0]
    meta = jnp.broadcast_to(meta_tokens[None].astype(x.dtype), (bsz, N_META, D_MODEL))
    h = jnp.concatenate([meta, x], axis=1)
    for i in range(DEPTH):
        hn = rmsnorm(h, norm_mix[i])
        j = i // N_MIXERS
        if i % N_MIXERS == 0:
            h = h + short_conv_mixer(hn, conv_in_w[j], conv_dw[j], conv_out_w[j])
        else:
            h = h + windowed_gqa(hn, attn_qkv[j], attn_q_gain[j], attn_k_gain[j],
                                 attn_sink[j], attn_o[j], rel_bias_table)
        h = h + conv_ffn(rmsnorm(h, norm_ffn[i]), ffn_up[i], ffn_dw[i], ffn_dw_b[i], ffn_down[i])
    return h[:, N_META:]
```

```python
import functools
import math

import jax
import jax.numpy as jnp
from jax import lax
from jax.experimental import pallas as pl
from jax.experimental.pallas import tpu as pltpu

D_MODEL = 2048
N_META = 16
N_HEADS = 16
N_KV_HEADS = 4
HEAD_DIM = D_MODEL // N_HEADS
GROUP = N_HEADS // N_KV_HEADS
KV_DIM = N_KV_HEADS * HEAD_DIM
QKV_DIM = D_MODEL + 2 * KV_DIM
WINDOW = 128
BLOCK = 128
N_BUCKETS = 32
MAX_DISTANCE = 128
D_FF = 5632
EPS = 1e-6
LEAD = BLOCK - N_META

HALO = 16
ROW_TILES_PER_SEQ = 4
COL_TILE = 512
KEYS = 4 * BLOCK
SINK_COL = 3 * BLOCK + N_META
CODE_MASKED = N_BUCKETS
CODE_SINK = N_BUCKETS + 1
NEG_INF = -1e30
V7X_VMEM_LIMIT_BYTES = 56 * 1024 * 1024

_BF16 = jnp.bfloat16
_F32 = jnp.float32


def _dot(a, b):
    return jnp.dot(a, b, preferred_element_type=_F32)


def _rmsnorm_rows(v, gain):
    return v * lax.rsqrt(jnp.mean(v * v, axis=-1, keepdims=True) + EPS) * gain


def _gelu_exact(v):
    return 0.5 * v * (1.0 + lax.erf(v * math.sqrt(0.5)))


def _fill_xext(hprev_ref, hcur_ref, hnext_ref, gain_ref, xext_ref, tm):
    i = pl.program_id(0)
    last = pl.num_programs(0) - 1
    gain = gain_ref[...]
    prev = _rmsnorm_rows(hprev_ref[...], gain) * (i > 0).astype(_F32)
    nxt = _rmsnorm_rows(hnext_ref[...], gain) * (i < last).astype(_F32)
    xext_ref[0:HALO, :] = prev.astype(_BF16)
    xext_ref[HALO:HALO + tm, :] = _rmsnorm_rows(hcur_ref[...], gain).astype(_BF16)
    xext_ref[HALO + tm:, :] = nxt.astype(_BF16)


def _conv3(v_ext, cw_ref, tm):
    w = cw_ref[...]
    return (v_ext[HALO - 1:HALO - 1 + tm] * w[0:1]
            + v_ext[HALO:HALO + tm] * w[1:2]
            + v_ext[HALO + 1:HALO + 1 + tm] * w[2:3])


def _mixer_kernel(tm, hprev_ref, hcur_ref, hnext_ref, gain_ref, wb_ref, wc_ref, wh_ref, cw_ref,
                  wo_ref, out_ref, xext_ref):
    @pl.when(pl.program_id(1) == 0)
    def _():
        _fill_xext(hprev_ref, hcur_ref, hnext_ref, gain_ref, xext_ref, tm)
        out_ref[...] = hcur_ref[...]

    x = xext_ref[...]
    b = _dot(xext_ref[HALO:HALO + tm, :], wb_ref[...])
    ch = _dot(x, wc_ref[...]) * _dot(x, wh_ref[...])
    y = b * _conv3(ch, cw_ref, tm)
    out_ref[...] += _dot(y.astype(_BF16), wo_ref[...])


def _ffn_kernel(tm, hprev_ref, hcur_ref, hnext_ref, gain_ref, wg_ref, wu_ref, cw_ref, cb_ref,
                wd_ref, out_ref, xext_ref):
    @pl.when(pl.program_id(1) == 0)
    def _():
        _fill_xext(hprev_ref, hcur_ref, hnext_ref, gain_ref, xext_ref, tm)
        out_ref[...] = hcur_ref[...]

    g = _conv3(_dot(xext_ref[...], wg_ref[...]), cw_ref, tm) + cb_ref[...]
    u = _dot(xext_ref[HALO:HALO + tm, :], wu_ref[...])
    a = _gelu_exact(g) * u
    out_ref[...] += _dot(a.astype(_BF16), wd_ref[...])


def _fused_block(kernel_body, h, gain, w_up, n_up, conv_w, conv_b, w_down, tm):
    m, d = h.shape
    hidden = w_down.shape[0]
    n_i, n_j = m // tm, hidden // COL_TILE
    halo_blocks_per_tile = tm // HALO
    last_halo_block = m // HALO - 1

    in_specs = [
        pl.BlockSpec((HALO, d), lambda i, j: (jnp.maximum(i * halo_blocks_per_tile - 1, 0), 0)),
        pl.BlockSpec((tm, d), lambda i, j: (i, 0), pipeline_mode=pl.Buffered(1)),
        pl.BlockSpec((HALO, d),
                     lambda i, j: (jnp.minimum((i + 1) * halo_blocks_per_tile, last_halo_block), 0)),
        pl.BlockSpec((1, d), lambda i, j: (0, 0)),
    ]
    operands = [h, h, h, gain.reshape(1, d)]
    for k in range(n_up):
        in_specs.append(pl.BlockSpec((d, COL_TILE), functools.partial(
            lambda i, j, k: (0, j + k * n_j), k=k)))
        operands.append(w_up)
    in_specs.append(pl.BlockSpec((3, COL_TILE), lambda i, j: (0, j)))
    operands.append(conv_w)
    if conv_b is not None:
        in_specs.append(pl.BlockSpec((1, COL_TILE), lambda i, j: (0, j)))
        operands.append(conv_b.reshape(1, hidden))
    in_specs.append(pl.BlockSpec((COL_TILE, d), lambda i, j: (j, 0)))
    operands.append(w_down)

    return pl.pallas_call(
        functools.partial(kernel_body, tm),
        grid=(n_i, n_j),
        in_specs=in_specs,
        out_specs=pl.BlockSpec((tm, d), lambda i, j: (i, 0), pipeline_mode=pl.Buffered(1)),
        out_shape=jax.ShapeDtypeStruct((m, d), _F32),
        scratch_shapes=[pltpu.VMEM((tm + 2 * HALO, d), _BF16)],
        compiler_params=pltpu.CompilerParams(
            dimension_semantics=("arbitrary", "arbitrary"),
            vmem_limit_bytes=V7X_VMEM_LIMIT_BYTES),
    )(*operands)


def _qkv_kernel(n_norm_tiles, h_ref, gain_ref, w_ref, hgain_ref, out_ref, xn_ref):
    j = pl.program_id(1)

    @pl.when(j == 0)
    def _():
        xn_ref[...] = _rmsnorm_rows(h_ref[...], gain_ref[...]).astype(_BF16)

    acc = _dot(xn_ref[...], w_ref[...])

    @pl.when(j < n_norm_tiles)
    def _():
        for hd in range(COL_TILE // HEAD_DIM):
            cols = slice(hd * HEAD_DIM, (hd + 1) * HEAD_DIM)
            out_ref[:, cols] = _rmsnorm_rows(acc[:, cols], hgain_ref[:, cols]).astype(_BF16)

    @pl.when(j >= n_norm_tiles)
    def _():
        out_ref[...] = acc.astype(_BF16)


def _qkv_proj(h, gain, w_qkv, head_gain_cols, tm):
    m, d = h.shape
    n_out = w_qkv.shape[1]
    n_norm_tiles = (D_MODEL + KV_DIM) // COL_TILE
    return pl.pallas_call(
        functools.partial(_qkv_kernel, n_norm_tiles),
        grid=(m // tm, n_out // COL_TILE),
        in_specs=[
            pl.BlockSpec((tm, d), lambda i, j: (i, 0)),
            pl.BlockSpec((1, d), lambda i, j: (0, 0)),
            pl.BlockSpec((d, COL_TILE), lambda i, j: (0, j)),
            pl.BlockSpec((1, COL_TILE), lambda i, j: (0, j)),
        ],
        out_specs=pl.BlockSpec((tm, COL_TILE), lambda i, j: (i, j)),
        out_shape=jax.ShapeDtypeStruct((m, n_out), _BF16),
        scratch_shapes=[pltpu.VMEM((tm, d), _BF16)],
        compiler_params=pltpu.CompilerParams(
            dimension_semantics=("arbitrary", "arbitrary"),
            vmem_limit_bytes=V7X_VMEM_LIMIT_BYTES),
    )(h, gain.reshape(1, d), w_qkv, head_gain_cols)


def _bias_kernel(idx_ref, tbl_ref, out_ref):
    head = pl.program_id(1)
    codes = idx_ref[0]
    acc = jnp.zeros(codes.shape, _F32)
    for code in range(N_BUCKETS + 2):
        acc = jnp.where(codes == code, tbl_ref[head, code], acc)
    out_ref[0, 0] = acc


def _attn_bias(codes, table):
    n_var = codes.shape[0]
    return pl.pallas_call(
        _bias_kernel,
        grid=(n_var, N_HEADS),
        in_specs=[
            pl.BlockSpec((1, BLOCK, KEYS), lambda v, h: (v, 0, 0)),
            pl.BlockSpec(memory_space=pltpu.SMEM),
        ],
        out_specs=pl.BlockSpec((1, 1, BLOCK, KEYS), lambda v, h: (v, h, 0, 0)),
        out_shape=jax.ShapeDtypeStruct((n_var, N_HEADS, BLOCK, KEYS), _F32),
    )(codes, table)


def _attn_kernel(q_ref, kp_ref, kc_ref, kn_ref, km_ref, vp_ref, vc_ref, vn_ref, vm_ref, bias_ref,
                 out_ref):
    n = pl.program_id(1)
    scale = HEAD_DIM ** -0.5
    dead = jnp.zeros((BLOCK - N_META, HEAD_DIM), _BF16)
    row = lax.broadcasted_iota(jnp.int32, (BLOCK, HEAD_DIM), 0)
    keep_row = jnp.logical_or(row >= LEAD, n > 0)
    for g in range(N_KV_HEADS):
        kv_cols = slice(g * HEAD_DIM, (g + 1) * HEAD_DIM)
        q = jnp.concatenate(
            [q_ref[:, (g * GROUP + r) * HEAD_DIM:(g * GROUP + r + 1) * HEAD_DIM] for r in range(GROUP)],
            axis=0)
        k = jnp.concatenate(
            [kp_ref[:, kv_cols], kc_ref[:, kv_cols], kn_ref[:, kv_cols], km_ref[:, kv_cols], dead], axis=0)
        v = jnp.concatenate(
            [vp_ref[:, kv_cols], vc_ref[:, kv_cols], vn_ref[:, kv_cols], vm_ref[:, kv_cols], dead], axis=0)
        s = lax.dot_general(q, k, (((1,), (1,)), ((), ())), preferred_element_type=_F32) * scale
        s = s + bias_ref[0, g * GROUP:(g + 1) * GROUP].reshape(GROUP * BLOCK, KEYS)
        e = jnp.exp(s - jnp.max(s, axis=-1, keepdims=True))
        denom = jnp.sum(e, axis=-1, keepdims=True)
        o = _dot(e.astype(_BF16), v) / denom
        for r in range(GROUP):
            piece = jnp.where(keep_row, o[r * BLOCK:(r + 1) * BLOCK], 0.0)
            out_ref[:, (g * GROUP + r) * HEAD_DIM:(g * GROUP + r + 1) * HEAD_DIM] = piece.astype(_BF16)


def _attention(qkv, bias, bsz, nb):
    m = qkv.shape[0]
    tp = nb * BLOCK
    k_col, v_col = D_MODEL // KV_DIM, D_MODEL // KV_DIM + 1
    meta_row_block = LEAD // N_META

    def blk(col, shift):
        return pl.BlockSpec(
            (BLOCK, KV_DIM), lambda b, n: (b * nb + jnp.clip(n + shift, 0, nb - 1), col))

    def meta(col):
        return pl.BlockSpec((N_META, KV_DIM), lambda b, n: (b * (tp // N_META) + meta_row_block, col))

    return pl.pallas_call(
        _attn_kernel,
        grid=(bsz, nb),
        in_specs=[
            pl.BlockSpec((BLOCK, D_MODEL), lambda b, n: (b * nb + n, 0)),
            blk(k_col, -1), blk(k_col, 0), blk(k_col, 1), meta(k_col),
            blk(v_col, -1), blk(v_col, 0), blk(v_col, 1), meta(v_col),
            pl.BlockSpec((1, N_HEADS, BLOCK, KEYS),
                         lambda b, n: (jnp.where(n == nb - 1, 3, jnp.minimum(n, 2)), 0, 0, 0)),
        ],
        out_specs=pl.BlockSpec((BLOCK, D_MODEL), lambda b, n: (b * nb + n, 0)),
        out_shape=jax.ShapeDtypeStruct((m, D_MODEL), _BF16),
        compiler_params=pltpu.CompilerParams(
            dimension_semantics=("arbitrary", "arbitrary"),
            vmem_limit_bytes=V7X_VMEM_LIMIT_BYTES),
    )(qkv, qkv, qkv, qkv, qkv, qkv, qkv, qkv, qkv, bias)


def _out_proj_kernel(a_ref, w_ref, h_ref, out_ref):
    out_ref[...] = h_ref[...] + _dot(a_ref[...], w_ref[...])


def _out_proj(a, w, h, tm):
    m, d = h.shape
    tn = d // 2
    return pl.pallas_call(
        _out_proj_kernel,
        grid=(m // tm, d // tn),
        in_specs=[
            pl.BlockSpec((tm, a.shape[1]), lambda i, j: (i, 0)),
            pl.BlockSpec((a.shape[1], tn), lambda i, j: (0, j)),
            pl.BlockSpec((tm, tn), lambda i, j: (i, j)),
        ],
        out_specs=pl.BlockSpec((tm, tn), lambda i, j: (i, j)),
        out_shape=jax.ShapeDtypeStruct((m, d), _F32),
        compiler_params=pltpu.CompilerParams(
            dimension_semantics=("arbitrary", "arbitrary"),
            vmem_limit_bytes=V7X_VMEM_LIMIT_BYTES),
    )(a, w, h)


def _t5_bucket(rel):
    half = N_BUCKETS // 2
    max_exact = half // 2
    side = jnp.where(rel > 0, half, 0)
    n = jnp.abs(rel)
    nf = jnp.maximum(n, 1).astype(_F32)
    large = max_exact + (jnp.log(nf / max_exact) / math.log(MAX_DISTANCE / max_exact)
                         * (half - max_exact)).astype(jnp.int32)
    large = jnp.minimum(large, half - 1)
    return side + jnp.where(n < max_exact, n, large)


def _bias_codes(nb):
    qi = jnp.arange(BLOCK, dtype=jnp.int32)[:, None]
    col = jnp.arange(KEYS, dtype=jnp.int32)[None, :]
    rel_band = col - BLOCK - qi
    band = jnp.where(jnp.abs(rel_band) <= WINDOW, _t5_bucket(rel_band), CODE_MASKED)
    variants = []
    for n_rep, first_valid_col, last_valid_col in (
            (0, 2 * BLOCK, 3 * BLOCK), (1, BLOCK, 3 * BLOCK), (2, 0, 3 * BLOCK), (2, 0, 2 * BLOCK)):
        qpos = n_rep * BLOCK + qi
        mpos = LEAD + (col - 3 * BLOCK)
        meta = _t5_bucket(mpos - qpos)
        band_ok = (col >= first_valid_col) & (col < last_valid_col)
        codes = jnp.where(band_ok, band, CODE_MASKED)
        codes = jnp.where((col >= 3 * BLOCK) & (col < SINK_COL), meta, codes)
        codes = jnp.where(col == SINK_COL, CODE_SINK, codes)
        variants.append(codes)
    return jnp.stack(variants).astype(jnp.int32)


def kernel(x, meta_tokens, rel_bias_table, norm_mix, norm_ffn, conv_in_w, conv_dw, conv_out_w,
           attn_qkv, attn_q_gain, attn_k_gain, attn_sink, attn_o, ffn_up, ffn_dw, ffn_dw_b, ffn_down):
    bsz, seq, d = x.shape
    tp = seq + BLOCK
    nb = tp // BLOCK
    tm = tp // ROW_TILES_PER_SEQ
    assert d == D_MODEL and seq % BLOCK == 0 and tm % HALO == 0 and nb >= 4

    meta = jnp.broadcast_to(meta_tokens[None].astype(x.dtype), (bsz, N_META, d))
    h = jnp.concatenate([jnp.zeros((bsz, LEAD, d), x.dtype), meta, x], axis=1).reshape(bsz * tp, d)

    bf = lambda w: w.astype(_BF16)

    h = _fused_block(_mixer_kernel, h, norm_mix[0], bf(conv_in_w[0]), 3, conv_dw[0], None,
                     bf(conv_out_w[0]), tm)
    h = _fused_block(_ffn_kernel, h, norm_ffn[0], bf(ffn_up[0]), 2, ffn_dw[0], ffn_dw_b[0],
                     bf(ffn_down[0]), tm)

    head_gain_cols = jnp.concatenate([
        jnp.tile(attn_q_gain[0], N_HEADS), jnp.tile(attn_k_gain[0], N_KV_HEADS),
        jnp.ones((KV_DIM,), _F32)]).reshape(1, QKV_DIM)
    qkv = _qkv_proj(h, norm_mix[1], bf(attn_qkv[0]), head_gain_cols, tm)
    table = jnp.concatenate([
        rel_bias_table.T.astype(_F32), jnp.full((N_HEADS, 1), NEG_INF, _F32),
        attn_sink[0].astype(_F32)[:, None]], axis=1)
    bias = _attn_bias(_bias_codes(nb), table)
    o = _attention(qkv, bias, bsz, nb)
    h = _out_proj(o, bf(attn_o[0]), h, tm)
    h = _fused_block(_ffn_kernel, h, norm_ffn[1], bf(ffn_up[1]), 2, ffn_dw[1], ffn_dw_b[1],
                     bf(ffn_down[1]), tm)

    return h.reshape(bsz, tp, d)[:, BLOCK:]
```

```python
import functools
import math

import jax
import jax.numpy as jnp
from jax import lax
from jax.experimental import pallas as pl
from jax.experimental.pallas import tpu as pltpu

D_MODEL = 2048
N_META = 16
N_HEADS = 16
N_KV_HEADS = 4
HEAD_DIM = D_MODEL // N_HEADS
GROUP = N_HEADS // N_KV_HEADS
KV_DIM = N_KV_HEADS * HEAD_DIM
QKV_DIM = D_MODEL + 2 * KV_DIM
WINDOW = 128
BLOCK = 128
N_BUCKETS = 32
MAX_DISTANCE = 128
D_FF = 5632
EPS = 1e-6
LEAD = BLOCK - N_META

HALO = 16
ROW_TILES_PER_SLAB = 4
COL_TILE = 512
KEYS = 4 * BLOCK
SINK_COL = 3 * BLOCK + N_META
CODE_MASKED = N_BUCKETS
CODE_SINK = N_BUCKETS + 1
N_BIAS_VARIANTS = 4
NEG_INF = -1e30
V7X_VMEM_LIMIT_BYTES = 56 * 1024 * 1024

_BF16 = jnp.bfloat16
_F32 = jnp.float32


def _dot(a, b):
    return jnp.dot(a, b, preferred_element_type=_F32)


def _rmsnorm_rows(v, gain):
    return v * lax.rsqrt(jnp.mean(v * v, axis=-1, keepdims=True) + EPS) * gain


def _gelu_exact(v):
    return 0.5 * v * (1.0 + lax.erf(v * math.sqrt(0.5)))


def _fill_from_slab(hprev_ref, hcur_ref, hnext_ref, gain_ref, xext_ref, out_ref, tm):
    gain = gain_ref[...]
    xext_ref[0:HALO, :] = _rmsnorm_rows(hprev_ref[0], gain).astype(_BF16)
    xext_ref[HALO:HALO + tm, :] = _rmsnorm_rows(hcur_ref[0], gain).astype(_BF16)
    xext_ref[HALO + tm:, :] = _rmsnorm_rows(hnext_ref[0], gain).astype(_BF16)
    out_ref[0] = hcur_ref[0]


def _fill_from_tokens(xprev_ref, xcur_ref, xnext_ref, meta_ref, gain_ref, xext_ref, out_ref, tm, seq):
    q = lax.rem(pl.program_id(0), ROW_TILES_PER_SLAB)
    gain = gain_ref[...]
    meta = meta_ref[...]
    meta_n = _rmsnorm_rows(meta, gain).astype(_BF16)
    tail_tokens = seq - (ROW_TILES_PER_SLAB - 1) * tm

    @pl.when(q == 0)
    def _():
        xext_ref[0:HALO, :] = meta_n

    @pl.when(q > 0)
    def _():
        xext_ref[0:HALO, :] = _rmsnorm_rows(xprev_ref[0], gain).astype(_BF16)

    @pl.when(q < ROW_TILES_PER_SLAB - 1)
    def _():
        xext_ref[HALO:HALO + tm, :] = _rmsnorm_rows(xcur_ref[0], gain).astype(_BF16)
        out_ref[0] = xcur_ref[0]

    @pl.when(q == ROW_TILES_PER_SLAB - 1)
    def _():
        tok = xcur_ref[0, 0:tail_tokens, :]
        xext_ref[HALO:HALO + tail_tokens, :] = _rmsnorm_rows(tok, gain).astype(_BF16)
        xext_ref[HALO + tail_tokens:HALO + tail_tokens + LEAD, :] = jnp.zeros((LEAD, D_MODEL), _BF16)
        xext_ref[HALO + tail_tokens + LEAD:HALO + tm, :] = meta_n
        out_ref[0, 0:tail_tokens, :] = tok
        out_ref[0, tail_tokens:tail_tokens + LEAD, :] = jnp.zeros((LEAD, D_MODEL), _F32)
        out_ref[0, tail_tokens + LEAD:tm, :] = meta

    xext_ref[HALO + tm:, :] = _rmsnorm_rows(xnext_ref[0], gain).astype(_BF16)


def _conv3(v_ext, cw_ref, tm):
    w = cw_ref[...]
    return (v_ext[HALO - 1:HALO - 1 + tm] * w[0:1]
            + v_ext[HALO:HALO + tm] * w[1:2]
            + v_ext[HALO + 1:HALO + 1 + tm] * w[2:3])


def _mixer_kernel(tm, seq, xprev_ref, xcur_ref, xnext_ref, meta_ref, gain_ref, wb_ref, wc_ref, wh_ref,
                  cw_ref, wo_ref, out_ref, xext_ref):
    @pl.when(pl.program_id(1) == 0)
    def _():
        _fill_from_tokens(xprev_ref, xcur_ref, xnext_ref, meta_ref, gain_ref, xext_ref, out_ref, tm, seq)

    x = xext_ref[...]
    b = _dot(xext_ref[HALO:HALO + tm, :], wb_ref[...])
    ch = _dot(x, wc_ref[...]) * _dot(x, wh_ref[...])
    y = b * _conv3(ch, cw_ref, tm)
    out_ref[0] += _dot(y.astype(_BF16), wo_ref[...])


def _ffn_kernel(tm, hprev_ref, hcur_ref, hnext_ref, gain_ref, wg_ref, wu_ref, cw_ref, cb_ref,
                wd_ref, out_ref, xext_ref):
    @pl.when(pl.program_id(1) == 0)
    def _():
        _fill_from_slab(hprev_ref, hcur_ref, hnext_ref, gain_ref, xext_ref, out_ref, tm)

    g = _conv3(_dot(xext_ref[...], wg_ref[...]), cw_ref, tm) + cb_ref[...]
    u = _dot(xext_ref[HALO:HALO + tm, :], wu_ref[...])
    a = _gelu_exact(g) * u
    out_ref[0] += _dot(a.astype(_BF16), wd_ref[...])


def _row_tile_index(i):
    return lax.div(i, ROW_TILES_PER_SLAB), lax.rem(i, ROW_TILES_PER_SLAB)


def _fused_params():
    return pltpu.CompilerParams(dimension_semantics=("arbitrary", "arbitrary"),
                                vmem_limit_bytes=V7X_VMEM_LIMIT_BYTES)


def _weight_specs(layer, d, hidden, n_up):
    n_j = hidden // COL_TILE
    ups = [pl.BlockSpec((None, d, COL_TILE),
                        functools.partial(lambda i, j, k: (layer, 0, j + k * n_j), k=k))
           for k in range(n_up)]
    down = pl.BlockSpec((None, COL_TILE, d), lambda i, j: (layer, j, 0))
    return ups, down


def _mixer_from_tokens(x, meta_tokens, gain, w_in, conv_w, w_out, layer, tm):
    bsz, seq, d = x.shape
    tp = seq + BLOCK
    halo_blocks_per_tile = tm // HALO

    def prev_map(i, j):
        b, q = _row_tile_index(i)
        return b, jnp.maximum(q * halo_blocks_per_tile - 1, 0), 0

    def next_map(i, j):
        b, q = _row_tile_index(i)
        wrap = q == ROW_TILES_PER_SLAB - 1
        return (jnp.where(wrap, lax.rem(b + 1, bsz), b),
                jnp.where(wrap, 0, (q + 1) * halo_blocks_per_tile), 0)

    ups, down = _weight_specs(layer, d, d, 3)
    in_specs = [
        pl.BlockSpec((1, HALO, d), prev_map),
        pl.BlockSpec((1, tm, d), lambda i, j: (*_row_tile_index(i), 0), pipeline_mode=pl.Buffered(1)),
        pl.BlockSpec((1, HALO, d), next_map),
        pl.BlockSpec((N_META, d), lambda i, j: (0, 0)),
        pl.BlockSpec((1, d), lambda i, j: (0, 0)),
        *ups,
        pl.BlockSpec((None, 3, COL_TILE), lambda i, j: (layer, 0, j)),
        down,
    ]
    return pl.pallas_call(
        functools.partial(_mixer_kernel, tm, seq),
        grid=(bsz * ROW_TILES_PER_SLAB, d // COL_TILE),
        in_specs=in_specs,
        out_specs=pl.BlockSpec((1, tm, d), lambda i, j: (*_row_tile_index(i), 0),
                               pipeline_mode=pl.Buffered(1)),
        out_shape=jax.ShapeDtypeStruct((bsz, tp, d), _F32),
        scratch_shapes=[pltpu.VMEM((tm + 2 * HALO, d), _BF16)],
        compiler_params=_fused_params(),
    )(x, x, x, meta_tokens, gain.reshape(1, d), w_in, w_in, w_in, conv_w, w_out)


def _conv_ffn(h, gain, w_up, conv_w, conv_b, w_down, layer, tm, out_rows):
    bsz, tp, d = h.shape
    hidden = w_down.shape[1]
    halo_blocks_per_tile = tm // HALO
    halo_blocks_per_slab = tp // HALO
    n_halo_blocks = bsz * halo_blocks_per_slab

    def halo_map(i, j, offset):
        blk = lax.rem(i * halo_blocks_per_tile + offset + n_halo_blocks, n_halo_blocks)
        return lax.div(blk, halo_blocks_per_slab), lax.rem(blk, halo_blocks_per_slab), 0

    ups, down = _weight_specs(layer, d, hidden, 2)
    in_specs = [
        pl.BlockSpec((1, HALO, d), functools.partial(halo_map, offset=-1)),
        pl.BlockSpec((1, tm, d), lambda i, j: (*_row_tile_index(i), 0), pipeline_mode=pl.Buffered(1)),
        pl.BlockSpec((1, HALO, d), functools.partial(halo_map, offset=halo_blocks_per_tile)),
        pl.BlockSpec((1, d), lambda i, j: (0, 0)),
        *ups,
        pl.BlockSpec((None, 3, COL_TILE), lambda i, j: (layer, 0, j)),
        pl.BlockSpec((None, 1, COL_TILE), lambda i, j: (layer, 0, j)),
        down,
    ]
    return pl.pallas_call(
        functools.partial(_ffn_kernel, tm),
        grid=(bsz * ROW_TILES_PER_SLAB, hidden // COL_TILE),
        in_specs=in_specs,
        out_specs=pl.BlockSpec((1, tm, d), lambda i, j: (*_row_tile_index(i), 0),
                               pipeline_mode=pl.Buffered(1)),
        out_shape=jax.ShapeDtypeStruct((bsz, out_rows, d), _F32),
        scratch_shapes=[pltpu.VMEM((tm + 2 * HALO, d), _BF16)],
        compiler_params=_fused_params(),
    )(h, h, h, gain.reshape(1, d), w_up, w_up, conv_w, conv_b.reshape(conv_b.shape[0], 1, hidden), w_down)


def _qkv_kernel(n_norm_cols, h_ref, gain_ref, w_ref, hgain_ref, out_ref, xn_ref):
    xn_ref[...] = _rmsnorm_rows(h_ref[...], gain_ref[...]).astype(_BF16)
    for c in range(w_ref.shape[1] // COL_TILE):
        acc = _dot(xn_ref[...], w_ref[:, c * COL_TILE:(c + 1) * COL_TILE])
        for hd in range(COL_TILE // HEAD_DIM):
            lo = hd * HEAD_DIM
            cols = slice(c * COL_TILE + lo, c * COL_TILE + lo + HEAD_DIM)
            head = acc[:, lo:lo + HEAD_DIM]
            if c * COL_TILE + lo < n_norm_cols:
                head = _rmsnorm_rows(head, hgain_ref[:, cols])
            out_ref[:, cols] = head.astype(_BF16)


def _qkv_proj(h, gain, w_qkv, layer, head_gain_cols, tm):
    m, d = h.shape
    n_out = w_qkv.shape[2]
    return pl.pallas_call(
        functools.partial(_qkv_kernel, D_MODEL + KV_DIM),
        grid=(m // tm,),
        in_specs=[
            pl.BlockSpec((tm, d), lambda i: (i, 0)),
            pl.BlockSpec((1, d), lambda i: (0, 0)),
            pl.BlockSpec((None, d, n_out), lambda i: (layer, 0, 0), pipeline_mode=pl.Buffered(1)),
            pl.BlockSpec((1, n_out), lambda i: (0, 0)),
        ],
        out_specs=pl.BlockSpec((tm, n_out), lambda i: (i, 0)),
        out_shape=jax.ShapeDtypeStruct((m, n_out), _BF16),
        scratch_shapes=[pltpu.VMEM((tm, d), _BF16)],
        compiler_params=pltpu.CompilerParams(
            dimension_semantics=("arbitrary",), vmem_limit_bytes=V7X_VMEM_LIMIT_BYTES),
    )(h, gain.reshape(1, d), w_qkv, head_gain_cols)


def _bias_kernel(band_codes_ref, tail_codes_ref, tbl_ref, out_ref):
    head = pl.program_id(0)

    def lookup(codes):
        acc = jnp.zeros(codes.shape, _F32)
        for code in range(N_BUCKETS + 2):
            acc = jnp.where(codes == code, tbl_ref[head, code], acc)
        return acc

    band = lookup(band_codes_ref[...])
    tails = [lookup(tail_codes_ref[v]) for v in range(3)]
    masked = jnp.full((BLOCK, BLOCK), NEG_INF, _F32)
    for variant, valid_blocks, tail in ((0, (2,), 0), (1, (1, 2), 1), (2, (0, 1, 2), 2), (3, (0, 1), 2)):
        for blk in range(3):
            cols = slice(blk * BLOCK, (blk + 1) * BLOCK)
            out_ref[variant, 0, :, cols] = band[:, cols] if blk in valid_blocks else masked
        out_ref[variant, 0, :, 3 * BLOCK:] = tails[tail]


def _attn_bias(band_codes, tail_codes, table):
    return pl.pallas_call(
        _bias_kernel,
        grid=(N_HEADS,),
        in_specs=[
            pl.BlockSpec((BLOCK, 3 * BLOCK), lambda h: (0, 0)),
            pl.BlockSpec((3, BLOCK, BLOCK), lambda h: (0, 0, 0)),
            pl.BlockSpec(memory_space=pltpu.SMEM),
        ],
        out_specs=pl.BlockSpec((N_BIAS_VARIANTS, 1, BLOCK, KEYS), lambda h: (0, h, 0, 0)),
        out_shape=jax.ShapeDtypeStruct((N_BIAS_VARIANTS, N_HEADS, BLOCK, KEYS), _F32),
    )(band_codes, tail_codes, table)


def _attn_kernel(q_ref, kp_ref, kc_ref, kn_ref, km_ref, vp_ref, vc_ref, vn_ref, vm_ref, bias_ref,
                 out_ref):
    n = pl.program_id(1)
    scale = HEAD_DIM ** -0.5
    dead = jnp.zeros((BLOCK - N_META, HEAD_DIM), _BF16)
    row = lax.broadcasted_iota(jnp.int32, (BLOCK, HEAD_DIM), 0)
    keep_row = jnp.logical_or(row >= LEAD, n > 0)
    for g in range(N_KV_HEADS):
        kv_cols = slice(g * HEAD_DIM, (g + 1) * HEAD_DIM)
        q = jnp.concatenate(
            [q_ref[:, (g * GROUP + r) * HEAD_DIM:(g * GROUP + r + 1) * HEAD_DIM] for r in range(GROUP)],
            axis=0)
        k = jnp.concatenate(
            [kp_ref[:, kv_cols], kc_ref[:, kv_cols], kn_ref[:, kv_cols], km_ref[:, kv_cols], dead], axis=0)
        v = jnp.concatenate(
            [vp_ref[:, kv_cols], vc_ref[:, kv_cols], vn_ref[:, kv_cols], vm_ref[:, kv_cols], dead], axis=0)
        s = lax.dot_general(q, k, (((1,), (1,)), ((), ())), preferred_element_type=_F32) * scale
        s = s + bias_ref[0, g * GROUP:(g + 1) * GROUP].reshape(GROUP * BLOCK, KEYS)
        e = jnp.exp(s - jnp.max(s, axis=-1, keepdims=True))
        denom = jnp.sum(e, axis=-1, keepdims=True)
        o = _dot(e.astype(_BF16), v) / denom
        for r in range(GROUP):
            piece = jnp.where(keep_row, o[r * BLOCK:(r + 1) * BLOCK], 0.0)
            out_ref[:, (g * GROUP + r) * HEAD_DIM:(g * GROUP + r + 1) * HEAD_DIM] = piece.astype(_BF16)


def _attention(qkv, bias, bsz, nb):
    m = qkv.shape[0]
    k_col, v_col = D_MODEL // KV_DIM, D_MODEL // KV_DIM + 1

    def block_row(b, n):
        meta_block = lax.rem(b + bsz - 1, bsz) * nb + nb - 1
        return jnp.where(n == 0, meta_block, b * nb + n - 1)

    def blk(col, shift):
        return pl.BlockSpec(
            (BLOCK, KV_DIM), lambda b, n: (block_row(b, jnp.clip(n + shift, 0, nb - 1)), col))

    def meta(col):
        return pl.BlockSpec(
            (N_META, KV_DIM), lambda b, n: (block_row(b, 0) * (BLOCK // N_META) + LEAD // N_META, col))

    return pl.pallas_call(
        _attn_kernel,
        grid=(bsz, nb),
        in_specs=[
            pl.BlockSpec((BLOCK, D_MODEL), lambda b, n: (block_row(b, n), 0)),
            blk(k_col, -1), blk(k_col, 0), blk(k_col, 1), meta(k_col),
            blk(v_col, -1), blk(v_col, 0), blk(v_col, 1), meta(v_col),
            pl.BlockSpec((1, N_HEADS, BLOCK, KEYS),
                         lambda b, n: (jnp.where(n == nb - 1, 3, jnp.minimum(n, 2)), 0, 0, 0)),
        ],
        out_specs=pl.BlockSpec((BLOCK, D_MODEL), lambda b, n: (block_row(b, n), 0)),
        out_shape=jax.ShapeDtypeStruct((m, D_MODEL), _BF16),
        compiler_params=pltpu.CompilerParams(
            dimension_semantics=("arbitrary", "arbitrary"),
            vmem_limit_bytes=V7X_VMEM_LIMIT_BYTES),
    )(qkv, qkv, qkv, qkv, qkv, qkv, qkv, qkv, qkv, bias)


def _out_proj_kernel(a_ref, w_ref, h_ref, out_ref):
    out_ref[...] = h_ref[...] + _dot(a_ref[...], w_ref[...])


def _out_proj(a, w, layer, h, tm):
    m, d = h.shape
    return pl.pallas_call(
        _out_proj_kernel,
        grid=(m // tm,),
        in_specs=[
            pl.BlockSpec((tm, d), lambda i: (i, 0)),
            pl.BlockSpec((None, d, d), lambda i: (layer, 0, 0), pipeline_mode=pl.Buffered(1)),
            pl.BlockSpec((tm, d), lambda i: (i, 0)),
        ],
        out_specs=pl.BlockSpec((tm, d), lambda i: (i, 0)),
        out_shape=jax.ShapeDtypeStruct((m, d), _F32),
        compiler_params=pltpu.CompilerParams(
            dimension_semantics=("arbitrary",), vmem_limit_bytes=V7X_VMEM_LIMIT_BYTES),
    )(a, w, h)


def _t5_bucket(rel):
    half = N_BUCKETS // 2
    max_exact = half // 2
    side = jnp.where(rel > 0, half, 0)
    n = jnp.abs(rel)
    nf = jnp.maximum(n, 1).astype(_F32)
    large = max_exact + (jnp.log(nf / max_exact) / math.log(MAX_DISTANCE / max_exact)
                         * (half - max_exact)).astype(jnp.int32)
    large = jnp.minimum(large, half - 1)
    return side + jnp.where(n < max_exact, n, large)


def _bias_codes():
    qi = jnp.arange(BLOCK, dtype=jnp.int32)[:, None]
    col = jnp.arange(3 * BLOCK, dtype=jnp.int32)[None, :]
    rel_band = col - BLOCK - qi
    band = jnp.where(jnp.abs(rel_band) <= WINDOW, _t5_bucket(rel_band), CODE_MASKED)
    tcol = jnp.arange(BLOCK, dtype=jnp.int32)[None, :]
    tails = []
    for n_rep in range(3):
        qpos = n_rep * BLOCK + qi
        codes = jnp.where(tcol < N_META, _t5_bucket(LEAD + tcol - qpos), CODE_MASKED)
        tails.append(jnp.where(tcol == N_META, CODE_SINK, codes))
    return band.astype(jnp.int32), jnp.stack(tails).astype(jnp.int32)


def kernel(x, meta_tokens, rel_bias_table, norm_mix, norm_ffn, conv_in_w, conv_dw, conv_out_w,
           attn_qkv, attn_q_gain, attn_k_gain, attn_sink, attn_o, ffn_up, ffn_dw, ffn_dw_b, ffn_down):
    bsz, seq, d = x.shape
    tp = seq + BLOCK
    nb = tp // BLOCK
    tm = tp // ROW_TILES_PER_SLAB
    assert d == D_MODEL and seq % BLOCK == 0 and tm % HALO == 0 and nb >= 4
    assert (seq - (ROW_TILES_PER_SLAB - 1) * tm) % HALO == 0 and seq % HALO == 0

    bf = lambda w: w.astype(_BF16)
    ffn_up_bf, ffn_down_bf = bf(ffn_up), bf(ffn_down)

    h = _mixer_from_tokens(x, meta_tokens.astype(x.dtype), norm_mix[0], bf(conv_in_w), conv_dw,
                           bf(conv_out_w), 0, tm)
    h = _conv_ffn(h, norm_ffn[0], ffn_up_bf, ffn_dw, ffn_dw_b, ffn_down_bf, 0, tm, tp)

    head_gain_cols = jnp.concatenate([
        jnp.tile(attn_q_gain[0], N_HEADS), jnp.tile(attn_k_gain[0], N_KV_HEADS),
        jnp.ones((KV_DIM,), _F32)]).reshape(1, QKV_DIM)
    h2 = h.reshape(bsz * tp, d)
    qkv = _qkv_proj(h2, norm_mix[1], bf(attn_qkv), 0, head_gain_cols, tm)
    table = jnp.concatenate([
        rel_bias_table.T.astype(_F32), jnp.full((N_HEADS, 1), NEG_INF, _F32),
        attn_sink[0].astype(_F32)[:, None]], axis=1)
    bias = _attn_bias(*_bias_codes(), table)
    o = _attention(qkv, bias, bsz, nb)
    h2 = _out_proj(o, bf(attn_o), 0, h2, tm // 2)
    return _conv_ffn(h2.reshape(bsz, tp, d), norm_ffn[1], ffn_up_bf, ffn_dw, ffn_dw_b, ffn_down_bf,
                     1, tm, seq)
```

```python
import functools
import math

import jax
import jax.numpy as jnp
from jax import lax
from jax.experimental import pallas as pl
from jax.experimental.pallas import tpu as pltpu

D_MODEL = 2048
N_META = 16
N_HEADS = 16
N_KV_HEADS = 4
HEAD_DIM = D_MODEL // N_HEADS
GROUP = N_HEADS // N_KV_HEADS
KV_DIM = N_KV_HEADS * HEAD_DIM
QKV_DIM = D_MODEL + 2 * KV_DIM
WINDOW = 128
BLOCK = 128
N_BUCKETS = 32
MAX_DISTANCE = 128
D_FF = 5632
EPS = 1e-6
LEAD = BLOCK - N_META

HALO = 16
ROW_TILES_PER_SLAB = 4
MIXER_COL_TILE = 256
FFN_COL_TILE = 512
QKV_COL_TILE = 512
KEYS = 4 * BLOCK
SINK_COL = 3 * BLOCK + N_META
CODE_MASKED = N_BUCKETS
CODE_SINK = N_BUCKETS + 1
N_BIAS_VARIANTS = 4
NEG_INF = -1e30
V7X_VMEM_LIMIT_BYTES = 56 * 1024 * 1024

_BF16 = jnp.bfloat16
_F32 = jnp.float32


def _dot(a, b):
    return jnp.dot(a, b, preferred_element_type=_F32)


def _rmsnorm_rows(v, gain):
    return v * lax.rsqrt(jnp.mean(v * v, axis=-1, keepdims=True) + EPS) * gain


def _gelu_exact(v):
    return 0.5 * v * (1.0 + lax.erf(v * math.sqrt(0.5)))


def _row_tile_index(i):
    return lax.div(i, ROW_TILES_PER_SLAB), lax.rem(i, ROW_TILES_PER_SLAB)


def _prefetch_row_tiles(start_copy, wait_copy):
    i, j = pl.program_id(0), pl.program_id(1)

    @pl.when(jnp.logical_and(i == 0, j == 0))
    def _():
        start_copy(i)

    @pl.when(j == 0)
    def _():
        wait_copy(i)

    @pl.when(jnp.logical_and(j == 1, i + 1 < pl.num_programs(0)))
    def _():
        start_copy(i + 1)


def _slab_tile_copy(h_hbm, hbuf_ref, sem, tm, t):
    b, q = _row_tile_index(t)
    return pltpu.make_async_copy(h_hbm.at[b, pl.ds(q * tm, tm), :], hbuf_ref, sem.at[0])


def _token_tile_copy(op, x_hbm, hbuf_ref, sem, tm, tail_tokens, t):
    b, q = _row_tile_index(t)

    @pl.when(q < ROW_TILES_PER_SLAB - 1)
    def _():
        op(pltpu.make_async_copy(x_hbm.at[b, pl.ds(q * tm, tm), :], hbuf_ref, sem.at[0]))

    @pl.when(q == ROW_TILES_PER_SLAB - 1)
    def _():
        op(pltpu.make_async_copy(x_hbm.at[b, pl.ds((ROW_TILES_PER_SLAB - 1) * tm, tail_tokens), :],
                                 hbuf_ref.at[pl.ds(0, tail_tokens), :], sem.at[0]))


def _fill_from_slab(hprev_ref, hbuf_ref, hnext_ref, gain_ref, xext_ref, out_ref, tm):
    gain = gain_ref[...]
    xext_ref[0:HALO, :] = _rmsnorm_rows(hprev_ref[0], gain).astype(_BF16)
    xext_ref[HALO:HALO + tm, :] = _rmsnorm_rows(hbuf_ref[...], gain).astype(_BF16)
    xext_ref[HALO + tm:, :] = _rmsnorm_rows(hnext_ref[0], gain).astype(_BF16)
    out_ref[0] = hbuf_ref[...]


def _fill_from_tokens(xprev_ref, hbuf_ref, xnext_ref, meta_ref, gain_ref, xext_ref, out_ref, tm,
                      tail_tokens):
    q = lax.rem(pl.program_id(0), ROW_TILES_PER_SLAB)
    gain = gain_ref[...]
    meta = meta_ref[...]
    meta_n = _rmsnorm_rows(meta, gain).astype(_BF16)

    @pl.when(q == 0)
    def _():
        xext_ref[0:HALO, :] = meta_n

    @pl.when(q > 0)
    def _():
        xext_ref[0:HALO, :] = _rmsnorm_rows(xprev_ref[0], gain).astype(_BF16)

    @pl.when(q < ROW_TILES_PER_SLAB - 1)
    def _():
        xext_ref[HALO:HALO + tm, :] = _rmsnorm_rows(hbuf_ref[...], gain).astype(_BF16)
        out_ref[0] = hbuf_ref[...]

    @pl.when(q == ROW_TILES_PER_SLAB - 1)
    def _():
        tok = hbuf_ref[0:tail_tokens, :]
        xext_ref[HALO:HALO + tail_tokens, :] = _rmsnorm_rows(tok, gain).astype(_BF16)
        xext_ref[HALO + tail_tokens:HALO + tail_tokens + LEAD, :] = jnp.zeros((LEAD, D_MODEL), _BF16)
        xext_ref[HALO + tail_tokens + LEAD:HALO + tm, :] = meta_n
        out_ref[0, 0:tail_tokens, :] = tok
        out_ref[0, tail_tokens:tail_tokens + LEAD, :] = jnp.zeros((LEAD, D_MODEL), _F32)
        out_ref[0, tail_tokens + LEAD:tm, :] = meta

    xext_ref[HALO + tm:, :] = _rmsnorm_rows(xnext_ref[0], gain).astype(_BF16)


def _conv3(v_ext, cw_ref, tm):
    w = cw_ref[...]
    return (v_ext[HALO - 1:HALO - 1 + tm] * w[0:1]
            + v_ext[HALO:HALO + tm] * w[1:2]
            + v_ext[HALO + 1:HALO + 1 + tm] * w[2:3])


def _mixer_kernel(tm, tail_tokens, xprev_ref, x_hbm, xnext_ref, meta_ref, gain_ref, wb_ref, wc_ref,
                  wh_ref, cw_ref, wo_ref, out_ref, xext_ref, hbuf_ref, sem):
    copy = functools.partial(_token_tile_copy, x_hbm=x_hbm, hbuf_ref=hbuf_ref, sem=sem, tm=tm,
                             tail_tokens=tail_tokens)
    _prefetch_row_tiles(lambda t: copy(lambda c: c.start(), t=t), lambda t: copy(lambda c: c.wait(), t=t))

    @pl.when(pl.program_id(1) == 0)
    def _():
        _fill_from_tokens(xprev_ref, hbuf_ref, xnext_ref, meta_ref, gain_ref, xext_ref, out_ref, tm,
                          tail_tokens)

    x = xext_ref[...]
    b = _dot(xext_ref[HALO:HALO + tm, :], wb_ref[...])
    ch = _dot(x, wc_ref[...]) * _dot(x, wh_ref[...])
    y = b * _conv3(ch, cw_ref, tm)
    out_ref[0] += _dot(y.astype(_BF16), wo_ref[...])


def _ffn_kernel(tm, hprev_ref, h_hbm, hnext_ref, gain_ref, wg_ref, wu_ref, cw_ref, cb_ref,
                wd_ref, out_ref, xext_ref, hbuf_ref, sem):
    _prefetch_row_tiles(lambda t: _slab_tile_copy(h_hbm, hbuf_ref, sem, tm, t).start(),
                        lambda t: _slab_tile_copy(h_hbm, hbuf_ref, sem, tm, t).wait())

    @pl.when(pl.program_id(1) == 0)
    def _():
        _fill_from_slab(hprev_ref, hbuf_ref, hnext_ref, gain_ref, xext_ref, out_ref, tm)

    g = _conv3(_dot(xext_ref[...], wg_ref[...]), cw_ref, tm) + cb_ref[...]
    u = _dot(xext_ref[HALO:HALO + tm, :], wu_ref[...])
    a = _gelu_exact(g) * u
    out_ref[0] += _dot(a.astype(_BF16), wd_ref[...])


def _fused_params():
    return pltpu.CompilerParams(dimension_semantics=("arbitrary", "arbitrary"),
                                vmem_limit_bytes=V7X_VMEM_LIMIT_BYTES)


def _fused_scratch(tm, d):
    return [pltpu.VMEM((tm + 2 * HALO, d), _BF16), pltpu.VMEM((tm, d), _F32),
            pltpu.SemaphoreType.DMA((1,))]


def _weight_specs(layer, d, hidden, n_up, col_tile):
    n_j = hidden // col_tile
    ups = [pl.BlockSpec((None, d, col_tile),
                        functools.partial(lambda i, j, k: (layer, 0, j + k * n_j), k=k))
           for k in range(n_up)]
    down = pl.BlockSpec((None, col_tile, d), lambda i, j: (layer, j, 0))
    return ups, down


def _mixer_from_tokens(x, meta_tokens, gain, w_in, conv_w, w_out, layer, tm):
    bsz, seq, d = x.shape
    tp = seq + BLOCK
    halo_blocks_per_tile = tm // HALO
    col_tile = MIXER_COL_TILE

    def prev_map(i, j):
        b, q = _row_tile_index(i)
        return b, jnp.maximum(q * halo_blocks_per_tile - 1, 0), 0

    def next_map(i, j):
        b, q = _row_tile_index(i)
        wrap = q == ROW_TILES_PER_SLAB - 1
        return (jnp.where(wrap, lax.rem(b + 1, bsz), b),
                jnp.where(wrap, 0, (q + 1) * halo_blocks_per_tile), 0)

    ups, down = _weight_specs(layer, d, d, 3, col_tile)
    in_specs = [
        pl.BlockSpec((1, HALO, d), prev_map),
        pl.BlockSpec(memory_space=pl.ANY),
        pl.BlockSpec((1, HALO, d), next_map),
        pl.BlockSpec((N_META, d), lambda i, j: (0, 0)),
        pl.BlockSpec((1, d), lambda i, j: (0, 0)),
        *ups,
        pl.BlockSpec((None, 3, col_tile), lambda i, j: (layer, 0, j)),
        down,
    ]
    return pl.pallas_call(
        functools.partial(_mixer_kernel, tm, seq - (ROW_TILES_PER_SLAB - 1) * tm),
        grid=(bsz * ROW_TILES_PER_SLAB, d // col_tile),
        in_specs=in_specs,
        out_specs=pl.BlockSpec((1, tm, d), lambda i, j: (*_row_tile_index(i), 0)),
        out_shape=jax.ShapeDtypeStruct((bsz, tp, d), _F32),
        scratch_shapes=_fused_scratch(tm, d),
        compiler_params=_fused_params(),
    )(x, x, x, meta_tokens, gain.reshape(1, d), w_in, w_in, w_in, conv_w, w_out)


def _conv_ffn(h, gain, w_up, conv_w, conv_b, w_down, layer, tm, out_rows):
    bsz, tp, d = h.shape
    hidden = w_down.shape[1]
    halo_blocks_per_tile = tm // HALO
    halo_blocks_per_slab = tp // HALO
    n_halo_blocks = bsz * halo_blocks_per_slab
    col_tile = FFN_COL_TILE

    def halo_map(i, j, offset):
        blk = lax.rem(i * halo_blocks_per_tile + offset + n_halo_blocks, n_halo_blocks)
        return lax.div(blk, halo_blocks_per_slab), lax.rem(blk, halo_blocks_per_slab), 0

    ups, down = _weight_specs(layer, d, hidden, 2, col_tile)
    in_specs = [
        pl.BlockSpec((1, HALO, d), functools.partial(halo_map, offset=-1)),
        pl.BlockSpec(memory_space=pl.ANY),
        pl.BlockSpec((1, HALO, d), functools.partial(halo_map, offset=halo_blocks_per_tile)),
        pl.BlockSpec((1, d), lambda i, j: (0, 0)),
        *ups,
        pl.BlockSpec((None, 3, col_tile), lambda i, j: (layer, 0, j)),
        pl.BlockSpec((None, 1, col_tile), lambda i, j: (layer, 0, j)),
        down,
    ]
    return pl.pallas_call(
        functools.partial(_ffn_kernel, tm),
        grid=(bsz * ROW_TILES_PER_SLAB, hidden // col_tile),
        in_specs=in_specs,
        out_specs=pl.BlockSpec((1, tm, d), lambda i, j: (*_row_tile_index(i), 0)),
        out_shape=jax.ShapeDtypeStruct((bsz, out_rows, d), _F32),
        scratch_shapes=_fused_scratch(tm, d),
        compiler_params=_fused_params(),
    )(h, h, h, gain.reshape(1, d), w_up, w_up, conv_w, conv_b.reshape(conv_b.shape[0], 1, hidden), w_down)


def _qkv_kernel(n_norm_cols, h_ref, gain_ref, w_ref, hgain_ref, out_ref, xn_ref):
    xn_ref[...] = _rmsnorm_rows(h_ref[...], gain_ref[...]).astype(_BF16)
    for c in range(w_ref.shape[1] // QKV_COL_TILE):
        acc = _dot(xn_ref[...], w_ref[:, c * QKV_COL_TILE:(c + 1) * QKV_COL_TILE])
        for hd in range(QKV_COL_TILE // HEAD_DIM):
            lo = hd * HEAD_DIM
            cols = slice(c * QKV_COL_TILE + lo, c * QKV_COL_TILE + lo + HEAD_DIM)
            head = acc[:, lo:lo + HEAD_DIM]
            if c * QKV_COL_TILE + lo < n_norm_cols:
                head = _rmsnorm_rows(head, hgain_ref[:, cols])
            out_ref[:, cols] = head.astype(_BF16)


def _qkv_proj(h, gain, w_qkv, layer, head_gain_cols, tm):
    m, d = h.shape
    n_out = w_qkv.shape[2]
    return pl.pallas_call(
        functools.partial(_qkv_kernel, D_MODEL + KV_DIM),
        grid=(m // tm,),
        in_specs=[
            pl.BlockSpec((tm, d), lambda i: (i, 0)),
            pl.BlockSpec((1, d), lambda i: (0, 0)),
            pl.BlockSpec((None, d, n_out), lambda i: (layer, 0, 0), pipeline_mode=pl.Buffered(1)),
            pl.BlockSpec((1, n_out), lambda i: (0, 0)),
        ],
        out_specs=pl.BlockSpec((tm, n_out), lambda i: (i, 0)),
        out_shape=jax.ShapeDtypeStruct((m, n_out), _BF16),
        scratch_shapes=[pltpu.VMEM((tm, d), _BF16)],
        compiler_params=pltpu.CompilerParams(
            dimension_semantics=("arbitrary",), vmem_limit_bytes=V7X_VMEM_LIMIT_BYTES),
    )(h, gain.reshape(1, d), w_qkv, head_gain_cols)


def _bias_kernel(band_codes_ref, tail_codes_ref, tbl_ref, out_ref):
    head = pl.program_id(0)

    def lookup(codes):
        acc = jnp.zeros(codes.shape, _F32)
        for code in range(N_BUCKETS + 2):
            acc = jnp.where(codes == code, tbl_ref[head, code], acc)
        return acc

    band = lookup(band_codes_ref[...])
    tails = [lookup(tail_codes_ref[v]) for v in range(3)]
    masked = jnp.full((BLOCK, BLOCK), NEG_INF, _F32)
    for variant, valid_blocks, tail in ((0, (2,), 0), (1, (1, 2), 1), (2, (0, 1, 2), 2), (3, (0, 1), 2)):
        for blk in range(3):
            cols = slice(blk * BLOCK, (blk + 1) * BLOCK)
            out_ref[variant, 0, :, cols] = band[:, cols] if blk in valid_blocks else masked
        out_ref[variant, 0, :, 3 * BLOCK:] = tails[tail]


def _attn_bias(band_codes, tail_codes, table):
    return pl.pallas_call(
        _bias_kernel,
        grid=(N_HEADS,),
        in_specs=[
            pl.BlockSpec((BLOCK, 3 * BLOCK), lambda h: (0, 0)),
            pl.BlockSpec((3, BLOCK, BLOCK), lambda h: (0, 0, 0)),
            pl.BlockSpec(memory_space=pltpu.SMEM),
        ],
        out_specs=pl.BlockSpec((N_BIAS_VARIANTS, 1, BLOCK, KEYS), lambda h: (0, h, 0, 0)),
        out_shape=jax.ShapeDtypeStruct((N_BIAS_VARIANTS, N_HEADS, BLOCK, KEYS), _F32),
    )(band_codes, tail_codes, table)


def _attn_kernel(q_ref, kp_ref, kc_ref, kn_ref, km_ref, vp_ref, vc_ref, vn_ref, vm_ref, bias_ref,
                 out_ref):
    n = pl.program_id(1)
    scale = HEAD_DIM ** -0.5
    dead = jnp.zeros((BLOCK - N_META, HEAD_DIM), _BF16)
    row = lax.broadcasted_iota(jnp.int32, (BLOCK, HEAD_DIM), 0)
    keep_row = jnp.logical_or(row >= LEAD, n > 0)
    for g in range(N_KV_HEADS):
        kv_cols = slice(g * HEAD_DIM, (g + 1) * HEAD_DIM)
        q = jnp.concatenate(
            [q_ref[:, (g * GROUP + r) * HEAD_DIM:(g * GROUP + r + 1) * HEAD_DIM] for r in range(GROUP)],
            axis=0)
        k = jnp.concatenate(
            [kp_ref[:, kv_cols], kc_ref[:, kv_cols], kn_ref[:, kv_cols], km_ref[:, kv_cols], dead], axis=0)
        v = jnp.concatenate(
            [vp_ref[:, kv_cols], vc_ref[:, kv_cols], vn_ref[:, kv_cols], vm_ref[:, kv_cols], dead], axis=0)
        s = lax.dot_general(q, k, (((1,), (1,)), ((), ())), preferred_element_type=_F32) * scale
        s = s + bias_ref[0, g * GROUP:(g + 1) * GROUP].reshape(GROUP * BLOCK, KEYS)
        e = jnp.exp(s - jnp.max(s, axis=-1, keepdims=True))
        denom = jnp.sum(e, axis=-1, keepdims=True)
        o = _dot(e.astype(_BF16), v) / denom
        for r in range(GROUP):
            piece = jnp.where(keep_row, o[r * BLOCK:(r + 1) * BLOCK], 0.0)
            out_ref[:, (g * GROUP + r) * HEAD_DIM:(g * GROUP + r + 1) * HEAD_DIM] = piece.astype(_BF16)


def _attention(qkv, bias, bsz, nb):
    m = qkv.shape[0]
    k_col, v_col = D_MODEL // KV_DIM, D_MODEL // KV_DIM + 1

    def block_row(b, n):
        meta_block = lax.rem(b + bsz - 1, bsz) * nb + nb - 1
        return jnp.where(n == 0, meta_block, b * nb + n - 1)

    def blk(col, shift):
        return pl.BlockSpec(
            (BLOCK, KV_DIM), lambda b, n: (block_row(b, jnp.clip(n + shift, 0, nb - 1)), col))

    def meta(col):
        return pl.BlockSpec(
            (N_META, KV_DIM), lambda b, n: (block_row(b, 0) * (BLOCK // N_META) + LEAD // N_META, col))

    return pl.pallas_call(
        _attn_kernel,
        grid=(bsz, nb),
        in_specs=[
            pl.BlockSpec((BLOCK, D_MODEL), lambda b, n: (block_row(b, n), 0)),
            blk(k_col, -1), blk(k_col, 0), blk(k_col, 1), meta(k_col),
            blk(v_col, -1), blk(v_col, 0), blk(v_col, 1), meta(v_col),
            pl.BlockSpec((1, N_HEADS, BLOCK, KEYS),
                         lambda b, n: (jnp.where(n == nb - 1, 3, jnp.minimum(n, 2)), 0, 0, 0)),
        ],
        out_specs=pl.BlockSpec((BLOCK, D_MODEL), lambda b, n: (block_row(b, n), 0)),
        out_shape=jax.ShapeDtypeStruct((m, D_MODEL), _BF16),
        compiler_params=pltpu.CompilerParams(
            dimension_semantics=("arbitrary", "arbitrary"),
            vmem_limit_bytes=V7X_VMEM_LIMIT_BYTES),
    )(qkv, qkv, qkv, qkv, qkv, qkv, qkv, qkv, qkv, bias)


def _out_proj_kernel(a_ref, w_ref, h_ref, out_ref):
    out_ref[...] = h_ref[...] + _dot(a_ref[...], w_ref[...])


def _out_proj(a, w, layer, h, tm):
    m, d = h.shape
    return pl.pallas_call(
        _out_proj_kernel,
        grid=(m // tm,),
        in_specs=[
            pl.BlockSpec((tm, d), lambda i: (i, 0)),
            pl.BlockSpec((None, d, d), lambda i: (layer, 0, 0), pipeline_mode=pl.Buffered(1)),
            pl.BlockSpec((tm, d), lambda i: (i, 0)),
        ],
        out_specs=pl.BlockSpec((tm, d), lambda i: (i, 0)),
        out_shape=jax.ShapeDtypeStruct((m, d), _F32),
        compiler_params=pltpu.CompilerParams(
            dimension_semantics=("arbitrary",), vmem_limit_bytes=V7X_VMEM_LIMIT_BYTES),
    )(a, w, h)


def _t5_bucket(rel):
    half = N_BUCKETS // 2
    max_exact = half // 2
    side = jnp.where(rel > 0, half, 0)
    n = jnp.abs(rel)
    nf = jnp.maximum(n, 1).astype(_F32)
    large = max_exact + (jnp.log(nf / max_exact) / math.log(MAX_DISTANCE / max_exact)
                         * (half - max_exact)).astype(jnp.int32)
    large = jnp.minimum(large, half - 1)
    return side + jnp.where(n < max_exact, n, large)


def _bias_codes():
    qi = jnp.arange(BLOCK, dtype=jnp.int32)[:, None]
    col = jnp.arange(3 * BLOCK, dtype=jnp.int32)[None, :]
    rel_band = col - BLOCK - qi
    band = jnp.where(jnp.abs(rel_band) <= WINDOW, _t5_bucket(rel_band), CODE_MASKED)
    tcol = jnp.arange(BLOCK, dtype=jnp.int32)[None, :]
    tails = []
    for n_rep in range(3):
        qpos = n_rep * BLOCK + qi
        codes = jnp.where(tcol < N_META, _t5_bucket(LEAD + tcol - qpos), CODE_MASKED)
        tails.append(jnp.where(tcol == N_META, CODE_SINK, codes))
    return band.astype(jnp.int32), jnp.stack(tails).astype(jnp.int32)


def kernel(x, meta_tokens, rel_bias_table, norm_mix, norm_ffn, conv_in_w, conv_dw, conv_out_w,
           attn_qkv, attn_q_gain, attn_k_gain, attn_sink, attn_o, ffn_up, ffn_dw, ffn_dw_b, ffn_down):
    bsz, seq, d = x.shape
    tp = seq + BLOCK
    nb = tp // BLOCK
    tm = tp // ROW_TILES_PER_SLAB
    assert d == D_MODEL and seq % BLOCK == 0 and tm % HALO == 0 and nb >= 4
    assert (seq - (ROW_TILES_PER_SLAB - 1) * tm) % HALO == 0 and seq % HALO == 0

    bf = lambda w: w.astype(_BF16)
    ffn_up_bf, ffn_down_bf = bf(ffn_up), bf(ffn_down)

    h = _mixer_from_tokens(x, meta_tokens.astype(x.dtype), norm_mix[0], bf(conv_in_w), conv_dw,
                           bf(conv_out_w), 0, tm)
    h = _conv_ffn(h, norm_ffn[0], ffn_up_bf, ffn_dw, ffn_dw_b, ffn_down_bf, 0, tm, tp)

    head_gain_cols = jnp.concatenate([
        jnp.tile(attn_q_gain[0], N_HEADS), jnp.tile(attn_k_gain[0], N_KV_HEADS),
        jnp.ones((KV_DIM,), _F32)]).reshape(1, QKV_DIM)
    h2 = h.reshape(bsz * tp, d)
    qkv = _qkv_proj(h2, norm_mix[1], bf(attn_qkv), 0, head_gain_cols, tm)
    table = jnp.concatenate([
        rel_bias_table.T.astype(_F32), jnp.full((N_HEADS, 1), NEG_INF, _F32),
        attn_sink[0].astype(_F32)[:, None]], axis=1)
    bias = _attn_bias(*_bias_codes(), table)
    o = _attention(qkv, bias, bsz, nb)
    h2 = _out_proj(o, bf(attn_o), 0, h2, tm // 2)
    return _conv_ffn(h2.reshape(bsz, tp, d), norm_ffn[1], ffn_up_bf, ffn_dw, ffn_dw_b, ffn_down_bf,
                     1, tm, seq)
```

```python
import functools
import math

import jax
import jax.numpy as jnp
from jax import lax
from jax.experimental import pallas as pl
from jax.experimental.pallas import tpu as pltpu

D_MODEL = 2048
N_META = 16
N_HEADS = 16
N_KV_HEADS = 4
HEAD_DIM = D_MODEL // N_HEADS
GROUP = N_HEADS // N_KV_HEADS
KV_DIM = N_KV_HEADS * HEAD_DIM
QKV_DIM = D_MODEL + 2 * KV_DIM
WINDOW = 128
BLOCK = 128
N_BUCKETS = 32
MAX_DISTANCE = 128
D_FF = 5632
EPS = 1e-6
LEAD = BLOCK - N_META

HALO = 16
FFN_ROW_TILES_PER_SLAB = 4
MIXER_ROW_TILES_PER_SLAB = 6
MIXER_COL_TILE = 512
FFN_COL_TILE = 512
QKV_COL_TILE = 512
KEYS = 4 * BLOCK
SINK_COL = 3 * BLOCK + N_META
CODE_MASKED = N_BUCKETS
CODE_SINK = N_BUCKETS + 1
N_BIAS_VARIANTS = 4
NEG_INF = -1e30
V7X_VMEM_LIMIT_BYTES = 56 * 1024 * 1024

_BF16 = jnp.bfloat16
_F32 = jnp.float32


def _dot(a, b):
    return jnp.dot(a, b, preferred_element_type=_F32)


def _rmsnorm_rows(v, gain):
    return v * lax.rsqrt(jnp.mean(v * v, axis=-1, keepdims=True) + EPS) * gain


def _gelu_exact(v):
    return 0.5 * v * (1.0 + lax.erf(v * math.sqrt(0.5)))


def _row_tile_index(i, tiles):
    return lax.div(i, tiles), lax.rem(i, tiles)


def _prefetch_row_tiles(start_copy, wait_copy):
    i, j = pl.program_id(0), pl.program_id(1)

    @pl.when(jnp.logical_and(i == 0, j == 0))
    def _():
        start_copy(i)

    @pl.when(j == 0)
    def _():
        wait_copy(i)

    @pl.when(jnp.logical_and(j == 1, i + 1 < pl.num_programs(0)))
    def _():
        start_copy(i + 1)


def _slab_tile_copy(h_hbm, hbuf_ref, sem, tm, t):
    b, q = _row_tile_index(t, h_hbm.shape[1] // tm)
    return pltpu.make_async_copy(h_hbm.at[b, pl.ds(q * tm, tm), :], hbuf_ref, sem.at[0])


def _token_tile_copy(op, x_hbm, hbuf_ref, sem, tm, tiles, tail_tokens, t):
    b, q = _row_tile_index(t, tiles)

    @pl.when(q < tiles - 1)
    def _():
        op(pltpu.make_async_copy(x_hbm.at[b, pl.ds(q * tm, tm), :], hbuf_ref, sem.at[0]))

    @pl.when(q == tiles - 1)
    def _():
        op(pltpu.make_async_copy(x_hbm.at[b, pl.ds((tiles - 1) * tm, tail_tokens), :],
                                 hbuf_ref.at[pl.ds(0, tail_tokens), :], sem.at[0]))


def _fill_from_slab(hprev_ref, hbuf_ref, hnext_ref, gain_ref, xext_ref, out_ref, tm):
    gain = gain_ref[...]
    xext_ref[0:HALO, :] = _rmsnorm_rows(hprev_ref[0], gain).astype(_BF16)
    xext_ref[HALO:HALO + tm, :] = _rmsnorm_rows(hbuf_ref[...], gain).astype(_BF16)
    xext_ref[HALO + tm:, :] = _rmsnorm_rows(hnext_ref[0], gain).astype(_BF16)
    out_ref[0] = hbuf_ref[...]


def _fill_from_tokens(xprev_ref, hbuf_ref, xnext_ref, meta_ref, gain_ref, xext_ref, out_ref, tm,
                      tiles, tail_tokens):
    q = lax.rem(pl.program_id(0), tiles)
    gain = gain_ref[...]
    meta = meta_ref[...]
    meta_n = _rmsnorm_rows(meta, gain).astype(_BF16)

    @pl.when(q == 0)
    def _():
        xext_ref[0:HALO, :] = meta_n

    @pl.when(q > 0)
    def _():
        xext_ref[0:HALO, :] = _rmsnorm_rows(xprev_ref[0], gain).astype(_BF16)

    @pl.when(q < tiles - 1)
    def _():
        xext_ref[HALO:HALO + tm, :] = _rmsnorm_rows(hbuf_ref[...], gain).astype(_BF16)
        out_ref[0] = hbuf_ref[...]

    @pl.when(q == tiles - 1)
    def _():
        tok = hbuf_ref[0:tail_tokens, :]
        xext_ref[HALO:HALO + tail_tokens, :] = _rmsnorm_rows(tok, gain).astype(_BF16)
        xext_ref[HALO + tail_tokens:HALO + tail_tokens + LEAD, :] = jnp.zeros((LEAD, D_MODEL), _BF16)
        xext_ref[HALO + tail_tokens + LEAD:HALO + tm, :] = meta_n
        out_ref[0, 0:tail_tokens, :] = tok
        out_ref[0, tail_tokens:tail_tokens + LEAD, :] = jnp.zeros((LEAD, D_MODEL), _F32)
        out_ref[0, tail_tokens + LEAD:tm, :] = meta

    xext_ref[HALO + tm:, :] = _rmsnorm_rows(xnext_ref[0], gain).astype(_BF16)


def _conv3(v_ext, cw_ref, tm):
    w = cw_ref[...]
    return (v_ext[HALO - 1:HALO - 1 + tm] * w[0:1]
            + v_ext[HALO:HALO + tm] * w[1:2]
            + v_ext[HALO + 1:HALO + 1 + tm] * w[2:3])


def _mixer_kernel(tm, tiles, tail_tokens, xprev_ref, x_hbm, xnext_ref, meta_ref, gain_ref, wb_ref, wc_ref,
                  wh_ref, cw_ref, wo_ref, out_ref, xext_ref, hbuf_ref, sem):
    copy = functools.partial(_token_tile_copy, x_hbm=x_hbm, hbuf_ref=hbuf_ref, sem=sem, tm=tm,
                             tiles=tiles, tail_tokens=tail_tokens)
    _prefetch_row_tiles(lambda t: copy(lambda c: c.start(), t=t), lambda t: copy(lambda c: c.wait(), t=t))

    @pl.when(pl.program_id(1) == 0)
    def _():
        _fill_from_tokens(xprev_ref, hbuf_ref, xnext_ref, meta_ref, gain_ref, xext_ref, out_ref, tm,
                          tiles, tail_tokens)

    x = xext_ref[...]
    b = _dot(xext_ref[HALO:HALO + tm, :], wb_ref[...])
    ch = _dot(x, wc_ref[...]) * _dot(x, wh_ref[...])
    y = b * _conv3(ch, cw_ref, tm)
    out_ref[0] += _dot(y.astype(_BF16), wo_ref[...])


def _ffn_kernel(tm, hprev_ref, h_hbm, hnext_ref, gain_ref, wg_ref, wu_ref, cw_ref, cb_ref,
                wd_ref, out_ref, xext_ref, hbuf_ref, sem):
    _prefetch_row_tiles(lambda t: _slab_tile_copy(h_hbm, hbuf_ref, sem, tm, t).start(),
                        lambda t: _slab_tile_copy(h_hbm, hbuf_ref, sem, tm, t).wait())

    @pl.when(pl.program_id(1) == 0)
    def _():
        _fill_from_slab(hprev_ref, hbuf_ref, hnext_ref, gain_ref, xext_ref, out_ref, tm)

    g = _conv3(_dot(xext_ref[...], wg_ref[...]), cw_ref, tm) + cb_ref[...]
    u = _dot(xext_ref[HALO:HALO + tm, :], wu_ref[...])
    a = _gelu_exact(g) * u
    out_ref[0] += _dot(a.astype(_BF16), wd_ref[...])


def _fused_params():
    return pltpu.CompilerParams(dimension_semantics=("arbitrary", "arbitrary"),
                                vmem_limit_bytes=V7X_VMEM_LIMIT_BYTES)


def _fused_scratch(tm, d):
    return [pltpu.VMEM((tm + 2 * HALO, d), _BF16), pltpu.VMEM((tm, d), _F32),
            pltpu.SemaphoreType.DMA((1,))]


def _weight_specs(layer, d, hidden, n_up, col_tile):
    n_j = hidden // col_tile
    ups = [pl.BlockSpec((None, d, col_tile),
                        functools.partial(lambda i, j, k: (layer, 0, j + k * n_j), k=k))
           for k in range(n_up)]
    down = pl.BlockSpec((None, col_tile, d), lambda i, j: (layer, j, 0))
    return ups, down


def _mixer_from_tokens(x, meta_tokens, gain, w_in, conv_w, w_out, layer, tm):
    bsz, seq, d = x.shape
    tp = seq + BLOCK
    tiles = tp // tm
    tail_tokens = seq - (tiles - 1) * tm
    assert tiles * tm == tp and 0 < tail_tokens <= tm and tail_tokens % HALO == 0 and tm % HALO == 0
    halo_blocks_per_tile = tm // HALO
    col_tile = MIXER_COL_TILE

    def prev_map(i, j):
        b, q = _row_tile_index(i, tiles)
        return b, jnp.maximum(q * halo_blocks_per_tile - 1, 0), 0

    def next_map(i, j):
        b, q = _row_tile_index(i, tiles)
        wrap = q == tiles - 1
        return (jnp.where(wrap, lax.rem(b + 1, bsz), b),
                jnp.where(wrap, 0, (q + 1) * halo_blocks_per_tile), 0)

    ups, down = _weight_specs(layer, d, d, 3, col_tile)
    in_specs = [
        pl.BlockSpec((1, HALO, d), prev_map),
        pl.BlockSpec(memory_space=pl.ANY),
        pl.BlockSpec((1, HALO, d), next_map),
        pl.BlockSpec((N_META, d), lambda i, j: (0, 0)),
        pl.BlockSpec((1, d), lambda i, j: (0, 0)),
        *ups,
        pl.BlockSpec((None, 3, col_tile), lambda i, j: (layer, 0, j)),
        down,
    ]
    return pl.pallas_call(
        functools.partial(_mixer_kernel, tm, tiles, tail_tokens),
        grid=(bsz * tiles, d // col_tile),
        in_specs=in_specs,
        out_specs=pl.BlockSpec((1, tm, d), lambda i, j: (*_row_tile_index(i, tiles), 0)),
        out_shape=jax.ShapeDtypeStruct((bsz, tp, d), _F32),
        scratch_shapes=_fused_scratch(tm, d),
        compiler_params=_fused_params(),
    )(x, x, x, meta_tokens, gain.reshape(1, d), w_in, w_in, w_in, conv_w, w_out)


def _conv_ffn(h, gain, w_up, conv_w, conv_b, w_down, layer, tm, out_rows):
    bsz, tp, d = h.shape
    hidden = w_down.shape[1]
    tiles = tp // tm
    assert tiles * tm == tp and tm % HALO == 0
    halo_blocks_per_tile = tm // HALO
    halo_blocks_per_slab = tp // HALO
    n_halo_blocks = bsz * halo_blocks_per_slab
    col_tile = FFN_COL_TILE

    def halo_map(i, j, offset):
        blk = lax.rem(i * halo_blocks_per_tile + offset + n_halo_blocks, n_halo_blocks)
        return lax.div(blk, halo_blocks_per_slab), lax.rem(blk, halo_blocks_per_slab), 0

    ups, down = _weight_specs(layer, d, hidden, 2, col_tile)
    in_specs = [
        pl.BlockSpec((1, HALO, d), functools.partial(halo_map, offset=-1)),
        pl.BlockSpec(memory_space=pl.ANY),
        pl.BlockSpec((1, HALO, d), functools.partial(halo_map, offset=halo_blocks_per_tile)),
        pl.BlockSpec((1, d), lambda i, j: (0, 0)),
        *ups,
        pl.BlockSpec((None, 3, col_tile), lambda i, j: (layer, 0, j)),
        pl.BlockSpec((None, 1, col_tile), lambda i, j: (layer, 0, j)),
        down,
    ]
    return pl.pallas_call(
        functools.partial(_ffn_kernel, tm),
        grid=(bsz * tiles, hidden // col_tile),
        in_specs=in_specs,
        out_specs=pl.BlockSpec((1, tm, d), lambda i, j: (*_row_tile_index(i, tiles), 0)),
        out_shape=jax.ShapeDtypeStruct((bsz, out_rows, d), _F32),
        scratch_shapes=_fused_scratch(tm, d),
        compiler_params=_fused_params(),
    )(h, h, h, gain.reshape(1, d), w_up, w_up, conv_w, conv_b.reshape(conv_b.shape[0], 1, hidden), w_down)


def _qkv_kernel(n_norm_cols, h_ref, gain_ref, w_ref, hgain_ref, out_ref, xn_ref):
    xn_ref[...] = _rmsnorm_rows(h_ref[...], gain_ref[...]).astype(_BF16)
    for c in range(w_ref.shape[1] // QKV_COL_TILE):
        acc = _dot(xn_ref[...], w_ref[:, c * QKV_COL_TILE:(c + 1) * QKV_COL_TILE])
        for hd in range(QKV_COL_TILE // HEAD_DIM):
            lo = hd * HEAD_DIM
            cols = slice(c * QKV_COL_TILE + lo, c * QKV_COL_TILE + lo + HEAD_DIM)
            head = acc[:, lo:lo + HEAD_DIM]
            if c * QKV_COL_TILE + lo < n_norm_cols:
                head = _rmsnorm_rows(head, hgain_ref[:, cols])
            out_ref[:, cols] = head.astype(_BF16)


def _qkv_proj(h, gain, w_qkv, layer, head_gain_cols, tm):
    m, d = h.shape
    n_out = w_qkv.shape[2]
    return pl.pallas_call(
        functools.partial(_qkv_kernel, D_MODEL + KV_DIM),
        grid=(m // tm,),
        in_specs=[
            pl.BlockSpec((tm, d), lambda i: (i, 0)),
            pl.BlockSpec((1, d), lambda i: (0, 0)),
            pl.BlockSpec((None, d, n_out), lambda i: (layer, 0, 0), pipeline_mode=pl.Buffered(1)),
            pl.BlockSpec((1, n_out), lambda i: (0, 0)),
        ],
        out_specs=pl.BlockSpec((tm, n_out), lambda i: (i, 0)),
        out_shape=jax.ShapeDtypeStruct((m, n_out), _BF16),
        scratch_shapes=[pltpu.VMEM((tm, d), _BF16)],
        compiler_params=pltpu.CompilerParams(
            dimension_semantics=("arbitrary",), vmem_limit_bytes=V7X_VMEM_LIMIT_BYTES),
    )(h, gain.reshape(1, d), w_qkv, head_gain_cols)


def _bias_kernel(band_codes_ref, tail_codes_ref, tbl_ref, out_ref):
    head = pl.program_id(0)

    def lookup(codes):
        acc = jnp.zeros(codes.shape, _F32)
        for code in range(N_BUCKETS + 2):
            acc = jnp.where(codes == code, tbl_ref[head, code], acc)
        return acc

    band = lookup(band_codes_ref[...])
    tails = [lookup(tail_codes_ref[v]) for v in range(3)]
    masked = jnp.full((BLOCK, BLOCK), NEG_INF, _F32)
    for variant, valid_blocks, tail in ((0, (2,), 0), (1, (1, 2), 1), (2, (0, 1, 2), 2), (3, (0, 1), 2)):
        for blk in range(3):
            cols = slice(blk * BLOCK, (blk + 1) * BLOCK)
            out_ref[variant, 0, :, cols] = band[:, cols] if blk in valid_blocks else masked
        out_ref[variant, 0, :, 3 * BLOCK:] = tails[tail]


def _attn_bias(band_codes, tail_codes, table):
    return pl.pallas_call(
        _bias_kernel,
        grid=(N_HEADS,),
        in_specs=[
            pl.BlockSpec((BLOCK, 3 * BLOCK), lambda h: (0, 0)),
            pl.BlockSpec((3, BLOCK, BLOCK), lambda h: (0, 0, 0)),
            pl.BlockSpec(memory_space=pltpu.SMEM),
        ],
        out_specs=pl.BlockSpec((N_BIAS_VARIANTS, 1, BLOCK, KEYS), lambda h: (0, h, 0, 0)),
        out_shape=jax.ShapeDtypeStruct((N_BIAS_VARIANTS, N_HEADS, BLOCK, KEYS), _F32),
    )(band_codes, tail_codes, table)


def _attn_kernel(q_ref, kp_ref, kc_ref, kn_ref, km_ref, vp_ref, vc_ref, vn_ref, vm_ref, bias_ref,
                 out_ref):
    n = pl.program_id(1)
    scale = HEAD_DIM ** -0.5
    dead = jnp.zeros((BLOCK - N_META, HEAD_DIM), _BF16)
    row = lax.broadcasted_iota(jnp.int32, (BLOCK, HEAD_DIM), 0)
    keep_row = jnp.logical_or(row >= LEAD, n > 0)
    for g in range(N_KV_HEADS):
        kv_cols = slice(g * HEAD_DIM, (g + 1) * HEAD_DIM)
        q = jnp.concatenate(
            [q_ref[:, (g * GROUP + r) * HEAD_DIM:(g * GROUP + r + 1) * HEAD_DIM] for r in range(GROUP)],
            axis=0)
        k = jnp.concatenate(
            [kp_ref[:, kv_cols], kc_ref[:, kv_cols], kn_ref[:, kv_cols], km_ref[:, kv_cols], dead], axis=0)
        v = jnp.concatenate(
            [vp_ref[:, kv_cols], vc_ref[:, kv_cols], vn_ref[:, kv_cols], vm_ref[:, kv_cols], dead], axis=0)
        s = lax.dot_general(q, k, (((1,), (1,)), ((), ())), preferred_element_type=_F32) * scale
        s = s + bias_ref[0, g * GROUP:(g + 1) * GROUP].reshape(GROUP * BLOCK, KEYS)
        e = jnp.exp(s - jnp.max(s, axis=-1, keepdims=True))
        denom = jnp.sum(e, axis=-1, keepdims=True)
        o = _dot(e.astype(_BF16), v) / denom
        for r in range(GROUP):
            piece = jnp.where(keep_row, o[r * BLOCK:(r + 1) * BLOCK], 0.0)
            out_ref[:, (g * GROUP + r) * HEAD_DIM:(g * GROUP + r + 1) * HEAD_DIM] = piece.astype(_BF16)


def _attention(qkv, bias, bsz, nb):
    m = qkv.shape[0]
    k_col, v_col = D_MODEL // KV_DIM, D_MODEL // KV_DIM + 1

    def block_row(b, n):
        meta_block = lax.rem(b + bsz - 1, bsz) * nb + nb - 1
        return jnp.where(n == 0, meta_block, b * nb + n - 1)

    def blk(col, shift):
        return pl.BlockSpec(
            (BLOCK, KV_DIM), lambda b, n: (block_row(b, jnp.clip(n + shift, 0, nb - 1)), col))

    def meta(col):
        return pl.BlockSpec(
            (N_META, KV_DIM), lambda b, n: (block_row(b, 0) * (BLOCK // N_META) + LEAD // N_META, col))

    return pl.pallas_call(
        _attn_kernel,
        grid=(bsz, nb),
        in_specs=[
            pl.BlockSpec((BLOCK, D_MODEL), lambda b, n: (block_row(b, n), 0)),
            blk(k_col, -1), blk(k_col, 0), blk(k_col, 1), meta(k_col),
            blk(v_col, -1), blk(v_col, 0), blk(v_col, 1), meta(v_col),
            pl.BlockSpec((1, N_HEADS, BLOCK, KEYS),
                         lambda b, n: (jnp.where(n == nb - 1, 3, jnp.minimum(n, 2)), 0, 0, 0)),
        ],
        out_specs=pl.BlockSpec((BLOCK, D_MODEL), lambda b, n: (block_row(b, n), 0)),
        out_shape=jax.ShapeDtypeStruct((m, D_MODEL), _BF16),
        compiler_params=pltpu.CompilerParams(
            dimension_semantics=("arbitrary", "arbitrary"),
            vmem_limit_bytes=V7X_VMEM_LIMIT_BYTES),
    )(qkv, qkv, qkv, qkv, qkv, qkv, qkv, qkv, qkv, bias)


def _out_proj_kernel(a_ref, w_ref, h_ref, out_ref):
    out_ref[...] = h_ref[...] + _dot(a_ref[...], w_ref[...])


def _out_proj(a, w, layer, h, tm):
    m, d = h.shape
    return pl.pallas_call(
        _out_proj_kernel,
        grid=(m // tm,),
        in_specs=[
            pl.BlockSpec((tm, d), lambda i: (i, 0)),
            pl.BlockSpec((None, d, d), lambda i: (layer, 0, 0), pipeline_mode=pl.Buffered(1)),
            pl.BlockSpec((tm, d), lambda i: (i, 0)),
        ],
        out_specs=pl.BlockSpec((tm, d), lambda i: (i, 0)),
        out_shape=jax.ShapeDtypeStruct((m, d), _F32),
        compiler_params=pltpu.CompilerParams(
            dimension_semantics=("arbitrary",), vmem_limit_bytes=V7X_VMEM_LIMIT_BYTES),
    )(a, w, h)


def _t5_bucket(rel):
    half = N_BUCKETS // 2
    max_exact = half // 2
    side = jnp.where(rel > 0, half, 0)
    n = jnp.abs(rel)
    nf = jnp.maximum(n, 1).astype(_F32)
    large = max_exact + (jnp.log(nf / max_exact) / math.log(MAX_DISTANCE / max_exact)
                         * (half - max_exact)).astype(jnp.int32)
    large = jnp.minimum(large, half - 1)
    return side + jnp.where(n < max_exact, n, large)


def _bias_codes():
    qi = jnp.arange(BLOCK, dtype=jnp.int32)[:, None]
    col = jnp.arange(3 * BLOCK, dtype=jnp.int32)[None, :]
    rel_band = col - BLOCK - qi
    band = jnp.where(jnp.abs(rel_band) <= WINDOW, _t5_bucket(rel_band), CODE_MASKED)
    tcol = jnp.arange(BLOCK, dtype=jnp.int32)[None, :]
    tails = []
    for n_rep in range(3):
        qpos = n_rep * BLOCK + qi
        codes = jnp.where(tcol < N_META, _t5_bucket(LEAD + tcol - qpos), CODE_MASKED)
        tails.append(jnp.where(tcol == N_META, CODE_SINK, codes))
    return band.astype(jnp.int32), jnp.stack(tails).astype(jnp.int32)


def kernel(x, meta_tokens, rel_bias_table, norm_mix, norm_ffn, conv_in_w, conv_dw, conv_out_w,
           attn_qkv, attn_q_gain, attn_k_gain, attn_sink, attn_o, ffn_up, ffn_dw, ffn_dw_b, ffn_down):
    bsz, seq, d = x.shape
    tp = seq + BLOCK
    nb = tp // BLOCK
    tm = tp // FFN_ROW_TILES_PER_SLAB
    assert d == D_MODEL and seq % BLOCK == 0 and nb >= 4

    bf = lambda w: w.astype(_BF16)
    ffn_up_bf, ffn_down_bf = bf(ffn_up), bf(ffn_down)

    h = _mixer_from_tokens(x, meta_tokens.astype(x.dtype), norm_mix[0], bf(conv_in_w), conv_dw,
                           bf(conv_out_w), 0, tp // MIXER_ROW_TILES_PER_SLAB)
    h = _conv_ffn(h, norm_ffn[0], ffn_up_bf, ffn_dw, ffn_dw_b, ffn_down_bf, 0, tm, tp)

    head_gain_cols = jnp.concatenate([
        jnp.tile(attn_q_gain[0], N_HEADS), jnp.tile(attn_k_gain[0], N_KV_HEADS),
        jnp.ones((KV_DIM,), _F32)]).reshape(1, QKV_DIM)
    h2 = h.reshape(bsz * tp, d)
    qkv = _qkv_proj(h2, norm_mix[1], bf(attn_qkv), 0, head_gain_cols, tm)
    table = jnp.concatenate([
        rel_bias_table.T.astype(_F32), jnp.full((N_HEADS, 1), NEG_INF, _F32),
        attn_sink[0].astype(_F32)[:, None]], axis=1)
    bias = _attn_bias(*_bias_codes(), table)
    o = _attention(qkv, bias, bsz, nb)
    h2 = _out_proj(o, bf(attn_o), 0, h2, tm // 2)
    return _conv_ffn(h2.reshape(bsz, tp, d), norm_ffn[1], ffn_up_bf, ffn_dw, ffn_dw_b, ffn_down_bf,
                     1, tm, seq)
```

```python
import functools
import math

import jax
import jax.numpy as jnp
from jax import lax
from jax.experimental import pallas as pl
from jax.experimental.pallas import tpu as pltpu

D_MODEL = 2048
N_META = 16
N_HEADS = 16
N_KV_HEADS = 4
HEAD_DIM = D_MODEL // N_HEADS
GROUP = N_HEADS // N_KV_HEADS
KV_DIM = N_KV_HEADS * HEAD_DIM
QKV_DIM = D_MODEL + 2 * KV_DIM
WINDOW = 128
BLOCK = 128
N_BUCKETS = 32
MAX_DISTANCE = 128
D_FF = 5632
EPS = 1e-6
LEAD = BLOCK - N_META

HALO = 16
NORM_ROWS = 32
FFN_ROW_TILES_PER_SLAB = 4
MIXER_ROW_TILES_PER_SLAB = 4
MIXER_COL_TILE = 512
FFN_COL_TILE = 512
QKV_COL_TILE = 512
KEYS = 4 * BLOCK
SINK_COL = 3 * BLOCK + N_META
CODE_MASKED = N_BUCKETS
CODE_SINK = N_BUCKETS + 1
N_BIAS_VARIANTS = 4
NEG_INF = -1e30
LOG2_E = math.log2(math.e)
V7X_VMEM_LIMIT_BYTES = 56 * 1024 * 1024

_BF16 = jnp.bfloat16
_F32 = jnp.float32


def _dot(a, b):
    return jnp.dot(a, b, preferred_element_type=_F32)


def _rmsnorm_rows(v, gain):
    return v * lax.rsqrt(jnp.mean(v * v, axis=-1, keepdims=True) + EPS) * gain


def _norm_rows_into(xext_ref, row0, src_ref, n_rows, gain, copy_ref=None):
    for r in range(0, n_rows, NORM_ROWS):
        rows = min(NORM_ROWS, n_rows - r)
        v = src_ref[r:r + rows, :]
        xext_ref[row0 + r:row0 + r + rows, :] = _rmsnorm_rows(v, gain).astype(_BF16)
        if copy_ref is not None:
            copy_ref[r:r + rows, :] = v


def _gelu_exact(v):
    return 0.5 * v * (1.0 + lax.erf(v * math.sqrt(0.5)))


def _row_tile_index(i, tiles):
    return lax.div(i, tiles), lax.rem(i, tiles)


def _prefetch_row_tiles(start_copy, wait_copy):
    i, j = pl.program_id(0), pl.program_id(1)

    @pl.when(jnp.logical_and(i == 0, j == 0))
    def _():
        start_copy(i)

    @pl.when(j == 0)
    def _():
        wait_copy(i)

    @pl.when(jnp.logical_and(j == 1, i + 1 < pl.num_programs(0)))
    def _():
        start_copy(i + 1)


def _slab_tile_copy(h_hbm, hbuf_ref, sem, tm, t):
    b, q = _row_tile_index(t, h_hbm.shape[1] // tm)
    return pltpu.make_async_copy(h_hbm.at[b, pl.ds(q * tm, tm), :], hbuf_ref, sem.at[0])


def _token_tile_copy(op, x_hbm, hbuf_ref, sem, tm, tiles, tail_tokens, t):
    b, q = _row_tile_index(t, tiles)

    @pl.when(q < tiles - 1)
    def _():
        op(pltpu.make_async_copy(x_hbm.at[b, pl.ds(q * tm, tm), :], hbuf_ref, sem.at[0]))

    @pl.when(q == tiles - 1)
    def _():
        op(pltpu.make_async_copy(x_hbm.at[b, pl.ds((tiles - 1) * tm, tail_tokens), :],
                                 hbuf_ref.at[pl.ds(0, tail_tokens), :], sem.at[0]))


def _fill_from_slab(hprev_ref, hbuf_ref, hnext_ref, gain_ref, xext_ref, out_ref, tm):
    gain = gain_ref[...]
    xext_ref[0:HALO, :] = _rmsnorm_rows(hprev_ref[0], gain).astype(_BF16)
    _norm_rows_into(xext_ref, HALO, hbuf_ref, tm, gain, out_ref.at[0])
    xext_ref[HALO + tm:, :] = _rmsnorm_rows(hnext_ref[0], gain).astype(_BF16)


def _fill_from_tokens(xprev_ref, hbuf_ref, xnext_ref, meta_ref, gain_ref, xext_ref, out_ref, tm,
                      tiles, tail_tokens):
    q = lax.rem(pl.program_id(0), tiles)
    gain = gain_ref[...]
    meta = meta_ref[...]
    meta_n = _rmsnorm_rows(meta, gain).astype(_BF16)

    @pl.when(q == 0)
    def _():
        xext_ref[0:HALO, :] = meta_n

    @pl.when(q > 0)
    def _():
        xext_ref[0:HALO, :] = _rmsnorm_rows(xprev_ref[0], gain).astype(_BF16)

    @pl.when(q < tiles - 1)
    def _():
        _norm_rows_into(xext_ref, HALO, hbuf_ref, tm, gain, out_ref.at[0])

    @pl.when(q == tiles - 1)
    def _():
        _norm_rows_into(xext_ref, HALO, hbuf_ref, tail_tokens, gain, out_ref.at[0])
        xext_ref[HALO + tail_tokens:HALO + tail_tokens + LEAD, :] = jnp.zeros((LEAD, D_MODEL), _BF16)
        xext_ref[HALO + tail_tokens + LEAD:HALO + tm, :] = meta_n
        out_ref[0, tail_tokens:tail_tokens + LEAD, :] = jnp.zeros((LEAD, D_MODEL), _F32)
        out_ref[0, tail_tokens + LEAD:tm, :] = meta

    xext_ref[HALO + tm:, :] = _rmsnorm_rows(xnext_ref[0], gain).astype(_BF16)


def _conv3(v_ext, cw_ref, tm):
    w = cw_ref[...]
    return (v_ext[HALO - 1:HALO - 1 + tm] * w[0:1]
            + v_ext[HALO:HALO + tm] * w[1:2]
            + v_ext[HALO + 1:HALO + 1 + tm] * w[2:3])


def _mixer_kernel(tm, tiles, tail_tokens, xprev_ref, x_hbm, xnext_ref, meta_ref, gain_ref, wb_ref, wc_ref,
                  wh_ref, cw_ref, wo_ref, out_ref, xext_ref, hbuf_ref, sem):
    copy = functools.partial(_token_tile_copy, x_hbm=x_hbm, hbuf_ref=hbuf_ref, sem=sem, tm=tm,
                             tiles=tiles, tail_tokens=tail_tokens)
    _prefetch_row_tiles(lambda t: copy(lambda c: c.start(), t=t), lambda t: copy(lambda c: c.wait(), t=t))

    @pl.when(pl.program_id(1) == 0)
    def _():
        _fill_from_tokens(xprev_ref, hbuf_ref, xnext_ref, meta_ref, gain_ref, xext_ref, out_ref, tm,
                          tiles, tail_tokens)

    x = xext_ref[...]
    b = _dot(xext_ref[HALO:HALO + tm, :], wb_ref[...])
    ch = _dot(x, wc_ref[...]) * _dot(x, wh_ref[...])
    y = b * _conv3(ch, cw_ref, tm)
    out_ref[0] += _dot(y.astype(_BF16), wo_ref[...])


def _ffn_kernel(tm, hprev_ref, h_hbm, hnext_ref, gain_ref, wg_ref, wu_ref, cw_ref, cb_ref,
                wd_ref, out_ref, xext_ref, hbuf_ref, sem):
    _prefetch_row_tiles(lambda t: _slab_tile_copy(h_hbm, hbuf_ref, sem, tm, t).start(),
                        lambda t: _slab_tile_copy(h_hbm, hbuf_ref, sem, tm, t).wait())

    @pl.when(pl.program_id(1) == 0)
    def _():
        _fill_from_slab(hprev_ref, hbuf_ref, hnext_ref, gain_ref, xext_ref, out_ref, tm)

    g = _conv3(_dot(xext_ref[...], wg_ref[...]), cw_ref, tm) + cb_ref[...]
    u = _dot(xext_ref[HALO:HALO + tm, :], wu_ref[...])
    a = _gelu_exact(g) * u
    out_ref[0] += _dot(a.astype(_BF16), wd_ref[...])


def _fused_params():
    return pltpu.CompilerParams(dimension_semantics=("arbitrary", "arbitrary"),
                                vmem_limit_bytes=V7X_VMEM_LIMIT_BYTES)


def _fused_scratch(tm, d):
    return [pltpu.VMEM((tm + 2 * HALO, d), _BF16), pltpu.VMEM((tm, d), _F32),
            pltpu.SemaphoreType.DMA((1,))]


def _weight_specs(layer, d, hidden, n_up, col_tile):
    n_j = hidden // col_tile
    ups = [pl.BlockSpec((None, d, col_tile),
                        functools.partial(lambda i, j, k: (layer, 0, j + k * n_j), k=k))
           for k in range(n_up)]
    down = pl.BlockSpec((None, col_tile, d), lambda i, j: (layer, j, 0))
    return ups, down


def _mixer_from_tokens(x, meta_tokens, gain, w_in, conv_w, w_out, layer, tm):
    bsz, seq, d = x.shape
    tp = seq + BLOCK
    tiles = tp // tm
    tail_tokens = seq - (tiles - 1) * tm
    assert tiles * tm == tp and 0 < tail_tokens <= tm and tail_tokens % HALO == 0 and tm % HALO == 0
    halo_blocks_per_tile = tm // HALO
    col_tile = MIXER_COL_TILE

    def prev_map(i, j):
        b, q = _row_tile_index(i, tiles)
        return b, jnp.maximum(q * halo_blocks_per_tile - 1, 0), 0

    def next_map(i, j):
        b, q = _row_tile_index(i, tiles)
        wrap = q == tiles - 1
        return (jnp.where(wrap, lax.rem(b + 1, bsz), b),
                jnp.where(wrap, 0, (q + 1) * halo_blocks_per_tile), 0)

    ups, down = _weight_specs(layer, d, d, 3, col_tile)
    in_specs = [
        pl.BlockSpec((1, HALO, d), prev_map),
        pl.BlockSpec(memory_space=pl.ANY),
        pl.BlockSpec((1, HALO, d), next_map),
        pl.BlockSpec((N_META, d), lambda i, j: (0, 0)),
        pl.BlockSpec((1, d), lambda i, j: (0, 0)),
        *ups,
        pl.BlockSpec((None, 3, col_tile), lambda i, j: (layer, 0, j)),
        down,
    ]
    return pl.pallas_call(
        functools.partial(_mixer_kernel, tm, tiles, tail_tokens),
        grid=(bsz * tiles, d // col_tile),
        in_specs=in_specs,
        out_specs=pl.BlockSpec((1, tm, d), lambda i, j: (*_row_tile_index(i, tiles), 0)),
        out_shape=jax.ShapeDtypeStruct((bsz, tp, d), _F32),
        scratch_shapes=_fused_scratch(tm, d),
        compiler_params=_fused_params(),
    )(x, x, x, meta_tokens, gain.reshape(1, d), w_in, w_in, w_in, conv_w, w_out)


def _conv_ffn(h, gain, w_up, conv_w, conv_b, w_down, layer, tm, out_rows):
    bsz, tp, d = h.shape
    hidden = w_down.shape[1]
    tiles = tp // tm
    assert tiles * tm == tp and tm % HALO == 0
    halo_blocks_per_tile = tm // HALO
    halo_blocks_per_slab = tp // HALO
    n_halo_blocks = bsz * halo_blocks_per_slab
    col_tile = FFN_COL_TILE

    def halo_map(i, j, offset):
        blk = lax.rem(i * halo_blocks_per_tile + offset + n_halo_blocks, n_halo_blocks)
        return lax.div(blk, halo_blocks_per_slab), lax.rem(blk, halo_blocks_per_slab), 0

    ups, down = _weight_specs(layer, d, hidden, 2, col_tile)
    in_specs = [
        pl.BlockSpec((1, HALO, d), functools.partial(halo_map, offset=-1)),
        pl.BlockSpec(memory_space=pl.ANY),
        pl.BlockSpec((1, HALO, d), functools.partial(halo_map, offset=halo_blocks_per_tile)),
        pl.BlockSpec((1, d), lambda i, j: (0, 0)),
        *ups,
        pl.BlockSpec((None, 3, col_tile), lambda i, j: (layer, 0, j)),
        pl.BlockSpec((None, 1, col_tile), lambda i, j: (layer, 0, j)),
        down,
    ]
    return pl.pallas_call(
        functools.partial(_ffn_kernel, tm),
        grid=(bsz * tiles, hidden // col_tile),
        in_specs=in_specs,
        out_specs=pl.BlockSpec((1, tm, d), lambda i, j: (*_row_tile_index(i, tiles), 0)),
        out_shape=jax.ShapeDtypeStruct((bsz, out_rows, d), _F32),
        scratch_shapes=_fused_scratch(tm, d),
        compiler_params=_fused_params(),
    )(h, h, h, gain.reshape(1, d), w_up, w_up, conv_w, conv_b.reshape(conv_b.shape[0], 1, hidden), w_down)


def _qkv_kernel(n_norm_cols, h_ref, gain_ref, w_ref, hgain_ref, out_ref, xn_ref):
    _norm_rows_into(xn_ref, 0, h_ref, h_ref.shape[0], gain_ref[...])
    for c in range(w_ref.shape[1] // QKV_COL_TILE):
        acc = _dot(xn_ref[...], w_ref[:, c * QKV_COL_TILE:(c + 1) * QKV_COL_TILE])
        for hd in range(QKV_COL_TILE // HEAD_DIM):
            lo = hd * HEAD_DIM
            cols = slice(c * QKV_COL_TILE + lo, c * QKV_COL_TILE + lo + HEAD_DIM)
            head = acc[:, lo:lo + HEAD_DIM]
            if c * QKV_COL_TILE + lo < n_norm_cols:
                head = _rmsnorm_rows(head, hgain_ref[:, cols])
            out_ref[:, cols] = head.astype(_BF16)


def _qkv_proj(h, gain, w_qkv, layer, head_gain_cols, tm):
    m, d = h.shape
    n_out = w_qkv.shape[2]
    return pl.pallas_call(
        functools.partial(_qkv_kernel, D_MODEL + KV_DIM),
        grid=(m // tm,),
        in_specs=[
            pl.BlockSpec((tm, d), lambda i: (i, 0)),
            pl.BlockSpec((1, d), lambda i: (0, 0)),
            pl.BlockSpec((None, d, n_out), lambda i: (layer, 0, 0), pipeline_mode=pl.Buffered(1)),
            pl.BlockSpec((1, n_out), lambda i: (0, 0)),
        ],
        out_specs=pl.BlockSpec((tm, n_out), lambda i: (i, 0)),
        out_shape=jax.ShapeDtypeStruct((m, n_out), _BF16),
        scratch_shapes=[pltpu.VMEM((tm, d), _BF16)],
        compiler_params=pltpu.CompilerParams(
            dimension_semantics=("arbitrary",), vmem_limit_bytes=V7X_VMEM_LIMIT_BYTES),
    )(h, gain.reshape(1, d), w_qkv, head_gain_cols)


def _bias_kernel(band_codes_ref, tail_codes_ref, tbl_ref, out_ref):
    head = pl.program_id(0)

    def lookup(codes):
        acc = jnp.zeros(codes.shape, _F32)
        for code in range(N_BUCKETS + 2):
            acc = jnp.where(codes == code, tbl_ref[head, code] * LOG2_E, acc)
        return acc

    band = lookup(band_codes_ref[...])
    tails = [lookup(tail_codes_ref[v]) for v in range(3)]
    masked = jnp.full((BLOCK, BLOCK), NEG_INF, _F32)
    for variant, valid_blocks, tail in ((0, (2,), 0), (1, (1, 2), 1), (2, (0, 1, 2), 2), (3, (0, 1), 2)):
        for blk in range(3):
            cols = slice(blk * BLOCK, (blk + 1) * BLOCK)
            out_ref[variant, 0, :, cols] = band[:, cols] if blk in valid_blocks else masked
        out_ref[variant, 0, :, 3 * BLOCK:] = tails[tail]


def _attn_bias(band_codes, tail_codes, table):
    return pl.pallas_call(
        _bias_kernel,
        grid=(N_HEADS,),
        in_specs=[
            pl.BlockSpec((BLOCK, 3 * BLOCK), lambda h: (0, 0)),
            pl.BlockSpec((3, BLOCK, BLOCK), lambda h: (0, 0, 0)),
            pl.BlockSpec(memory_space=pltpu.SMEM),
        ],
        out_specs=pl.BlockSpec((N_BIAS_VARIANTS, 1, BLOCK, KEYS), lambda h: (0, h, 0, 0)),
        out_shape=jax.ShapeDtypeStruct((N_BIAS_VARIANTS, N_HEADS, BLOCK, KEYS), _F32),
    )(band_codes, tail_codes, table)


def _attn_kernel(q_ref, kp_ref, kc_ref, kn_ref, km_ref, vp_ref, vc_ref, vn_ref, vm_ref, bias_ref,
                 out_ref):
    dead = jnp.zeros((BLOCK - N_META, HEAD_DIM), _BF16)
    for g in range(N_KV_HEADS):
        kv_cols = slice(g * HEAD_DIM, (g + 1) * HEAD_DIM)
        q = jnp.concatenate(
            [q_ref[:, (g * GROUP + r) * HEAD_DIM:(g * GROUP + r + 1) * HEAD_DIM] for r in range(GROUP)],
            axis=0)
        k = jnp.concatenate(
            [kp_ref[:, kv_cols], kc_ref[:, kv_cols], kn_ref[:, kv_cols], km_ref[:, kv_cols], dead], axis=0)
        v = jnp.concatenate(
            [vp_ref[:, kv_cols], vc_ref[:, kv_cols], vn_ref[:, kv_cols], vm_ref[:, kv_cols], dead], axis=0)
        s = lax.dot_general(q, k, (((1,), (1,)), ((), ())), preferred_element_type=_F32)
        s = s + bias_ref[0, g * GROUP:(g + 1) * GROUP].reshape(GROUP * BLOCK, KEYS)
        e = jnp.exp2(s - jnp.max(s, axis=-1, keepdims=True))
        denom = jnp.sum(e, axis=-1, keepdims=True)
        o = _dot(e.astype(_BF16), v) / denom
        for r in range(GROUP):
            out_ref[:, (g * GROUP + r) * HEAD_DIM:(g * GROUP + r + 1) * HEAD_DIM] = (
                o[r * BLOCK:(r + 1) * BLOCK].astype(_BF16))

    @pl.when(pl.program_id(1) == 0)
    def _():
        out_ref[0:LEAD, :] = jnp.zeros((LEAD, D_MODEL), _BF16)


def _attention(qkv, bias, bsz, nb):
    m = qkv.shape[0]
    k_col, v_col = D_MODEL // KV_DIM, D_MODEL // KV_DIM + 1

    def block_row(b, n):
        meta_block = lax.rem(b + bsz - 1, bsz) * nb + nb - 1
        return jnp.where(n == 0, meta_block, b * nb + n - 1)

    def blk(col, shift):
        return pl.BlockSpec(
            (BLOCK, KV_DIM), lambda b, n: (block_row(b, jnp.clip(n + shift, 0, nb - 1)), col))

    def meta(col):
        return pl.BlockSpec(
            (N_META, KV_DIM), lambda b, n: (block_row(b, 0) * (BLOCK // N_META) + LEAD // N_META, col))

    return pl.pallas_call(
        _attn_kernel,
        grid=(bsz, nb),
        in_specs=[
            pl.BlockSpec((BLOCK, D_MODEL), lambda b, n: (block_row(b, n), 0)),
            blk(k_col, -1), blk(k_col, 0), blk(k_col, 1), meta(k_col),
            blk(v_col, -1), blk(v_col, 0), blk(v_col, 1), meta(v_col),
            pl.BlockSpec((1, N_HEADS, BLOCK, KEYS),
                         lambda b, n: (jnp.where(n == nb - 1, 3, jnp.minimum(n, 2)), 0, 0, 0)),
        ],
        out_specs=pl.BlockSpec((BLOCK, D_MODEL), lambda b, n: (block_row(b, n), 0)),
        out_shape=jax.ShapeDtypeStruct((m, D_MODEL), _BF16),
        compiler_params=pltpu.CompilerParams(
            dimension_semantics=("arbitrary", "arbitrary"),
            vmem_limit_bytes=V7X_VMEM_LIMIT_BYTES),
    )(qkv, qkv, qkv, qkv, qkv, qkv, qkv, qkv, qkv, bias)


def _out_proj_kernel(a_ref, w_ref, h_ref, out_ref):
    out_ref[...] = h_ref[...] + _dot(a_ref[...], w_ref[...])


def _out_proj(a, w, layer, h, tm):
    m, d = h.shape
    return pl.pallas_call(
        _out_proj_kernel,
        grid=(m // tm,),
        in_specs=[
            pl.BlockSpec((tm, d), lambda i: (i, 0)),
            pl.BlockSpec((None, d, d), lambda i: (layer, 0, 0), pipeline_mode=pl.Buffered(1)),
            pl.BlockSpec((tm, d), lambda i: (i, 0)),
        ],
        out_specs=pl.BlockSpec((tm, d), lambda i: (i, 0)),
        out_shape=jax.ShapeDtypeStruct((m, d), _F32),
        compiler_params=pltpu.CompilerParams(
            dimension_semantics=("arbitrary",), vmem_limit_bytes=V7X_VMEM_LIMIT_BYTES),
    )(a, w, h)


def _t5_bucket(rel):
    half = N_BUCKETS // 2
    max_exact = half // 2
    side = jnp.where(rel > 0, half, 0)
    n = jnp.abs(rel)
    nf = jnp.maximum(n, 1).astype(_F32)
    large = max_exact + (jnp.log(nf / max_exact) / math.log(MAX_DISTANCE / max_exact)
                         * (half - max_exact)).astype(jnp.int32)
    large = jnp.minimum(large, half - 1)
    return side + jnp.where(n < max_exact, n, large)


def _bias_codes():
    qi = jnp.arange(BLOCK, dtype=jnp.int32)[:, None]
    col = jnp.arange(3 * BLOCK, dtype=jnp.int32)[None, :]
    rel_band = col - BLOCK - qi
    band = jnp.where(jnp.abs(rel_band) <= WINDOW, _t5_bucket(rel_band), CODE_MASKED)
    tcol = jnp.arange(BLOCK, dtype=jnp.int32)[None, :]
    tails = []
    for n_rep in range(3):
        qpos = n_rep * BLOCK + qi
        codes = jnp.where(tcol < N_META, _t5_bucket(LEAD + tcol - qpos), CODE_MASKED)
        tails.append(jnp.where(tcol == N_META, CODE_SINK, codes))
    return band.astype(jnp.int32), jnp.stack(tails).astype(jnp.int32)


def kernel(x, meta_tokens, rel_bias_table, norm_mix, norm_ffn, conv_in_w, conv_dw, conv_out_w,
           attn_qkv, attn_q_gain, attn_k_gain, attn_sink, attn_o, ffn_up, ffn_dw, ffn_dw_b, ffn_down):
    bsz, seq, d = x.shape
    tp = seq + BLOCK
    nb = tp // BLOCK
    tm = tp // FFN_ROW_TILES_PER_SLAB
    assert d == D_MODEL and seq % BLOCK == 0 and nb >= 4

    bf = lambda w: w.astype(_BF16)
    ffn_up_bf, ffn_down_bf = bf(ffn_up), bf(ffn_down)

    h = _mixer_from_tokens(x, meta_tokens.astype(x.dtype), norm_mix[0], bf(conv_in_w), conv_dw,
                           bf(conv_out_w), 0, tp // MIXER_ROW_TILES_PER_SLAB)
    h = _conv_ffn(h, norm_ffn[0], ffn_up_bf, ffn_dw, ffn_dw_b, ffn_down_bf, 0, tm, tp)

    q_scale = HEAD_DIM ** -0.5 * LOG2_E
    head_gain_cols = jnp.concatenate([
        jnp.tile(attn_q_gain[0] * q_scale, N_HEADS), jnp.tile(attn_k_gain[0], N_KV_HEADS),
        jnp.ones((KV_DIM,), _F32)]).reshape(1, QKV_DIM)
    h2 = h.reshape(bsz * tp, d)
    qkv = _qkv_proj(h2, norm_mix[1], bf(attn_qkv), 0, head_gain_cols, tm)
    table = jnp.concatenate([
        rel_bias_table.T.astype(_F32), jnp.full((N_HEADS, 1), NEG_INF, _F32),
        attn_sink[0].astype(_F32)[:, None]], axis=1)
    bias = _attn_bias(*_bias_codes(), table)
    o = _attention(qkv, bias, bsz, nb)
    h2 = _out_proj(o, bf(attn_o), 0, h2, tm // 2)
    return _conv_ffn(h2.reshape(bsz, tp, d), norm_ffn[1], ffn_up_bf, ffn_dw, ffn_dw_b, ffn_down_bf,
                     1, tm, seq)
```

```python
import functools
import math

import jax
import jax.numpy as jnp
from jax import lax
from jax.experimental import pallas as pl
from jax.experimental.pallas import tpu as pltpu

D_MODEL = 2048
N_META = 16
N_HEADS = 16
N_KV_HEADS = 4
HEAD_DIM = D_MODEL // N_HEADS
GROUP = N_HEADS // N_KV_HEADS
KV_DIM = N_KV_HEADS * HEAD_DIM
QKV_DIM = D_MODEL + 2 * KV_DIM
WINDOW = 128
BLOCK = 128
N_BUCKETS = 32
MAX_DISTANCE = 128
D_FF = 5632
EPS = 1e-6
LEAD = BLOCK - N_META

HALO = 16
NORM_ROWS = 32
FFN_ROW_TILES_PER_SLAB = 4
MIXER_ROW_TILES_PER_SLAB = 4
MIXER_COL_TILE = 512
FFN_COL_TILE = 512
QKV_COL_TILE = 512
KEYS = 4 * BLOCK
SINK_COL = 3 * BLOCK + N_META
CODE_MASKED = N_BUCKETS
CODE_SINK = N_BUCKETS + 1
N_BIAS_VARIANTS = 4
NEG_INF = -1e30
LOG2_E = math.log2(math.e)
V7X_VMEM_LIMIT_BYTES = 56 * 1024 * 1024

_BF16 = jnp.bfloat16
_F32 = jnp.float32


def _dot(a, b):
    return jnp.dot(a, b, preferred_element_type=_F32)


def _rmsnorm_rows(v, gain):
    return v * lax.rsqrt(jnp.mean(v * v, axis=-1, keepdims=True) + EPS) * gain


def _norm_rows_into(xext_ref, row0, src_ref, n_rows, gain, copy_ref=None):
    for r in range(0, n_rows, NORM_ROWS):
        rows = min(NORM_ROWS, n_rows - r)
        v = src_ref[r:r + rows, :]
        xext_ref[row0 + r:row0 + r + rows, :] = _rmsnorm_rows(v, gain).astype(_BF16)
        if copy_ref is not None:
            copy_ref[r:r + rows, :] = v


def _gelu_exact(v):
    return 0.5 * v * (1.0 + lax.erf(v * math.sqrt(0.5)))


def _row_tile_index(i, tiles):
    return lax.div(i, tiles), lax.rem(i, tiles)


def _prefetch_row_tiles(start_copy, wait_copy):
    i, j = pl.program_id(0), pl.program_id(1)

    @pl.when(jnp.logical_and(i == 0, j == 0))
    def _():
        start_copy(i)

    @pl.when(j == 0)
    def _():
        wait_copy(i)

    @pl.when(jnp.logical_and(j == 1, i + 1 < pl.num_programs(0)))
    def _():
        start_copy(i + 1)


def _slab_tile_copy(h_hbm, hbuf_ref, sem, tm, t):
    b, q = _row_tile_index(t, h_hbm.shape[1] // tm)
    return pltpu.make_async_copy(h_hbm.at[b, pl.ds(q * tm, tm), :], hbuf_ref, sem.at[0])


def _token_tile_copy(op, x_hbm, hbuf_ref, sem, tm, tiles, tail_tokens, t):
    b, q = _row_tile_index(t, tiles)

    @pl.when(q < tiles - 1)
    def _():
        op(pltpu.make_async_copy(x_hbm.at[b, pl.ds(q * tm, tm), :], hbuf_ref, sem.at[0]))

    @pl.when(q == tiles - 1)
    def _():
        op(pltpu.make_async_copy(x_hbm.at[b, pl.ds((tiles - 1) * tm, tail_tokens), :],
                                 hbuf_ref.at[pl.ds(0, tail_tokens), :], sem.at[0]))


def _fill_from_slab(hprev_ref, hbuf_ref, hnext_ref, gain_ref, xext_ref, out_ref, tm):
    gain = gain_ref[...]
    xext_ref[0:HALO, :] = _rmsnorm_rows(hprev_ref[0], gain).astype(_BF16)
    _norm_rows_into(xext_ref, HALO, hbuf_ref, tm, gain, out_ref.at[0])
    xext_ref[HALO + tm:, :] = _rmsnorm_rows(hnext_ref[0], gain).astype(_BF16)


def _fill_from_tokens(xprev_ref, hbuf_ref, xnext_ref, meta_ref, gain_ref, xext_ref, out_ref, tm,
                      tiles, tail_tokens):
    q = lax.rem(pl.program_id(0), tiles)
    gain = gain_ref[...]
    meta = meta_ref[...]
    meta_n = _rmsnorm_rows(meta, gain).astype(_BF16)

    @pl.when(q == 0)
    def _():
        xext_ref[0:HALO, :] = meta_n

    @pl.when(q > 0)
    def _():
        xext_ref[0:HALO, :] = _rmsnorm_rows(xprev_ref[0], gain).astype(_BF16)

    @pl.when(q < tiles - 1)
    def _():
        _norm_rows_into(xext_ref, HALO, hbuf_ref, tm, gain, out_ref.at[0])

    @pl.when(q == tiles - 1)
    def _():
        _norm_rows_into(xext_ref, HALO, hbuf_ref, tail_tokens, gain, out_ref.at[0])
        xext_ref[HALO + tail_tokens:HALO + tail_tokens + LEAD, :] = jnp.zeros((LEAD, D_MODEL), _BF16)
        xext_ref[HALO + tail_tokens + LEAD:HALO + tm, :] = meta_n
        out_ref[0, tail_tokens:tail_tokens + LEAD, :] = jnp.zeros((LEAD, D_MODEL), _F32)
        out_ref[0, tail_tokens + LEAD:tm, :] = meta

    xext_ref[HALO + tm:, :] = _rmsnorm_rows(xnext_ref[0], gain).astype(_BF16)


def _conv3(v_ext, cw_ref, tm):
    w = cw_ref[...]
    return (v_ext[HALO - 1:HALO - 1 + tm] * w[0:1]
            + v_ext[HALO:HALO + tm] * w[1:2]
            + v_ext[HALO + 1:HALO + 1 + tm] * w[2:3])


def _mixer_kernel(tm, tiles, tail_tokens, xprev_ref, x_hbm, xnext_ref, meta_ref, gain_ref, wb_ref, wc_ref,
                  wh_ref, cw_ref, wo_ref, out_ref, xext_ref, hbuf_ref, sem):
    copy = functools.partial(_token_tile_copy, x_hbm=x_hbm, hbuf_ref=hbuf_ref, sem=sem, tm=tm,
                             tiles=tiles, tail_tokens=tail_tokens)
    _prefetch_row_tiles(lambda t: copy(lambda c: c.start(), t=t), lambda t: copy(lambda c: c.wait(), t=t))

    @pl.when(pl.program_id(1) == 0)
    def _():
        _fill_from_tokens(xprev_ref, hbuf_ref, xnext_ref, meta_ref, gain_ref, xext_ref, out_ref, tm,
                          tiles, tail_tokens)

    x = xext_ref[...]
    ch = _conv3(_dot(x, wc_ref[...]) * _dot(x, wh_ref[...]), cw_ref, tm)
    y = _dot(xext_ref[HALO:HALO + tm, :], wb_ref[...]) * ch
    out_ref[0] += _dot(y.astype(_BF16), wo_ref[...])


def _ffn_kernel(tm, hprev_ref, h_hbm, hnext_ref, gain_ref, wg_ref, wu_ref, cw_ref, cb_ref,
                wd_ref, out_ref, xext_ref, hbuf_ref, sem):
    _prefetch_row_tiles(lambda t: _slab_tile_copy(h_hbm, hbuf_ref, sem, tm, t).start(),
                        lambda t: _slab_tile_copy(h_hbm, hbuf_ref, sem, tm, t).wait())

    @pl.when(pl.program_id(1) == 0)
    def _():
        _fill_from_slab(hprev_ref, hbuf_ref, hnext_ref, gain_ref, xext_ref, out_ref, tm)

    g = _conv3(_dot(xext_ref[...], wg_ref[...]), cw_ref, tm) + cb_ref[...]
    u = _dot(xext_ref[HALO:HALO + tm, :], wu_ref[...])
    a = _gelu_exact(g) * u
    out_ref[0] += _dot(a.astype(_BF16), wd_ref[...])


def _fused_params():
    return pltpu.CompilerParams(dimension_semantics=("arbitrary", "arbitrary"),
                                vmem_limit_bytes=V7X_VMEM_LIMIT_BYTES)


def _fused_scratch(tm, d):
    return [pltpu.VMEM((tm + 2 * HALO, d), _BF16), pltpu.VMEM((tm, d), _F32),
            pltpu.SemaphoreType.DMA((1,))]


def _weight_specs(layer, d, hidden, n_up, col_tile):
    n_j = hidden // col_tile
    ups = [pl.BlockSpec((None, d, col_tile),
                        functools.partial(lambda i, j, k: (layer, 0, j + k * n_j), k=k))
           for k in range(n_up)]
    down = pl.BlockSpec((None, col_tile, d), lambda i, j: (layer, j, 0))
    return ups, down


def _mixer_from_tokens(x, meta_tokens, gain, w_in, conv_w, w_out, layer, tm):
    bsz, seq, d = x.shape
    tp = seq + BLOCK
    tiles = tp // tm
    tail_tokens = seq - (tiles - 1) * tm
    assert tiles * tm == tp and 0 < tail_tokens <= tm and tail_tokens % HALO == 0 and tm % HALO == 0
    halo_blocks_per_tile = tm // HALO
    col_tile = MIXER_COL_TILE

    def prev_map(i, j):
        b, q = _row_tile_index(i, tiles)
        return b, jnp.maximum(q * halo_blocks_per_tile - 1, 0), 0

    def next_map(i, j):
        b, q = _row_tile_index(i, tiles)
        wrap = q == tiles - 1
        return (jnp.where(wrap, lax.rem(b + 1, bsz), b),
                jnp.where(wrap, 0, (q + 1) * halo_blocks_per_tile), 0)

    ups, down = _weight_specs(layer, d, d, 3, col_tile)
    in_specs = [
        pl.BlockSpec((1, HALO, d), prev_map),
        pl.BlockSpec(memory_space=pl.ANY),
        pl.BlockSpec((1, HALO, d), next_map),
        pl.BlockSpec((N_META, d), lambda i, j: (0, 0)),
        pl.BlockSpec((1, d), lambda i, j: (0, 0)),
        *ups,
        pl.BlockSpec((None, 3, col_tile), lambda i, j: (layer, 0, j)),
        down,
    ]
    return pl.pallas_call(
        functools.partial(_mixer_kernel, tm, tiles, tail_tokens),
        grid=(bsz * tiles, d // col_tile),
        in_specs=in_specs,
        out_specs=pl.BlockSpec((1, tm, d), lambda i, j: (*_row_tile_index(i, tiles), 0)),
        out_shape=jax.ShapeDtypeStruct((bsz, tp, d), _F32),
        scratch_shapes=_fused_scratch(tm, d),
        compiler_params=_fused_params(),
    )(x, x, x, meta_tokens, gain.reshape(1, d), w_in, w_in, w_in, conv_w, w_out)


def _conv_ffn(h, gain, w_up, conv_w, conv_b, w_down, layer, tm, out_rows):
    bsz, tp, d = h.shape
    hidden = w_down.shape[1]
    tiles = tp // tm
    assert tiles * tm == tp and tm % HALO == 0
    halo_blocks_per_tile = tm // HALO
    halo_blocks_per_slab = tp // HALO
    n_halo_blocks = bsz * halo_blocks_per_slab
    col_tile = FFN_COL_TILE

    def halo_map(i, j, offset):
        blk = lax.rem(i * halo_blocks_per_tile + offset + n_halo_blocks, n_halo_blocks)
        return lax.div(blk, halo_blocks_per_slab), lax.rem(blk, halo_blocks_per_slab), 0

    ups, down = _weight_specs(layer, d, hidden, 2, col_tile)
    in_specs = [
        pl.BlockSpec((1, HALO, d), functools.partial(halo_map, offset=-1)),
        pl.BlockSpec(memory_space=pl.ANY),
        pl.BlockSpec((1, HALO, d), functools.partial(halo_map, offset=halo_blocks_per_tile)),
        pl.BlockSpec((1, d), lambda i, j: (0, 0)),
        *ups,
        pl.BlockSpec((None, 3, col_tile), lambda i, j: (layer, 0, j)),
        pl.BlockSpec((None, 1, col_tile), lambda i, j: (layer, 0, j)),
        down,
    ]
    return pl.pallas_call(
        functools.partial(_ffn_kernel, tm),
        grid=(bsz * tiles, hidden // col_tile),
        in_specs=in_specs,
        out_specs=pl.BlockSpec((1, tm, d), lambda i, j: (*_row_tile_index(i, tiles), 0)),
        out_shape=jax.ShapeDtypeStruct((bsz, out_rows, d), _F32),
        scratch_shapes=_fused_scratch(tm, d),
        compiler_params=_fused_params(),
    )(h, h, h, gain.reshape(1, d), w_up, w_up, conv_w, conv_b.reshape(conv_b.shape[0], 1, hidden), w_down)


def _qkv_kernel(n_norm_cols, h_ref, gain_ref, w_ref, hgain_ref, out_ref, xn_ref):
    _norm_rows_into(xn_ref, 0, h_ref, h_ref.shape[0], gain_ref[...])
    for c in range(w_ref.shape[1] // QKV_COL_TILE):
        acc = _dot(xn_ref[...], w_ref[:, c * QKV_COL_TILE:(c + 1) * QKV_COL_TILE])
        for hd in range(QKV_COL_TILE // HEAD_DIM):
            lo = hd * HEAD_DIM
            cols = slice(c * QKV_COL_TILE + lo, c * QKV_COL_TILE + lo + HEAD_DIM)
            head = acc[:, lo:lo + HEAD_DIM]
            if c * QKV_COL_TILE + lo < n_norm_cols:
                head = _rmsnorm_rows(head, hgain_ref[:, cols])
            out_ref[:, cols] = head.astype(_BF16)


def _qkv_proj(h, gain, w_qkv, layer, head_gain_cols, tm):
    m, d = h.shape
    n_out = w_qkv.shape[2]
    return pl.pallas_call(
        functools.partial(_qkv_kernel, D_MODEL + KV_DIM),
        grid=(m // tm,),
        in_specs=[
            pl.BlockSpec((tm, d), lambda i: (i, 0)),
            pl.BlockSpec((1, d), lambda i: (0, 0)),
            pl.BlockSpec((None, d, n_out), lambda i: (layer, 0, 0), pipeline_mode=pl.Buffered(1)),
            pl.BlockSpec((1, n_out), lambda i: (0, 0)),
        ],
        out_specs=pl.BlockSpec((tm, n_out), lambda i: (i, 0)),
        out_shape=jax.ShapeDtypeStruct((m, n_out), _BF16),
        scratch_shapes=[pltpu.VMEM((tm, d), _BF16)],
        compiler_params=pltpu.CompilerParams(
            dimension_semantics=("arbitrary",), vmem_limit_bytes=V7X_VMEM_LIMIT_BYTES),
    )(h, gain.reshape(1, d), w_qkv, head_gain_cols)


def _bias_kernel(band_codes_ref, tail_codes_ref, tbl_ref, out_ref):
    head = pl.program_id(0)

    def lookup(codes):
        acc = jnp.zeros(codes.shape, _F32)
        for code in range(N_BUCKETS + 2):
            acc = jnp.where(codes == code, tbl_ref[head, code] * LOG2_E, acc)
        return acc

    band = lookup(band_codes_ref[...])
    tails = [lookup(tail_codes_ref[v]) for v in range(3)]
    masked = jnp.full((BLOCK, BLOCK), NEG_INF, _F32)
    for variant, valid_blocks, tail in ((0, (2,), 0), (1, (1, 2), 1), (2, (0, 1, 2), 2), (3, (0, 1), 2)):
        for blk in range(3):
            cols = slice(blk * BLOCK, (blk + 1) * BLOCK)
            out_ref[variant, 0, :, cols] = band[:, cols] if blk in valid_blocks else masked
        out_ref[variant, 0, :, 3 * BLOCK:] = tails[tail]


def _attn_bias(band_codes, tail_codes, table):
    return pl.pallas_call(
        _bias_kernel,
        grid=(N_HEADS,),
        in_specs=[
            pl.BlockSpec((BLOCK, 3 * BLOCK), lambda h: (0, 0)),
            pl.BlockSpec((3, BLOCK, BLOCK), lambda h: (0, 0, 0)),
            pl.BlockSpec(memory_space=pltpu.SMEM),
        ],
        out_specs=pl.BlockSpec((N_BIAS_VARIANTS, 1, BLOCK, KEYS), lambda h: (0, h, 0, 0)),
        out_shape=jax.ShapeDtypeStruct((N_BIAS_VARIANTS, N_HEADS, BLOCK, KEYS), _F32),
    )(band_codes, tail_codes, table)


def _attn_chain(q_ref, k_refs, v_refs, bias_ref, out_ref, slab, k_slabs, g):
    kv_cols = slice(g * HEAD_DIM, (g + 1) * HEAD_DIM)
    dead = jnp.zeros((BLOCK - N_META, HEAD_DIM), _BF16)
    q = jnp.concatenate(
        [q_ref[slab, :, (g * GROUP + r) * HEAD_DIM:(g * GROUP + r + 1) * HEAD_DIM] for r in range(GROUP)],
        axis=0)
    k = jnp.concatenate([ref[ks, :, kv_cols] for ref, ks in zip(k_refs, k_slabs)] + [dead], axis=0)
    v = jnp.concatenate([ref[ks, :, kv_cols] for ref, ks in zip(v_refs, k_slabs)] + [dead], axis=0)
    s = lax.dot_general(q, k, (((1,), (1,)), ((), ())), preferred_element_type=_F32)
    s = s + bias_ref[0, g * GROUP:(g + 1) * GROUP].reshape(GROUP * BLOCK, KEYS)
    e = jnp.exp2(s - jnp.max(s, axis=-1, keepdims=True))
    denom = jnp.sum(e, axis=-1, keepdims=True)
    o = _dot(e.astype(_BF16), v) / denom
    for r in range(GROUP):
        out_ref[slab, :, (g * GROUP + r) * HEAD_DIM:(g * GROUP + r + 1) * HEAD_DIM] = (
            o[r * BLOCK:(r + 1) * BLOCK].astype(_BF16))


def _attn_kernel(q_ref, kp_ref, kc_ref, kn_ref, km_ref, vp_ref, vc_ref, vn_ref, vm_ref, bias_ref,
                 out_ref):
    n = pl.program_id(0)
    n_slabs = q_ref.shape[0]
    k_refs = (kp_ref, kc_ref, kn_ref, km_ref)
    v_refs = (vp_ref, vc_ref, vn_ref, vm_ref)

    def all_chains(first_block):
        for slab in range(n_slabs):
            if first_block:
                k_slabs = (slab, slab, (slab + 1) % n_slabs, slab)
            else:
                k_slabs = (slab, slab, slab, (slab - 1) % n_slabs)
            for g in range(N_KV_HEADS):
                _attn_chain(q_ref, k_refs, v_refs, bias_ref, out_ref, slab, k_slabs, g)

    @pl.when(n == 0)
    def _():
        all_chains(True)
        out_ref[:, 0:LEAD, :] = jnp.zeros((n_slabs, LEAD, D_MODEL), _BF16)

    @pl.when(n > 0)
    def _():
        all_chains(False)


def _attention(qkv, bias, nb):
    bsz = qkv.shape[0]
    k_col, v_col = D_MODEL // KV_DIM, D_MODEL // KV_DIM + 1

    def slab_block(n):
        return jnp.where(n == 0, nb - 1, n - 1)

    def blk(col, shift):
        return pl.BlockSpec(
            (bsz, BLOCK, KV_DIM), lambda n: (0, slab_block(jnp.clip(n + shift, 0, nb - 1)), col))

    def meta(col):
        return pl.BlockSpec(
            (bsz, N_META, KV_DIM), lambda n: (0, (nb - 1) * (BLOCK // N_META) + LEAD // N_META, col))

    return pl.pallas_call(
        _attn_kernel,
        grid=(nb,),
        in_specs=[
            pl.BlockSpec((bsz, BLOCK, D_MODEL), lambda n: (0, slab_block(n), 0)),
            blk(k_col, -1), blk(k_col, 0), blk(k_col, 1), meta(k_col),
            blk(v_col, -1), blk(v_col, 0), blk(v_col, 1), meta(v_col),
            pl.BlockSpec((1, N_HEADS, BLOCK, KEYS),
                         lambda n: (jnp.where(n == nb - 1, 3, jnp.minimum(n, 2)), 0, 0, 0)),
        ],
        out_specs=pl.BlockSpec((bsz, BLOCK, D_MODEL), lambda n: (0, slab_block(n), 0)),
        out_shape=jax.ShapeDtypeStruct((bsz, nb * BLOCK, D_MODEL), _BF16),
        compiler_params=pltpu.CompilerParams(
            dimension_semantics=("arbitrary",), vmem_limit_bytes=V7X_VMEM_LIMIT_BYTES),
    )(qkv, qkv, qkv, qkv, qkv, qkv, qkv, qkv, qkv, bias)


def _out_proj_kernel(a_ref, w_ref, h_ref, out_ref):
    out_ref[...] = h_ref[...] + _dot(a_ref[...], w_ref[...])


def _out_proj(a, w, layer, h, tm):
    m, d = h.shape
    return pl.pallas_call(
        _out_proj_kernel,
        grid=(m // tm,),
        in_specs=[
            pl.BlockSpec((tm, d), lambda i: (i, 0)),
            pl.BlockSpec((None, d, d), lambda i: (layer, 0, 0), pipeline_mode=pl.Buffered(1)),
            pl.BlockSpec((tm, d), lambda i: (i, 0)),
        ],
        out_specs=pl.BlockSpec((tm, d), lambda i: (i, 0)),
        out_shape=jax.ShapeDtypeStruct((m, d), _F32),
        compiler_params=pltpu.CompilerParams(
            dimension_semantics=("arbitrary",), vmem_limit_bytes=V7X_VMEM_LIMIT_BYTES),
    )(a, w, h)


def _t5_bucket(rel):
    half = N_BUCKETS // 2
    max_exact = half // 2
    side = jnp.where(rel > 0, half, 0)
    n = jnp.abs(rel)
    nf = jnp.maximum(n, 1).astype(_F32)
    large = max_exact + (jnp.log(nf / max_exact) / math.log(MAX_DISTANCE / max_exact)
                         * (half - max_exact)).astype(jnp.int32)
    large = jnp.minimum(large, half - 1)
    return side + jnp.where(n < max_exact, n, large)


def _bias_codes():
    qi = jnp.arange(BLOCK, dtype=jnp.int32)[:, None]
    col = jnp.arange(3 * BLOCK, dtype=jnp.int32)[None, :]
    rel_band = col - BLOCK - qi
    band = jnp.where(jnp.abs(rel_band) <= WINDOW, _t5_bucket(rel_band), CODE_MASKED)
    tcol = jnp.arange(BLOCK, dtype=jnp.int32)[None, :]
    tails = []
    for n_rep in range(3):
        qpos = n_rep * BLOCK + qi
        codes = jnp.where(tcol < N_META, _t5_bucket(LEAD + tcol - qpos), CODE_MASKED)
        tails.append(jnp.where(tcol == N_META, CODE_SINK, codes))
    return band.astype(jnp.int32), jnp.stack(tails).astype(jnp.int32)


def kernel(x, meta_tokens, rel_bias_table, norm_mix, norm_ffn, conv_in_w, conv_dw, conv_out_w,
           attn_qkv, attn_q_gain, attn_k_gain, attn_sink, attn_o, ffn_up, ffn_dw, ffn_dw_b, ffn_down):
    bsz, seq, d = x.shape
    tp = seq + BLOCK
    nb = tp // BLOCK
    tm = tp // FFN_ROW_TILES_PER_SLAB
    assert d == D_MODEL and seq % BLOCK == 0 and nb >= 4

    bf = lambda w: w.astype(_BF16)
    ffn_up_bf, ffn_down_bf = bf(ffn_up), bf(ffn_down)

    h = _mixer_from_tokens(x, meta_tokens.astype(x.dtype), norm_mix[0], bf(conv_in_w), conv_dw,
                           bf(conv_out_w), 0, tp // MIXER_ROW_TILES_PER_SLAB)
    h = _conv_ffn(h, norm_ffn[0], ffn_up_bf, ffn_dw, ffn_dw_b, ffn_down_bf, 0, tm, tp)

    q_scale = HEAD_DIM ** -0.5 * LOG2_E
    head_gain_cols = jnp.concatenate([
        jnp.tile(attn_q_gain[0] * q_scale, N_HEADS), jnp.tile(attn_k_gain[0], N_KV_HEADS),
        jnp.ones((KV_DIM,), _F32)]).reshape(1, QKV_DIM)
    h2 = h.reshape(bsz * tp, d)
    qkv = _qkv_proj(h2, norm_mix[1], bf(attn_qkv), 0, head_gain_cols, tm)
    table = jnp.concatenate([
        rel_bias_table.T.astype(_F32), jnp.full((N_HEADS, 1), NEG_INF, _F32),
        attn_sink[0].astype(_F32)[:, None]], axis=1)
    bias = _attn_bias(*_bias_codes(), table)
    o = _attention(qkv.reshape(bsz, tp, QKV_DIM), bias, nb).reshape(bsz * tp, d)
    h2 = _out_proj(o, bf(attn_o), 0, h2, tm // 2)
    return _conv_ffn(h2.reshape(bsz, tp, d), norm_ffn[1], ffn_up_bf, ffn_dw, ffn_dw_b, ffn_down_bf,
                     1, tm, seq)
```

```python
import functools
import math

import jax
import jax.numpy as jnp
from jax import lax
from jax.experimental import pallas as pl
from jax.experimental.pallas import tpu as pltpu

D_MODEL = 2048
N_META = 16
N_HEADS = 16
N_KV_HEADS = 4
HEAD_DIM = D_MODEL // N_HEADS
GROUP = N_HEADS // N_KV_HEADS
KV_DIM = N_KV_HEADS * HEAD_DIM
QKV_DIM = D_MODEL + 2 * KV_DIM
WINDOW = 128
BLOCK = 128
N_BUCKETS = 32
MAX_DISTANCE = 128
D_FF = 5632
EPS = 1e-6
LEAD = BLOCK - N_META

HALO = 16
NORM_ROWS = 32
FFN_ROW_TILES_PER_SLAB = 4
MIXER_ROW_TILES_PER_SLAB = 4
QKV_ROW_TILES_PER_SLAB = 4
OUT_PROJ_ROW_TILES_PER_SLAB = 8
MIXER_COL_TILE = 512
FFN_COL_TILE = 512
QKV_COL_TILE = 512
KEYS = 4 * BLOCK
SINK_COL = 3 * BLOCK + N_META
CODE_MASKED = N_BUCKETS
CODE_SINK = N_BUCKETS + 1
N_BIAS_VARIANTS = 4
NEG_INF = -1e30
LOG2_E = math.log2(math.e)
V7X_VMEM_LIMIT_BYTES = 56 * 1024 * 1024

_BF16 = jnp.bfloat16
_F32 = jnp.float32


def _dot(a, b):
    return jnp.dot(a, b, preferred_element_type=_F32)


def _rmsnorm_rows(v, gain):
    return v * lax.rsqrt(jnp.mean(v * v, axis=-1, keepdims=True) + EPS) * gain


def _norm_rows_into(xext_ref, row0, src_ref, n_rows, gain):
    for r in range(0, n_rows, NORM_ROWS):
        rows = min(NORM_ROWS, n_rows - r)
        xext_ref[row0 + r:row0 + r + rows, :] = (
            _rmsnorm_rows(src_ref[r:r + rows, :], gain).astype(_BF16))


def _gelu_exact(v):
    return 0.5 * v * (1.0 + lax.erf(v * math.sqrt(0.5)))


def _row_tile_index(i, tiles):
    return lax.div(i, tiles), lax.rem(i, tiles)


def _tile_copy(op, hbm, acc_ref, sems, tm, tiles, tail_rows, to_hbm, t):
    b, q = _row_tile_index(t, tiles)
    slot = lax.rem(t, 2)

    def copy(rows):
        hbm_rows = hbm.at[b, pl.ds(q * tm, rows), :]
        acc_rows = acc_ref.at[slot, pl.ds(0, rows), :]
        src, dst = (acc_rows, hbm_rows) if to_hbm else (hbm_rows, acc_rows)
        op(pltpu.make_async_copy(src, dst, sems.at[slot]))

    if tail_rows == tm:
        copy(tm)
    else:
        pl.when(q < tiles - 1)(lambda: copy(tm))
        pl.when(q == tiles - 1)(lambda: copy(tail_rows))


def _stream_row_tiles(load, store, body):
    i, j = pl.program_id(0), pl.program_id(1)
    n_i, n_j = pl.num_programs(0), pl.num_programs(1)
    start, wait = (lambda c: c.start()), (lambda c: c.wait())

    pl.when(jnp.logical_and(i == 0, j == 0))(lambda: load(start, t=i))
    pl.when(j == 0)(lambda: load(wait, t=i))
    pl.when(jnp.logical_and(j == 1, i >= 1))(lambda: store(wait, t=i - 1))
    pl.when(jnp.logical_and(j == 1, i + 1 < n_i))(lambda: load(start, t=i + 1))
    body()
    pl.when(j == n_j - 1)(lambda: store(start, t=i))
    pl.when(jnp.logical_and(j == n_j - 1, i == n_i - 1))(lambda: store(wait, t=i))


def _fill_from_slab(hprev_ref, acc, hnext_ref, gain_ref, xext_ref, tm):
    gain = gain_ref[...]
    xext_ref[0:HALO, :] = _rmsnorm_rows(hprev_ref[0], gain).astype(_BF16)
    _norm_rows_into(xext_ref, HALO, acc, tm, gain)
    xext_ref[HALO + tm:, :] = _rmsnorm_rows(hnext_ref[0], gain).astype(_BF16)


def _fill_from_tokens(xprev_ref, acc, xnext_ref, meta_ref, gain_ref, xext_ref, tm, tiles, tail_tokens):
    q = lax.rem(pl.program_id(0), tiles)
    gain = gain_ref[...]
    meta = meta_ref[...]
    meta_n = _rmsnorm_rows(meta, gain).astype(_BF16)

    @pl.when(q == 0)
    def _():
        xext_ref[0:HALO, :] = meta_n

    @pl.when(q > 0)
    def _():
        xext_ref[0:HALO, :] = _rmsnorm_rows(xprev_ref[0], gain).astype(_BF16)

    @pl.when(q < tiles - 1)
    def _():
        _norm_rows_into(xext_ref, HALO, acc, tm, gain)

    @pl.when(q == tiles - 1)
    def _():
        _norm_rows_into(xext_ref, HALO, acc, tail_tokens, gain)
        xext_ref[HALO + tail_tokens:HALO + tail_tokens + LEAD, :] = jnp.zeros((LEAD, D_MODEL), _BF16)
        xext_ref[HALO + tail_tokens + LEAD:HALO + tm, :] = meta_n
        acc[tail_tokens:tail_tokens + LEAD, :] = jnp.zeros((LEAD, D_MODEL), _F32)
        acc[tail_tokens + LEAD:tm, :] = meta

    xext_ref[HALO + tm:, :] = _rmsnorm_rows(xnext_ref[0], gain).astype(_BF16)


def _conv3(v_ext, cw_ref, tm):
    w = cw_ref[...]
    return (v_ext[HALO - 1:HALO - 1 + tm] * w[0:1]
            + v_ext[HALO:HALO + tm] * w[1:2]
            + v_ext[HALO + 1:HALO + 1 + tm] * w[2:3])


def _mixer_kernel(tm, tiles, tail_tokens, xprev_ref, x_hbm, xnext_ref, meta_ref, gain_ref, wb_ref, wc_ref,
                  wh_ref, cw_ref, wo_ref, out_hbm, xext_ref, acc_ref, in_sems, out_sems):
    acc = acc_ref.at[lax.rem(pl.program_id(0), 2)]

    def body():
        @pl.when(pl.program_id(1) == 0)
        def _():
            _fill_from_tokens(xprev_ref, acc, xnext_ref, meta_ref, gain_ref, xext_ref, tm, tiles,
                              tail_tokens)

        x = xext_ref[...]
        ch = _conv3(_dot(x, wc_ref[...]) * _dot(x, wh_ref[...]), cw_ref, tm)
        y = _dot(xext_ref[HALO:HALO + tm, :], wb_ref[...]) * ch
        acc[...] += _dot(y.astype(_BF16), wo_ref[...])

    _stream_row_tiles(
        functools.partial(_tile_copy, hbm=x_hbm, acc_ref=acc_ref, sems=in_sems, tm=tm, tiles=tiles,
                          tail_rows=tail_tokens, to_hbm=False),
        functools.partial(_tile_copy, hbm=out_hbm, acc_ref=acc_ref, sems=out_sems, tm=tm, tiles=tiles,
                          tail_rows=tm, to_hbm=True),
        body)


def _ffn_kernel(tm, tiles, out_tail_rows, hprev_ref, h_hbm, hnext_ref, gain_ref, wg_ref, wu_ref, cw_ref,
                cb_ref, wd_ref, out_hbm, xext_ref, acc_ref, in_sems, out_sems):
    acc = acc_ref.at[lax.rem(pl.program_id(0), 2)]

    def body():
        @pl.when(pl.program_id(1) == 0)
        def _():
            _fill_from_slab(hprev_ref, acc, hnext_ref, gain_ref, xext_ref, tm)

        g = _conv3(_dot(xext_ref[...], wg_ref[...]), cw_ref, tm) + cb_ref[...]
        u = _dot(xext_ref[HALO:HALO + tm, :], wu_ref[...])
        a = _gelu_exact(g) * u
        acc[...] += _dot(a.astype(_BF16), wd_ref[...])

    _stream_row_tiles(
        functools.partial(_tile_copy, hbm=h_hbm, acc_ref=acc_ref, sems=in_sems, tm=tm, tiles=tiles,
                          tail_rows=tm, to_hbm=False),
        functools.partial(_tile_copy, hbm=out_hbm, acc_ref=acc_ref, sems=out_sems, tm=tm, tiles=tiles,
                          tail_rows=out_tail_rows, to_hbm=True),
        body)


def _fused_params():
    return pltpu.CompilerParams(dimension_semantics=("arbitrary", "arbitrary"),
                                vmem_limit_bytes=V7X_VMEM_LIMIT_BYTES)


def _fused_scratch(tm, d):
    return [pltpu.VMEM((tm + 2 * HALO, d), _BF16), pltpu.VMEM((2, tm, d), _F32),
            pltpu.SemaphoreType.DMA((2,)), pltpu.SemaphoreType.DMA((2,))]


def _weight_specs(layer, d, hidden, n_up, col_tile):
    n_j = hidden // col_tile
    ups = [pl.BlockSpec((None, d, col_tile),
                        functools.partial(lambda i, j, k: (layer, 0, j + k * n_j), k=k))
           for k in range(n_up)]
    down = pl.BlockSpec((None, col_tile, d), lambda i, j: (layer, j, 0))
    return ups, down


def _mixer_from_tokens(x, meta_tokens, gain, w_in, conv_w, w_out, layer, tm):
    bsz, seq, d = x.shape
    tp = seq + BLOCK
    tiles = tp // tm
    tail_tokens = seq - (tiles - 1) * tm
    assert tiles * tm == tp and 0 < tail_tokens < tm and tail_tokens % HALO == 0 and tm % HALO == 0
    halo_blocks_per_tile = tm // HALO
    col_tile = MIXER_COL_TILE

    def prev_map(i, j):
        b, q = _row_tile_index(i, tiles)
        return b, jnp.maximum(q * halo_blocks_per_tile - 1, 0), 0

    def next_map(i, j):
        b, q = _row_tile_index(i, tiles)
        wrap = q == tiles - 1
        return (jnp.where(wrap, lax.rem(b + 1, bsz), b),
                jnp.where(wrap, 0, (q + 1) * halo_blocks_per_tile), 0)

    ups, down = _weight_specs(layer, d, d, 3, col_tile)
    in_specs = [
        pl.BlockSpec((1, HALO, d), prev_map),
        pl.BlockSpec(memory_space=pl.ANY),
        pl.BlockSpec((1, HALO, d), next_map),
        pl.BlockSpec((N_META, d), lambda i, j: (0, 0)),
        pl.BlockSpec((1, d), lambda i, j: (0, 0)),
        *ups,
        pl.BlockSpec((None, 3, col_tile), lambda i, j: (layer, 0, j)),
        down,
    ]
    return pl.pallas_call(
        functools.partial(_mixer_kernel, tm, tiles, tail_tokens),
        grid=(bsz * tiles, d // col_tile),
        in_specs=in_specs,
        out_specs=pl.BlockSpec(memory_space=pl.ANY),
        out_shape=jax.ShapeDtypeStruct((bsz, tp, d), _F32),
        scratch_shapes=_fused_scratch(tm, d),
        compiler_params=_fused_params(),
    )(x, x, x, meta_tokens, gain.reshape(1, d), w_in, w_in, w_in, conv_w, w_out)


def _conv_ffn(h, gain, w_up, conv_w, conv_b, w_down, layer, tm, out_rows):
    bsz, tp, d = h.shape
    hidden = w_down.shape[1]
    tiles = tp // tm
    out_tail_rows = out_rows - (tiles - 1) * tm
    assert tiles * tm == tp and tm % HALO == 0 and 0 < out_tail_rows <= tm and out_tail_rows % 8 == 0
    halo_blocks_per_tile = tm // HALO
    halo_blocks_per_slab = tp // HALO
    n_halo_blocks = bsz * halo_blocks_per_slab
    col_tile = FFN_COL_TILE

    def halo_map(i, j, offset):
        blk = lax.rem(i * halo_blocks_per_tile + offset + n_halo_blocks, n_halo_blocks)
        return lax.div(blk, halo_blocks_per_slab), lax.rem(blk, halo_blocks_per_slab), 0

    ups, down = _weight_specs(layer, d, hidden, 2, col_tile)
    in_specs = [
        pl.BlockSpec((1, HALO, d), functools.partial(halo_map, offset=-1)),
        pl.BlockSpec(memory_space=pl.ANY),
        pl.BlockSpec((1, HALO, d), functools.partial(halo_map, offset=halo_blocks_per_tile)),
        pl.BlockSpec((1, d), lambda i, j: (0, 0)),
        *ups,
        pl.BlockSpec((None, 3, col_tile), lambda i, j: (layer, 0, j)),
        pl.BlockSpec((None, 1, col_tile), lambda i, j: (layer, 0, j)),
        down,
    ]
    return pl.pallas_call(
        functools.partial(_ffn_kernel, tm, tiles, out_tail_rows),
        grid=(bsz * tiles, hidden // col_tile),
        in_specs=in_specs,
        out_specs=pl.BlockSpec(memory_space=pl.ANY),
        out_shape=jax.ShapeDtypeStruct((bsz, out_rows, d), _F32),
        scratch_shapes=_fused_scratch(tm, d),
        compiler_params=_fused_params(),
    )(h, h, h, gain.reshape(1, d), w_up, w_up, conv_w, conv_b.reshape(conv_b.shape[0], 1, hidden), w_down)


def _qkv_kernel(n_norm_cols, h_ref, gain_ref, w_ref, hgain_ref, out_ref, xn_ref):
    _norm_rows_into(xn_ref, 0, h_ref, h_ref.shape[0], gain_ref[...])
    for c in range(w_ref.shape[1] // QKV_COL_TILE):
        acc = _dot(xn_ref[...], w_ref[:, c * QKV_COL_TILE:(c + 1) * QKV_COL_TILE])
        for hd in range(QKV_COL_TILE // HEAD_DIM):
            lo = hd * HEAD_DIM
            cols = slice(c * QKV_COL_TILE + lo, c * QKV_COL_TILE + lo + HEAD_DIM)
            head = acc[:, lo:lo + HEAD_DIM]
            if c * QKV_COL_TILE + lo < n_norm_cols:
                head = _rmsnorm_rows(head, hgain_ref[:, cols])
            out_ref[:, cols] = head.astype(_BF16)


def _qkv_proj(h, gain, w_qkv, layer, head_gain_cols, tm):
    m, d = h.shape
    n_out = w_qkv.shape[2]
    return pl.pallas_call(
        functools.partial(_qkv_kernel, D_MODEL + KV_DIM),
        grid=(m // tm,),
        in_specs=[
            pl.BlockSpec((tm, d), lambda i: (i, 0)),
            pl.BlockSpec((1, d), lambda i: (0, 0)),
            pl.BlockSpec((None, d, n_out), lambda i: (layer, 0, 0), pipeline_mode=pl.Buffered(1)),
            pl.BlockSpec((1, n_out), lambda i: (0, 0)),
        ],
        out_specs=pl.BlockSpec((tm, n_out), lambda i: (i, 0)),
        out_shape=jax.ShapeDtypeStruct((m, n_out), _BF16),
        scratch_shapes=[pltpu.VMEM((tm, d), _BF16)],
        compiler_params=pltpu.CompilerParams(
            dimension_semantics=("arbitrary",), vmem_limit_bytes=V7X_VMEM_LIMIT_BYTES),
    )(h, gain.reshape(1, d), w_qkv, head_gain_cols)


def _bias_kernel(band_codes_ref, tail_codes_ref, tbl_ref, out_ref):
    head = pl.program_id(0)

    def lookup(codes):
        acc = jnp.zeros(codes.shape, _F32)
        for code in range(N_BUCKETS + 2):
            acc = jnp.where(codes == code, tbl_ref[head, code] * LOG2_E, acc)
        return acc

    band = lookup(band_codes_ref[...])
    tails = [lookup(tail_codes_ref[v]) for v in range(3)]
    masked = jnp.full((BLOCK, BLOCK), NEG_INF, _F32)
    for variant, valid_blocks, tail in ((0, (2,), 0), (1, (1, 2), 1), (2, (0, 1, 2), 2), (3, (0, 1), 2)):
        for blk in range(3):
            cols = slice(blk * BLOCK, (blk + 1) * BLOCK)
            out_ref[variant, 0, :, cols] = band[:, cols] if blk in valid_blocks else masked
        out_ref[variant, 0, :, 3 * BLOCK:] = tails[tail]


def _attn_bias(band_codes, tail_codes, table):
    return pl.pallas_call(
        _bias_kernel,
        grid=(N_HEADS,),
        in_specs=[
            pl.BlockSpec((BLOCK, 3 * BLOCK), lambda h: (0, 0)),
            pl.BlockSpec((3, BLOCK, BLOCK), lambda h: (0, 0, 0)),
            pl.BlockSpec(memory_space=pltpu.SMEM),
        ],
        out_specs=pl.BlockSpec((N_BIAS_VARIANTS, 1, BLOCK, KEYS), lambda h: (0, h, 0, 0)),
        out_shape=jax.ShapeDtypeStruct((N_BIAS_VARIANTS, N_HEADS, BLOCK, KEYS), _F32),
    )(band_codes, tail_codes, table)


def _attn_chain(q_ref, k_refs, v_refs, bias_ref, out_ref, slab, k_slabs, g):
    kv_cols = slice(g * HEAD_DIM, (g + 1) * HEAD_DIM)
    dead = jnp.zeros((BLOCK - N_META, HEAD_DIM), _BF16)
    q = jnp.concatenate(
        [q_ref[slab, :, (g * GROUP + r) * HEAD_DIM:(g * GROUP + r + 1) * HEAD_DIM] for r in range(GROUP)],
        axis=0)
    k = jnp.concatenate([ref[ks, :, kv_cols] for ref, ks in zip(k_refs, k_slabs)] + [dead], axis=0)
    v = jnp.concatenate([ref[ks, :, kv_cols] for ref, ks in zip(v_refs, k_slabs)] + [dead], axis=0)
    s = lax.dot_general(q, k, (((1,), (1,)), ((), ())), preferred_element_type=_F32)
    s = s + bias_ref[0, g * GROUP:(g + 1) * GROUP].reshape(GROUP * BLOCK, KEYS)
    e = jnp.exp2(s - jnp.max(s, axis=-1, keepdims=True))
    denom = jnp.sum(e, axis=-1, keepdims=True)
    o = _dot(e.astype(_BF16), v) / denom
    for r in range(GROUP):
        out_ref[slab, :, (g * GROUP + r) * HEAD_DIM:(g * GROUP + r + 1) * HEAD_DIM] = (
            o[r * BLOCK:(r + 1) * BLOCK].astype(_BF16))


def _attn_kernel(q_ref, kp_ref, kc_ref, kn_ref, km_ref, vp_ref, vc_ref, vn_ref, vm_ref, bias_ref,
                 out_ref):
    n = pl.program_id(0)
    n_slabs = q_ref.shape[0]
    k_refs = (kp_ref, kc_ref, kn_ref, km_ref)
    v_refs = (vp_ref, vc_ref, vn_ref, vm_ref)

    def all_chains(first_block):
        for slab in range(n_slabs):
            if first_block:
                k_slabs = (slab, slab, (slab + 1) % n_slabs, slab)
            else:
                k_slabs = (slab, slab, slab, (slab - 1) % n_slabs)
            for g in range(N_KV_HEADS):
                _attn_chain(q_ref, k_refs, v_refs, bias_ref, out_ref, slab, k_slabs, g)

    @pl.when(n == 0)
    def _():
        all_chains(True)
        out_ref[:, 0:LEAD, :] = jnp.zeros((n_slabs, LEAD, D_MODEL), _BF16)

    @pl.when(n > 0)
    def _():
        all_chains(False)


def _attention(qkv, bias, nb):
    bsz = qkv.shape[0]
    k_col, v_col = D_MODEL // KV_DIM, D_MODEL // KV_DIM + 1

    def slab_block(n):
        return jnp.where(n == 0, nb - 1, n - 1)

    def blk(col, shift):
        return pl.BlockSpec(
            (bsz, BLOCK, KV_DIM), lambda n: (0, slab_block(jnp.clip(n + shift, 0, nb - 1)), col))

    def meta(col):
        return pl.BlockSpec(
            (bsz, N_META, KV_DIM), lambda n: (0, (nb - 1) * (BLOCK // N_META) + LEAD // N_META, col))

    return pl.pallas_call(
        _attn_kernel,
        grid=(nb,),
        in_specs=[
            pl.BlockSpec((bsz, BLOCK, D_MODEL), lambda n: (0, slab_block(n), 0)),
            blk(k_col, -1), blk(k_col, 0), blk(k_col, 1), meta(k_col),
            blk(v_col, -1), blk(v_col, 0), blk(v_col, 1), meta(v_col),
            pl.BlockSpec((1, N_HEADS, BLOCK, KEYS),
                         lambda n: (jnp.where(n == nb - 1, 3, jnp.minimum(n, 2)), 0, 0, 0)),
        ],
        out_specs=pl.BlockSpec((bsz, BLOCK, D_MODEL), lambda n: (0, slab_block(n), 0)),
        out_shape=jax.ShapeDtypeStruct((bsz, nb * BLOCK, D_MODEL), _BF16),
        compiler_params=pltpu.CompilerParams(
            dimension_semantics=("arbitrary",), vmem_limit_bytes=V7X_VMEM_LIMIT_BYTES),
    )(qkv, qkv, qkv, qkv, qkv, qkv, qkv, qkv, qkv, bias)


def _out_proj_kernel(a_ref, w_ref, h_ref, out_ref):
    out_ref[...] = h_ref[...] + _dot(a_ref[...], w_ref[...])


def _out_proj(a, w, layer, h, tm):
    m, d = h.shape
    return pl.pallas_call(
        _out_proj_kernel,
        grid=(m // tm,),
        in_specs=[
            pl.BlockSpec((tm, d), lambda i: (i, 0)),
            pl.BlockSpec((None, d, d), lambda i: (layer, 0, 0), pipeline_mode=pl.Buffered(1)),
            pl.BlockSpec((tm, d), lambda i: (i, 0)),
        ],
        out_specs=pl.BlockSpec((tm, d), lambda i: (i, 0)),
        out_shape=jax.ShapeDtypeStruct((m, d), _F32),
        compiler_params=pltpu.CompilerParams(
            dimension_semantics=("arbitrary",), vmem_limit_bytes=V7X_VMEM_LIMIT_BYTES),
    )(a, w, h)


def _t5_bucket(rel):
    half = N_BUCKETS // 2
    max_exact = half // 2
    side = jnp.where(rel > 0, half, 0)
    n = jnp.abs(rel)
    nf = jnp.maximum(n, 1).astype(_F32)
    large = max_exact + (jnp.log(nf / max_exact) / math.log(MAX_DISTANCE / max_exact)
                         * (half - max_exact)).astype(jnp.int32)
    large = jnp.minimum(large, half - 1)
    return side + jnp.where(n < max_exact, n, large)


def _bias_codes():
    qi = jnp.arange(BLOCK, dtype=jnp.int32)[:, None]
    col = jnp.arange(3 * BLOCK, dtype=jnp.int32)[None, :]
    rel_band = col - BLOCK - qi
    band = jnp.where(jnp.abs(rel_band) <= WINDOW, _t5_bucket(rel_band), CODE_MASKED)
    tcol = jnp.arange(BLOCK, dtype=jnp.int32)[None, :]
    tails = []
    for n_rep in range(3):
        qpos = n_rep * BLOCK + qi
        codes = jnp.where(tcol < N_META, _t5_bucket(LEAD + tcol - qpos), CODE_MASKED)
        tails.append(jnp.where(tcol == N_META, CODE_SINK, codes))
    return band.astype(jnp.int32), jnp.stack(tails).astype(jnp.int32)


def kernel(x, meta_tokens, rel_bias_table, norm_mix, norm_ffn, conv_in_w, conv_dw, conv_out_w,
           attn_qkv, attn_q_gain, attn_k_gain, attn_sink, attn_o, ffn_up, ffn_dw, ffn_dw_b, ffn_down):
    bsz, seq, d = x.shape
    tp = seq + BLOCK
    nb = tp // BLOCK
    tm = tp // FFN_ROW_TILES_PER_SLAB
    assert tm * FFN_ROW_TILES_PER_SLAB == tp and d == D_MODEL and seq % BLOCK == 0 and nb >= 4

    bf = lambda w: w.astype(_BF16)
    ffn_up_bf, ffn_down_bf = bf(ffn_up), bf(ffn_down)

    h = _mixer_from_tokens(x, meta_tokens.astype(x.dtype), norm_mix[0], bf(conv_in_w), conv_dw,
                           bf(conv_out_w), 0, tp // MIXER_ROW_TILES_PER_SLAB)
    h = _conv_ffn(h, norm_ffn[0], ffn_up_bf, ffn_dw, ffn_dw_b, ffn_down_bf, 0, tm, tp)

    q_scale = HEAD_DIM ** -0.5 * LOG2_E
    head_gain_cols = jnp.concatenate([
        jnp.tile(attn_q_gain[0] * q_scale, N_HEADS), jnp.tile(attn_k_gain[0], N_KV_HEADS),
        jnp.ones((KV_DIM,), _F32)]).reshape(1, QKV_DIM)
    h2 = h.reshape(bsz * tp, d)
    qkv = _qkv_proj(h2, norm_mix[1], bf(attn_qkv), 0, head_gain_cols, tp // QKV_ROW_TILES_PER_SLAB)
    table = jnp.concatenate([
        rel_bias_table.T.astype(_F32), jnp.full((N_HEADS, 1), NEG_INF, _F32),
        attn_sink[0].astype(_F32)[:, None]], axis=1)
    bias = _attn_bias(*_bias_codes(), table)
    o = _attention(qkv.reshape(bsz, tp, QKV_DIM), bias, nb).reshape(bsz * tp, d)
    h2 = _out_proj(o, bf(attn_o), 0, h2, tp // OUT_PROJ_ROW_TILES_PER_SLAB)
    return _conv_ffn(h2.reshape(bsz, tp, d), norm_ffn[1], ffn_up_bf, ffn_dw, ffn_dw_b, ffn_down_bf,
                     1, tm, seq)
```

```python
import functools
import math

import jax
import jax.numpy as jnp
from jax import lax
from jax.experimental import pallas as pl
from jax.experimental.pallas import tpu as pltpu

D_MODEL = 2048
N_META = 16
N_HEADS = 16
N_KV_HEADS = 4
HEAD_DIM = D_MODEL // N_HEADS
GROUP = N_HEADS // N_KV_HEADS
KV_DIM = N_KV_HEADS * HEAD_DIM
QKV_DIM = D_MODEL + 2 * KV_DIM
WINDOW = 128
BLOCK = 128
N_BUCKETS = 32
MAX_DISTANCE = 128
D_FF = 5632
EPS = 1e-6
LEAD = BLOCK - N_META

HALO = 16
NORM_ROWS = 32
FFN_ROW_TILES_PER_SLAB = 4
MIXER_ROW_TILES_PER_SLAB = 4
QKV_ROW_TILES_PER_SLAB = 4
OUT_PROJ_ROW_TILES_PER_SLAB = 8
MIXER_COL_TILE = 512
FFN_COL_TILE = 512
QKV_COL_TILE = 512
KEYS = 4 * BLOCK
SINK_COL = 3 * BLOCK + N_META
CODE_MASKED = N_BUCKETS
CODE_SINK = N_BUCKETS + 1
N_BIAS_VARIANTS = 4
NEG_INF = -1e30
LOG2_E = math.log2(math.e)
V7X_VMEM_LIMIT_BYTES = 56 * 1024 * 1024

_BF16 = jnp.bfloat16
_F32 = jnp.float32


def _dot(a, b):
    return jnp.dot(a, b, preferred_element_type=_F32)


def _rmsnorm_rows(v, gain):
    return v * lax.rsqrt(jnp.mean(v * v, axis=-1, keepdims=True) + EPS) * gain


def _norm_rows_into(xext_ref, row0, src_ref, n_rows, gain):
    for r in range(0, n_rows, NORM_ROWS):
        rows = min(NORM_ROWS, n_rows - r)
        xext_ref[row0 + r:row0 + r + rows, :] = (
            _rmsnorm_rows(src_ref[r:r + rows, :], gain).astype(_BF16))


def _gelu_exact(v):
    return 0.5 * v * (1.0 + lax.erf(v * math.sqrt(0.5)))


def _row_tile_index(i, tiles):
    return lax.div(i, tiles), lax.rem(i, tiles)


def _tile_copy(op, hbm, acc_ref, sems, tm, tiles, tail_rows, to_hbm, t):
    b, q = _row_tile_index(t, tiles)
    slot = lax.rem(t, 2)

    def copy(rows):
        hbm_rows = hbm.at[b, pl.ds(q * tm, rows), :]
        acc_rows = acc_ref.at[slot, pl.ds(0, rows), :]
        src, dst = (acc_rows, hbm_rows) if to_hbm else (hbm_rows, acc_rows)
        op(pltpu.make_async_copy(src, dst, sems.at[slot]))

    if tail_rows == tm:
        copy(tm)
    else:
        pl.when(q < tiles - 1)(lambda: copy(tm))
        pl.when(q == tiles - 1)(lambda: copy(tail_rows))


def _stream_row_tiles(load, store, body):
    i, j = pl.program_id(0), pl.program_id(1)
    n_i, n_j = pl.num_programs(0), pl.num_programs(1)
    start, wait = (lambda c: c.start()), (lambda c: c.wait())

    pl.when(jnp.logical_and(i == 0, j == 0))(lambda: load(start, t=i))
    pl.when(j == 0)(lambda: load(wait, t=i))
    pl.when(jnp.logical_and(j == 1, i >= 1))(lambda: store(wait, t=i - 1))
    pl.when(jnp.logical_and(j == 1, i + 1 < n_i))(lambda: load(start, t=i + 1))
    body()
    pl.when(j == n_j - 1)(lambda: store(start, t=i))
    pl.when(jnp.logical_and(j == n_j - 1, i == n_i - 1))(lambda: store(wait, t=i))


def _fill_from_slab(hprev_ref, acc, hnext_ref, gain_ref, xext_ref, tm):
    gain = gain_ref[...]
    xext_ref[0:HALO, :] = _rmsnorm_rows(hprev_ref[0], gain).astype(_BF16)
    _norm_rows_into(xext_ref, HALO, acc, tm, gain)
    xext_ref[HALO + tm:, :] = _rmsnorm_rows(hnext_ref[0], gain).astype(_BF16)


def _fill_from_tokens(xprev_ref, acc, xnext_ref, meta_ref, gain_ref, xext_ref, tm, tiles, tail_tokens):
    q = lax.rem(pl.program_id(0), tiles)
    gain = gain_ref[...]
    meta = meta_ref[...]
    meta_n = _rmsnorm_rows(meta, gain).astype(_BF16)

    @pl.when(q == 0)
    def _():
        xext_ref[0:HALO, :] = meta_n

    @pl.when(q > 0)
    def _():
        xext_ref[0:HALO, :] = _rmsnorm_rows(xprev_ref[0], gain).astype(_BF16)

    @pl.when(q < tiles - 1)
    def _():
        _norm_rows_into(xext_ref, HALO, acc, tm, gain)

    @pl.when(q == tiles - 1)
    def _():
        _norm_rows_into(xext_ref, HALO, acc, tail_tokens, gain)
        xext_ref[HALO + tail_tokens:HALO + tail_tokens + LEAD, :] = jnp.zeros((LEAD, D_MODEL), _BF16)
        xext_ref[HALO + tail_tokens + LEAD:HALO + tm, :] = meta_n
        acc[tail_tokens:tail_tokens + LEAD, :] = jnp.zeros((LEAD, D_MODEL), _F32)
        acc[tail_tokens + LEAD:tm, :] = meta

    xext_ref[HALO + tm:, :] = _rmsnorm_rows(xnext_ref[0], gain).astype(_BF16)


def _conv3(v_ext, w, tm):
    return (v_ext[HALO - 1:HALO - 1 + tm] * w[0:1]
            + v_ext[HALO:HALO + tm] * w[1:2]
            + v_ext[HALO + 1:HALO + 1 + tm] * w[2:3])


def _mixer_kernel(tm, tiles, tail_tokens, xprev_ref, x_hbm, xnext_ref, meta_ref, gain_ref, wb_ref, wc_ref,
                  wh_ref, cw_ref, wo_ref, out_hbm, xext_ref, acc_ref, in_sems, out_sems):
    acc = acc_ref.at[lax.rem(pl.program_id(0), 2)]

    def body():
        @pl.when(pl.program_id(1) == 0)
        def _():
            _fill_from_tokens(xprev_ref, acc, xnext_ref, meta_ref, gain_ref, xext_ref, tm, tiles,
                              tail_tokens)

        x = xext_ref[...]
        ch = _conv3(_dot(x, wc_ref[...]) * _dot(x, wh_ref[...]), cw_ref[...], tm)
        y = _dot(xext_ref[HALO:HALO + tm, :], wb_ref[...]) * ch
        acc[...] += _dot(y.astype(_BF16), wo_ref[...])

    _stream_row_tiles(
        functools.partial(_tile_copy, hbm=x_hbm, acc_ref=acc_ref, sems=in_sems, tm=tm, tiles=tiles,
                          tail_rows=tail_tokens, to_hbm=False),
        functools.partial(_tile_copy, hbm=out_hbm, acc_ref=acc_ref, sems=out_sems, tm=tm, tiles=tiles,
                          tail_rows=tm, to_hbm=True),
        body)


def _ffn_kernel(tm, tiles, out_tail_rows, n_chunks, hprev_ref, h_hbm, hnext_ref, gain_ref, wg_ref,
                wua_ref, wub_ref, cw_ref, cb_ref, wd_ref, out_hbm, xext_ref, acc_ref, in_sems, out_sems):
    c = FFN_COL_TILE
    j = pl.program_id(1)
    acc = acc_ref.at[lax.rem(pl.program_id(0), 2)]

    def act(g_ext, u, lo):
        g = _conv3(g_ext, cw_ref[:, lo:lo + c], tm) + cb_ref[:, lo:lo + c]
        return (_gelu_exact(g) * u).astype(_BF16)

    def body():
        @pl.when(j == 0)
        def _():
            _fill_from_slab(hprev_ref, acc, hnext_ref, gain_ref, xext_ref, tm)

        @pl.when(j < n_chunks // 2)
        def _():
            x, xc = xext_ref[...], xext_ref[HALO:HALO + tm, :]
            g2 = _dot(x, wg_ref[...])
            ua, ub = _dot(xc, wua_ref[...]), _dot(xc, wub_ref[...])
            a = jnp.concatenate([act(g2[:, :c], ua, 0), act(g2[:, c:], ub, c)], axis=1)
            acc[...] += _dot(a, wd_ref[...])

        if n_chunks % 2:
            @pl.when(j == n_chunks // 2)
            def _():
                g = _dot(xext_ref[...], wg_ref[:, 0:c])
                ua = _dot(xext_ref[HALO:HALO + tm, :], wua_ref[...])
                acc[...] += _dot(act(g, ua, 0), wd_ref[0:c, :])

    _stream_row_tiles(
        functools.partial(_tile_copy, hbm=h_hbm, acc_ref=acc_ref, sems=in_sems, tm=tm, tiles=tiles,
                          tail_rows=tm, to_hbm=False),
        functools.partial(_tile_copy, hbm=out_hbm, acc_ref=acc_ref, sems=out_sems, tm=tm, tiles=tiles,
                          tail_rows=out_tail_rows, to_hbm=True),
        body)


def _fused_params():
    return pltpu.CompilerParams(dimension_semantics=("arbitrary", "arbitrary"),
                                vmem_limit_bytes=V7X_VMEM_LIMIT_BYTES)


def _fused_scratch(tm, d):
    return [pltpu.VMEM((tm + 2 * HALO, d), _BF16), pltpu.VMEM((2, tm, d), _F32),
            pltpu.SemaphoreType.DMA((2,)), pltpu.SemaphoreType.DMA((2,))]


def _weight_specs(layer, d, hidden, n_up, col_tile):
    n_j = hidden // col_tile
    ups = [pl.BlockSpec((None, d, col_tile),
                        functools.partial(lambda i, j, k: (layer, 0, j + k * n_j), k=k))
           for k in range(n_up)]
    down = pl.BlockSpec((None, col_tile, d), lambda i, j: (layer, j, 0))
    return ups, down


def _mixer_from_tokens(x, meta_tokens, gain, w_in, conv_w, w_out, layer, tm):
    bsz, seq, d = x.shape
    tp = seq + BLOCK
    tiles = tp // tm
    tail_tokens = seq - (tiles - 1) * tm
    assert tiles * tm == tp and 0 < tail_tokens < tm and tail_tokens % HALO == 0 and tm % HALO == 0
    halo_blocks_per_tile = tm // HALO
    col_tile = MIXER_COL_TILE

    def prev_map(i, j):
        b, q = _row_tile_index(i, tiles)
        return b, jnp.maximum(q * halo_blocks_per_tile - 1, 0), 0

    def next_map(i, j):
        b, q = _row_tile_index(i, tiles)
        wrap = q == tiles - 1
        return (jnp.where(wrap, lax.rem(b + 1, bsz), b),
                jnp.where(wrap, 0, (q + 1) * halo_blocks_per_tile), 0)

    ups, down = _weight_specs(layer, d, d, 3, col_tile)
    in_specs = [
        pl.BlockSpec((1, HALO, d), prev_map),
        pl.BlockSpec(memory_space=pl.ANY),
        pl.BlockSpec((1, HALO, d), next_map),
        pl.BlockSpec((N_META, d), lambda i, j: (0, 0)),
        pl.BlockSpec((1, d), lambda i, j: (0, 0)),
        *ups,
        pl.BlockSpec((None, 3, col_tile), lambda i, j: (layer, 0, j)),
        down,
    ]
    return pl.pallas_call(
        functools.partial(_mixer_kernel, tm, tiles, tail_tokens),
        grid=(bsz * tiles, d // col_tile),
        in_specs=in_specs,
        out_specs=pl.BlockSpec(memory_space=pl.ANY),
        out_shape=jax.ShapeDtypeStruct((bsz, tp, d), _F32),
        scratch_shapes=_fused_scratch(tm, d),
        compiler_params=_fused_params(),
    )(x, x, x, meta_tokens, gain.reshape(1, d), w_in, w_in, w_in, conv_w, w_out)


def _conv_ffn(h, gain, w_up, conv_w, conv_b, w_down, layer, tm, out_rows):
    bsz, tp, d = h.shape
    hidden = w_down.shape[1]
    tiles = tp // tm
    out_tail_rows = out_rows - (tiles - 1) * tm
    assert tiles * tm == tp and tm % HALO == 0 and 0 < out_tail_rows <= tm and out_tail_rows % 8 == 0
    halo_blocks_per_tile = tm // HALO
    halo_blocks_per_slab = tp // HALO
    n_halo_blocks = bsz * halo_blocks_per_slab
    col_tile = FFN_COL_TILE

    def halo_map(i, j, offset):
        blk = lax.rem(i * halo_blocks_per_tile + offset + n_halo_blocks, n_halo_blocks)
        return lax.div(blk, halo_blocks_per_slab), lax.rem(blk, halo_blocks_per_slab), 0

    n_chunks = hidden // col_tile
    assert n_chunks * col_tile == hidden and w_up.shape[2] == 2 * hidden
    in_specs = [
        pl.BlockSpec((1, HALO, d), functools.partial(halo_map, offset=-1)),
        pl.BlockSpec(memory_space=pl.ANY),
        pl.BlockSpec((1, HALO, d), functools.partial(halo_map, offset=halo_blocks_per_tile)),
        pl.BlockSpec((1, d), lambda i, j: (0, 0)),
        pl.BlockSpec((None, d, 2 * col_tile), lambda i, j: (layer, 0, j)),
        pl.BlockSpec((None, d, col_tile), lambda i, j: (layer, 0, n_chunks + 2 * j)),
        pl.BlockSpec((None, d, col_tile),
                     lambda i, j: (layer, 0, jnp.minimum(n_chunks + 2 * j + 1, 2 * n_chunks - 1))),
        pl.BlockSpec((None, 3, 2 * col_tile), lambda i, j: (layer, 0, j)),
        pl.BlockSpec((None, 1, 2 * col_tile), lambda i, j: (layer, 0, j)),
        pl.BlockSpec((None, 2 * col_tile, d), lambda i, j: (layer, j, 0)),
    ]
    return pl.pallas_call(
        functools.partial(_ffn_kernel, tm, tiles, out_tail_rows, n_chunks),
        grid=(bsz * tiles, (n_chunks + 1) // 2),
        in_specs=in_specs,
        out_specs=pl.BlockSpec(memory_space=pl.ANY),
        out_shape=jax.ShapeDtypeStruct((bsz, out_rows, d), _F32),
        scratch_shapes=_fused_scratch(tm, d),
        compiler_params=_fused_params(),
    )(h, h, h, gain.reshape(1, d), w_up, w_up, w_up, conv_w, conv_b.reshape(conv_b.shape[0], 1, hidden),
      w_down)


def _qkv_kernel(n_norm_cols, h_ref, gain_ref, w_ref, hgain_ref, out_ref, xn_ref):
    _norm_rows_into(xn_ref, 0, h_ref, h_ref.shape[0], gain_ref[...])
    for c in range(w_ref.shape[1] // QKV_COL_TILE):
        acc = _dot(xn_ref[...], w_ref[:, c * QKV_COL_TILE:(c + 1) * QKV_COL_TILE])
        for hd in range(QKV_COL_TILE // HEAD_DIM):
            lo = hd * HEAD_DIM
            cols = slice(c * QKV_COL_TILE + lo, c * QKV_COL_TILE + lo + HEAD_DIM)
            head = acc[:, lo:lo + HEAD_DIM]
            if c * QKV_COL_TILE + lo < n_norm_cols:
                head = _rmsnorm_rows(head, hgain_ref[:, cols])
            out_ref[:, cols] = head.astype(_BF16)


def _qkv_proj(h, gain, w_qkv, layer, head_gain_cols, tm):
    m, d = h.shape
    n_out = w_qkv.shape[2]
    return pl.pallas_call(
        functools.partial(_qkv_kernel, D_MODEL + KV_DIM),
        grid=(m // tm,),
        in_specs=[
            pl.BlockSpec((tm, d), lambda i: (i, 0)),
            pl.BlockSpec((1, d), lambda i: (0, 0)),
            pl.BlockSpec((None, d, n_out), lambda i: (layer, 0, 0), pipeline_mode=pl.Buffered(1)),
            pl.BlockSpec((1, n_out), lambda i: (0, 0)),
        ],
        out_specs=pl.BlockSpec((tm, n_out), lambda i: (i, 0)),
        out_shape=jax.ShapeDtypeStruct((m, n_out), _BF16),
        scratch_shapes=[pltpu.VMEM((tm, d), _BF16)],
        compiler_params=pltpu.CompilerParams(
            dimension_semantics=("arbitrary",), vmem_limit_bytes=V7X_VMEM_LIMIT_BYTES),
    )(h, gain.reshape(1, d), w_qkv, head_gain_cols)


def _bias_kernel(band_codes_ref, tail_codes_ref, tbl_ref, out_ref):
    head = pl.program_id(0)

    def lookup(codes):
        acc = jnp.zeros(codes.shape, _F32)
        for code in range(N_BUCKETS + 2):
            acc = jnp.where(codes == code, tbl_ref[head, code] * LOG2_E, acc)
        return acc

    band = lookup(band_codes_ref[...])
    tails = [lookup(tail_codes_ref[v]) for v in range(3)]
    masked = jnp.full((BLOCK, BLOCK), NEG_INF, _F32)
    for variant, valid_blocks, tail in ((0, (2,), 0), (1, (1, 2), 1), (2, (0, 1, 2), 2), (3, (0, 1), 2)):
        for blk in range(3):
            cols = slice(blk * BLOCK, (blk + 1) * BLOCK)
            out_ref[variant, 0, :, cols] = band[:, cols] if blk in valid_blocks else masked
        out_ref[variant, 0, :, 3 * BLOCK:] = tails[tail]


def _attn_bias(band_codes, tail_codes, table):
    return pl.pallas_call(
        _bias_kernel,
        grid=(N_HEADS,),
        in_specs=[
            pl.BlockSpec((BLOCK, 3 * BLOCK), lambda h: (0, 0)),
            pl.BlockSpec((3, BLOCK, BLOCK), lambda h: (0, 0, 0)),
            pl.BlockSpec(memory_space=pltpu.SMEM),
        ],
        out_specs=pl.BlockSpec((N_BIAS_VARIANTS, 1, BLOCK, KEYS), lambda h: (0, h, 0, 0)),
        out_shape=jax.ShapeDtypeStruct((N_BIAS_VARIANTS, N_HEADS, BLOCK, KEYS), _F32),
    )(band_codes, tail_codes, table)


def _attn_chain(q_ref, k_refs, v_refs, bias_ref, out_ref, slab, k_slabs, g):
    kv_cols = slice(g * HEAD_DIM, (g + 1) * HEAD_DIM)
    dead = jnp.zeros((BLOCK - N_META, HEAD_DIM), _BF16)
    q = jnp.concatenate(
        [q_ref[slab, :, (g * GROUP + r) * HEAD_DIM:(g * GROUP + r + 1) * HEAD_DIM] for r in range(GROUP)],
        axis=0)
    k = jnp.concatenate([ref[ks, :, kv_cols] for ref, ks in zip(k_refs, k_slabs)] + [dead], axis=0)
    v = jnp.concatenate([ref[ks, :, kv_cols] for ref, ks in zip(v_refs, k_slabs)] + [dead], axis=0)
    s = lax.dot_general(q, k, (((1,), (1,)), ((), ())), preferred_element_type=_F32)
    s = s + bias_ref[0, g * GROUP:(g + 1) * GROUP].reshape(GROUP * BLOCK, KEYS)
    e = jnp.exp2(s - jnp.max(s, axis=-1, keepdims=True))
    denom = jnp.sum(e, axis=-1, keepdims=True)
    o = _dot(e.astype(_BF16), v) / denom
    for r in range(GROUP):
        out_ref[slab, :, (g * GROUP + r) * HEAD_DIM:(g * GROUP + r + 1) * HEAD_DIM] = (
            o[r * BLOCK:(r + 1) * BLOCK].astype(_BF16))


def _attn_kernel(q_ref, kp_ref, kc_ref, kn_ref, km_ref, vp_ref, vc_ref, vn_ref, vm_ref, bias_ref,
                 out_ref):
    n = pl.program_id(0)
    n_slabs = q_ref.shape[0]
    k_refs = (kp_ref, kc_ref, kn_ref, km_ref)
    v_refs = (vp_ref, vc_ref, vn_ref, vm_ref)

    def all_chains(first_block):
        for slab in range(n_slabs):
            if first_block:
                k_slabs = (slab, slab, (slab + 1) % n_slabs, slab)
            else:
                k_slabs = (slab, slab, slab, (slab - 1) % n_slabs)
            for g in range(N_KV_HEADS):
                _attn_chain(q_ref, k_refs, v_refs, bias_ref, out_ref, slab, k_slabs, g)

    @pl.when(n == 0)
    def _():
        all_chains(True)
        out_ref[:, 0:LEAD, :] = jnp.zeros((n_slabs, LEAD, D_MODEL), _BF16)

    @pl.when(n > 0)
    def _():
        all_chains(False)


def _attention(qkv, bias, nb):
    bsz = qkv.shape[0]
    k_col, v_col = D_MODEL // KV_DIM, D_MODEL // KV_DIM + 1

    def slab_block(n):
        return jnp.where(n == 0, nb - 1, n - 1)

    def blk(col, shift):
        return pl.BlockSpec(
            (bsz, BLOCK, KV_DIM), lambda n: (0, slab_block(jnp.clip(n + shift, 0, nb - 1)), col))

    def meta(col):
        return pl.BlockSpec(
            (bsz, N_META, KV_DIM), lambda n: (0, (nb - 1) * (BLOCK // N_META) + LEAD // N_META, col))

    return pl.pallas_call(
        _attn_kernel,
        grid=(nb,),
        in_specs=[
            pl.BlockSpec((bsz, BLOCK, D_MODEL), lambda n: (0, slab_block(n), 0)),
            blk(k_col, -1), blk(k_col, 0), blk(k_col, 1), meta(k_col),
            blk(v_col, -1), blk(v_col, 0), blk(v_col, 1), meta(v_col),
            pl.BlockSpec((1, N_HEADS, BLOCK, KEYS),
                         lambda n: (jnp.where(n == nb - 1, 3, jnp.minimum(n, 2)), 0, 0, 0)),
        ],
        out_specs=pl.BlockSpec((bsz, BLOCK, D_MODEL), lambda n: (0, slab_block(n), 0)),
        out_shape=jax.ShapeDtypeStruct((bsz, nb * BLOCK, D_MODEL), _BF16),
        compiler_params=pltpu.CompilerParams(
            dimension_semantics=("arbitrary",), vmem_limit_bytes=V7X_VMEM_LIMIT_BYTES),
    )(qkv, qkv, qkv, qkv, qkv, qkv, qkv, qkv, qkv, bias)


def _out_proj_kernel(a_ref, w_ref, h_ref, out_ref):
    out_ref[...] = h_ref[...] + _dot(a_ref[...], w_ref[...])


def _out_proj(a, w, layer, h, tm):
    m, d = h.shape
    return pl.pallas_call(
        _out_proj_kernel,
        grid=(m // tm,),
        in_specs=[
            pl.BlockSpec((tm, d), lambda i: (i, 0)),
            pl.BlockSpec((None, d, d), lambda i: (layer, 0, 0), pipeline_mode=pl.Buffered(1)),
            pl.BlockSpec((tm, d), lambda i: (i, 0)),
        ],
        out_specs=pl.BlockSpec((tm, d), lambda i: (i, 0)),
        out_shape=jax.ShapeDtypeStruct((m, d), _F32),
        compiler_params=pltpu.CompilerParams(
            dimension_semantics=("arbitrary",), vmem_limit_bytes=V7X_VMEM_LIMIT_BYTES),
    )(a, w, h)


def _t5_bucket(rel):
    half = N_BUCKETS // 2
    max_exact = half // 2
    side = jnp.where(rel > 0, half, 0)
    n = jnp.abs(rel)
    nf = jnp.maximum(n, 1).astype(_F32)
    large = max_exact + (jnp.log(nf / max_exact) / math.log(MAX_DISTANCE / max_exact)
                         * (half - max_exact)).astype(jnp.int32)
    large = jnp.minimum(large, half - 1)
    return side + jnp.where(n < max_exact, n, large)


def _bias_codes():
    qi = jnp.arange(BLOCK, dtype=jnp.int32)[:, None]
    col = jnp.arange(3 * BLOCK, dtype=jnp.int32)[None, :]
    rel_band = col - BLOCK - qi
    band = jnp.where(jnp.abs(rel_band) <= WINDOW, _t5_bucket(rel_band), CODE_MASKED)
    tcol = jnp.arange(BLOCK, dtype=jnp.int32)[None, :]
    tails = []
    for n_rep in range(3):
        qpos = n_rep * BLOCK + qi
        codes = jnp.where(tcol < N_META, _t5_bucket(LEAD + tcol - qpos), CODE_MASKED)
        tails.append(jnp.where(tcol == N_META, CODE_SINK, codes))
    return band.astype(jnp.int32), jnp.stack(tails).astype(jnp.int32)


def kernel(x, meta_tokens, rel_bias_table, norm_mix, norm_ffn, conv_in_w, conv_dw, conv_out_w,
           attn_qkv, attn_q_gain, attn_k_gain, attn_sink, attn_o, ffn_up, ffn_dw, ffn_dw_b, ffn_down):
    bsz, seq, d = x.shape
    tp = seq + BLOCK
    nb = tp // BLOCK
    tm = tp // FFN_ROW_TILES_PER_SLAB
    assert tm * FFN_ROW_TILES_PER_SLAB == tp and d == D_MODEL and seq % BLOCK == 0 and nb >= 4

    bf = lambda w: w.astype(_BF16)
    ffn_up_bf, ffn_down_bf = bf(ffn_up), bf(ffn_down)

    h = _mixer_from_tokens(x, meta_tokens.astype(x.dtype), norm_mix[0], bf(conv_in_w), conv_dw,
                           bf(conv_out_w), 0, tp // MIXER_ROW_TILES_PER_SLAB)
    h = _conv_ffn(h, norm_ffn[0], ffn_up_bf, ffn_dw, ffn_dw_b, ffn_down_bf, 0, tm, tp)

    q_scale = HEAD_DIM ** -0.5 * LOG2_E
    head_gain_cols = jnp.concatenate([
        jnp.tile(attn_q_gain[0] * q_scale, N_HEADS), jnp.tile(attn_k_gain[0], N_KV_HEADS),
        jnp.ones((KV_DIM,), _F32)]).reshape(1, QKV_DIM)
    h2 = h.reshape(bsz * tp, d)
    qkv = _qkv_proj(h2, norm_mix[1], bf(attn_qkv), 0, head_gain_cols, tp // QKV_ROW_TILES_PER_SLAB)
    table = jnp.concatenate([
        rel_bias_table.T.astype(_F32), jnp.full((N_HEADS, 1), NEG_INF, _F32),
        attn_sink[0].astype(_F32)[:, None]], axis=1)
    bias = _attn_bias(*_bias_codes(), table)
    o = _attention(qkv.reshape(bsz, tp, QKV_DIM), bias, nb).reshape(bsz * tp, d)
    h2 = _out_proj(o, bf(attn_o), 0, h2, tp // OUT_PROJ_ROW_TILES_PER_SLAB)
    return _conv_ffn(h2.reshape(bsz, tp, d), norm_ffn[1], ffn_up_bf, ffn_dw, ffn_dw_b, ffn_down_bf,
                     1, tm, seq)
```

```python
import functools
import math

import jax
import jax.numpy as jnp
from jax import lax
from jax.experimental import pallas as pl
from jax.experimental.pallas import tpu as pltpu

D_MODEL = 2048
N_META = 16
N_HEADS = 16
N_KV_HEADS = 4
HEAD_DIM = D_MODEL // N_HEADS
GROUP = N_HEADS // N_KV_HEADS
KV_DIM = N_KV_HEADS * HEAD_DIM
QKV_DIM = D_MODEL + 2 * KV_DIM
WINDOW = 128
BLOCK = 128
N_BUCKETS = 32
MAX_DISTANCE = 128
D_FF = 5632
EPS = 1e-6
LEAD = BLOCK - N_META

HALO = 16
NORM_ROWS = 32
FFN_ROW_TILES_PER_SLAB = 4
MIXER_ROW_TILES_PER_SLAB = 4
QKV_ROW_TILES_PER_SLAB = 4
OUT_PROJ_ROW_TILES_PER_SLAB = 8
MIXER_COL_TILE = 512
FFN_COL_TILE = 512
QKV_COL_TILE = 512
KEYS = 4 * BLOCK
SINK_COL = 3 * BLOCK + N_META
CODE_MASKED = N_BUCKETS
CODE_SINK = N_BUCKETS + 1
N_BIAS_VARIANTS = 4
NEG_INF = -1e30
LOG2_E = math.log2(math.e)
V7X_VMEM_LIMIT_BYTES = 56 * 1024 * 1024

_BF16 = jnp.bfloat16
_F32 = jnp.float32


def _dot(a, b):
    return jnp.dot(a, b, preferred_element_type=_F32)


def _rmsnorm_rows(v, gain):
    return v * lax.rsqrt(jnp.mean(v * v, axis=-1, keepdims=True) + EPS) * gain


def _norm_rows_into(xext_ref, row0, src_ref, n_rows, gain):
    for r in range(0, n_rows, NORM_ROWS):
        rows = min(NORM_ROWS, n_rows - r)
        xext_ref[row0 + r:row0 + r + rows, :] = (
            _rmsnorm_rows(src_ref[r:r + rows, :], gain).astype(_BF16))


def _gelu_exact(v):
    return 0.5 * v * (1.0 + lax.erf(v * math.sqrt(0.5)))


def _row_tile_index(i, tiles):
    return lax.div(i, tiles), lax.rem(i, tiles)


def _tile_copy(op, hbm, acc_ref, sems, tm, tiles, tail_rows, to_hbm, t):
    b, q = _row_tile_index(t, tiles)
    slot = lax.rem(t, 2)

    def copy(rows):
        hbm_rows = hbm.at[b, pl.ds(q * tm, rows), :]
        acc_rows = acc_ref.at[slot, pl.ds(0, rows), :]
        src, dst = (acc_rows, hbm_rows) if to_hbm else (hbm_rows, acc_rows)
        op(pltpu.make_async_copy(src, dst, sems.at[slot]))

    if tail_rows == tm:
        copy(tm)
    else:
        pl.when(q < tiles - 1)(lambda: copy(tm))
        pl.when(q == tiles - 1)(lambda: copy(tail_rows))


def _stream_row_tiles(load, store, body):
    i, j = pl.program_id(0), pl.program_id(1)
    n_i, n_j = pl.num_programs(0), pl.num_programs(1)
    start, wait = (lambda c: c.start()), (lambda c: c.wait())

    pl.when(jnp.logical_and(i == 0, j == 0))(lambda: load(start, t=i))
    pl.when(j == 0)(lambda: load(wait, t=i))
    pl.when(jnp.logical_and(j == 1, i >= 1))(lambda: store(wait, t=i - 1))
    pl.when(jnp.logical_and(j == 1, i + 1 < n_i))(lambda: load(start, t=i + 1))
    body()
    pl.when(j == n_j - 1)(lambda: store(start, t=i))
    pl.when(jnp.logical_and(j == n_j - 1, i == n_i - 1))(lambda: store(wait, t=i))


def _fill_from_slab(hprev_ref, acc, hnext_ref, gain_ref, xext_ref, tm):
    gain = gain_ref[...]
    xext_ref[0:HALO, :] = _rmsnorm_rows(hprev_ref[0], gain).astype(_BF16)
    _norm_rows_into(xext_ref, HALO, acc, tm, gain)
    xext_ref[HALO + tm:, :] = _rmsnorm_rows(hnext_ref[0], gain).astype(_BF16)


def _fill_from_tokens(xprev_ref, acc, xnext_ref, meta_ref, gain_ref, xext_ref, tm, tiles, tail_tokens):
    q = lax.rem(pl.program_id(0), tiles)
    gain = gain_ref[...]
    meta = meta_ref[...]
    meta_n = _rmsnorm_rows(meta, gain).astype(_BF16)

    @pl.when(q == 0)
    def _():
        xext_ref[0:HALO, :] = meta_n

    @pl.when(q > 0)
    def _():
        xext_ref[0:HALO, :] = _rmsnorm_rows(xprev_ref[0], gain).astype(_BF16)

    @pl.when(q < tiles - 1)
    def _():
        _norm_rows_into(xext_ref, HALO, acc, tm, gain)

    @pl.when(q == tiles - 1)
    def _():
        _norm_rows_into(xext_ref, HALO, acc, tail_tokens, gain)
        xext_ref[HALO + tail_tokens:HALO + tail_tokens + LEAD, :] = jnp.zeros((LEAD, D_MODEL), _BF16)
        xext_ref[HALO + tail_tokens + LEAD:HALO + tm, :] = meta_n
        acc[tail_tokens:tail_tokens + LEAD, :] = jnp.zeros((LEAD, D_MODEL), _F32)
        acc[tail_tokens + LEAD:tm, :] = meta

    xext_ref[HALO + tm:, :] = _rmsnorm_rows(xnext_ref[0], gain).astype(_BF16)


class _CastPlan:
    def __init__(self, w, layer, n_steps, step_of_grid):
        _, rows, cols = w.shape
        n_blocks = next(nb for nb in range(min(n_steps, rows // 16), 0, -1)
                        if rows % nb == 0 and (rows // nb) % 16 == 0)
        block_rows = rows // n_blocks

        def block(*grid_idx):
            return lax.div(step_of_grid(*grid_idx) * n_blocks, n_steps)

        self.operand = w
        self.in_spec = pl.BlockSpec((None, block_rows, cols), lambda *g: (layer, block(*g), 0))
        self.out_spec = pl.BlockSpec((None, block_rows, cols), lambda *g: (0, block(*g), 0))
        self.out_shape = jax.ShapeDtypeStruct((1, rows, cols), _BF16)


def _run_casts(refs):
    n = len(refs) // 2
    for src, dst in zip(refs[:n], refs[n:]):
        dst[...] = src[...].astype(_BF16)


def _conv3(v_ext, w, tm):
    return (v_ext[HALO - 1:HALO - 1 + tm] * w[0:1]
            + v_ext[HALO:HALO + tm] * w[1:2]
            + v_ext[HALO + 1:HALO + 1 + tm] * w[2:3])


def _mixer_kernel(tm, tiles, tail_tokens, n_casts, xprev_ref, x_hbm, xnext_ref, meta_ref, gain_ref, wb_ref,
                  wc_ref, wh_ref, cw_ref, wo_ref, *rest):
    cast_srcs, out_hbm, cast_dsts = rest[:n_casts], rest[n_casts], rest[n_casts + 1:2 * n_casts + 1]
    xext_ref, acc_ref, in_sems, out_sems = rest[2 * n_casts + 1:]
    acc = acc_ref.at[lax.rem(pl.program_id(0), 2)]
    _run_casts(cast_srcs + cast_dsts)

    def body():
        @pl.when(pl.program_id(1) == 0)
        def _():
            _fill_from_tokens(xprev_ref, acc, xnext_ref, meta_ref, gain_ref, xext_ref, tm, tiles,
                              tail_tokens)

        x = xext_ref[...]
        ch = _conv3(_dot(x, wc_ref[...]) * _dot(x, wh_ref[...]), cw_ref[...], tm)
        y = _dot(xext_ref[HALO:HALO + tm, :], wb_ref[...]) * ch
        acc[...] += _dot(y.astype(_BF16), wo_ref[...])

    _stream_row_tiles(
        functools.partial(_tile_copy, hbm=x_hbm, acc_ref=acc_ref, sems=in_sems, tm=tm, tiles=tiles,
                          tail_rows=tail_tokens, to_hbm=False),
        functools.partial(_tile_copy, hbm=out_hbm, acc_ref=acc_ref, sems=out_sems, tm=tm, tiles=tiles,
                          tail_rows=tm, to_hbm=True),
        body)


def _ffn_kernel(tm, tiles, out_tail_rows, n_chunks, hprev_ref, h_hbm, hnext_ref, gain_ref, wg_ref,
                wua_ref, wub_ref, cw_ref, cb_ref, wd_ref, out_hbm, xext_ref, acc_ref, in_sems, out_sems):
    c = FFN_COL_TILE
    j = pl.program_id(1)
    acc = acc_ref.at[lax.rem(pl.program_id(0), 2)]

    def act(g_ext, u, lo):
        g = _conv3(g_ext, cw_ref[:, lo:lo + c], tm) + cb_ref[:, lo:lo + c]
        return (_gelu_exact(g) * u).astype(_BF16)

    def body():
        @pl.when(j == 0)
        def _():
            _fill_from_slab(hprev_ref, acc, hnext_ref, gain_ref, xext_ref, tm)

        @pl.when(j < n_chunks // 2)
        def _():
            x, xc = xext_ref[...], xext_ref[HALO:HALO + tm, :]
            g2 = _dot(x, wg_ref[...])
            ua, ub = _dot(xc, wua_ref[...]), _dot(xc, wub_ref[...])
            a = jnp.concatenate([act(g2[:, :c], ua, 0), act(g2[:, c:], ub, c)], axis=1)
            acc[...] += _dot(a, wd_ref[...])

        if n_chunks % 2:
            @pl.when(j == n_chunks // 2)
            def _():
                g = _dot(xext_ref[...], wg_ref[:, 0:c])
                ua = _dot(xext_ref[HALO:HALO + tm, :], wua_ref[...])
                acc[...] += _dot(act(g, ua, 0), wd_ref[0:c, :])

    _stream_row_tiles(
        functools.partial(_tile_copy, hbm=h_hbm, acc_ref=acc_ref, sems=in_sems, tm=tm, tiles=tiles,
                          tail_rows=tm, to_hbm=False),
        functools.partial(_tile_copy, hbm=out_hbm, acc_ref=acc_ref, sems=out_sems, tm=tm, tiles=tiles,
                          tail_rows=out_tail_rows, to_hbm=True),
        body)


def _fused_params():
    return pltpu.CompilerParams(dimension_semantics=("arbitrary", "arbitrary"),
                                vmem_limit_bytes=V7X_VMEM_LIMIT_BYTES)


def _fused_scratch(tm, d):
    return [pltpu.VMEM((tm + 2 * HALO, d), _BF16), pltpu.VMEM((2, tm, d), _F32),
            pltpu.SemaphoreType.DMA((2,)), pltpu.SemaphoreType.DMA((2,))]


def _weight_specs(layer, d, hidden, n_up, col_tile):
    n_j = hidden // col_tile
    ups = [pl.BlockSpec((None, d, col_tile),
                        functools.partial(lambda i, j, k: (layer, 0, j + k * n_j), k=k))
           for k in range(n_up)]
    down = pl.BlockSpec((None, col_tile, d), lambda i, j: (layer, j, 0))
    return ups, down


def _mixer_from_tokens(x, meta_tokens, gain, w_in, conv_w, w_out, layer, tm, cast_weights):
    bsz, seq, d = x.shape
    tp = seq + BLOCK
    tiles = tp // tm
    tail_tokens = seq - (tiles - 1) * tm
    assert tiles * tm == tp and 0 < tail_tokens < tm and tail_tokens % HALO == 0 and tm % HALO == 0
    halo_blocks_per_tile = tm // HALO
    col_tile = MIXER_COL_TILE

    def prev_map(i, j):
        b, q = _row_tile_index(i, tiles)
        return b, jnp.maximum(q * halo_blocks_per_tile - 1, 0), 0

    def next_map(i, j):
        b, q = _row_tile_index(i, tiles)
        wrap = q == tiles - 1
        return (jnp.where(wrap, lax.rem(b + 1, bsz), b),
                jnp.where(wrap, 0, (q + 1) * halo_blocks_per_tile), 0)

    ups, down = _weight_specs(layer, d, d, 3, col_tile)
    n_j = d // col_tile
    casts = [_CastPlan(w, l, bsz * tiles * n_j, lambda i, j: i * n_j + j) for w, l in cast_weights]
    in_specs = [
        pl.BlockSpec((1, HALO, d), prev_map),
        pl.BlockSpec(memory_space=pl.ANY),
        pl.BlockSpec((1, HALO, d), next_map),
        pl.BlockSpec((N_META, d), lambda i, j: (0, 0)),
        pl.BlockSpec((1, d), lambda i, j: (0, 0)),
        *ups,
        pl.BlockSpec((None, 3, col_tile), lambda i, j: (layer, 0, j)),
        down,
        *[c.in_spec for c in casts],
    ]
    out, *cast_out = pl.pallas_call(
        functools.partial(_mixer_kernel, tm, tiles, tail_tokens, len(casts)),
        grid=(bsz * tiles, n_j),
        in_specs=in_specs,
        out_specs=[pl.BlockSpec(memory_space=pl.ANY), *[c.out_spec for c in casts]],
        out_shape=[jax.ShapeDtypeStruct((bsz, tp, d), _F32), *[c.out_shape for c in casts]],
        scratch_shapes=_fused_scratch(tm, d),
        compiler_params=_fused_params(),
    )(x, x, x, meta_tokens, gain.reshape(1, d), w_in, w_in, w_in, conv_w, w_out,
      *[c.operand for c in casts])
    return out, cast_out


def _conv_ffn(h, gain, w_up, conv_w, conv_b, w_down, layer, conv_layer, tm, out_rows):
    bsz, tp, d = h.shape
    hidden = w_down.shape[1]
    tiles = tp // tm
    out_tail_rows = out_rows - (tiles - 1) * tm
    assert tiles * tm == tp and tm % HALO == 0 and 0 < out_tail_rows <= tm and out_tail_rows % 8 == 0
    halo_blocks_per_tile = tm // HALO
    halo_blocks_per_slab = tp // HALO
    n_halo_blocks = bsz * halo_blocks_per_slab
    col_tile = FFN_COL_TILE

    def halo_map(i, j, offset):
        blk = lax.rem(i * halo_blocks_per_tile + offset + n_halo_blocks, n_halo_blocks)
        return lax.div(blk, halo_blocks_per_slab), lax.rem(blk, halo_blocks_per_slab), 0

    n_chunks = hidden // col_tile
    assert n_chunks * col_tile == hidden and w_up.shape[2] == 2 * hidden
    in_specs = [
        pl.BlockSpec((1, HALO, d), functools.partial(halo_map, offset=-1)),
        pl.BlockSpec(memory_space=pl.ANY),
        pl.BlockSpec((1, HALO, d), functools.partial(halo_map, offset=halo_blocks_per_tile)),
        pl.BlockSpec((1, d), lambda i, j: (0, 0)),
        pl.BlockSpec((None, d, 2 * col_tile), lambda i, j: (layer, 0, j)),
        pl.BlockSpec((None, d, col_tile), lambda i, j: (layer, 0, n_chunks + 2 * j)),
        pl.BlockSpec((None, d, col_tile),
                     lambda i, j: (layer, 0, jnp.minimum(n_chunks + 2 * j + 1, 2 * n_chunks - 1))),
        pl.BlockSpec((None, 3, 2 * col_tile), lambda i, j: (conv_layer, 0, j)),
        pl.BlockSpec((None, 1, 2 * col_tile), lambda i, j: (conv_layer, 0, j)),
        pl.BlockSpec((None, 2 * col_tile, d), lambda i, j: (layer, j, 0)),
    ]
    return pl.pallas_call(
        functools.partial(_ffn_kernel, tm, tiles, out_tail_rows, n_chunks),
        grid=(bsz * tiles, (n_chunks + 1) // 2),
        in_specs=in_specs,
        out_specs=pl.BlockSpec(memory_space=pl.ANY),
        out_shape=jax.ShapeDtypeStruct((bsz, out_rows, d), _F32),
        scratch_shapes=_fused_scratch(tm, d),
        compiler_params=_fused_params(),
    )(h, h, h, gain.reshape(1, d), w_up, w_up, w_up, conv_w, conv_b.reshape(conv_b.shape[0], 1, hidden),
      w_down)


def _qkv_kernel(n_norm_cols, h_ref, gain_ref, w_ref, hgain_ref, out_ref, xn_ref):
    _norm_rows_into(xn_ref, 0, h_ref, h_ref.shape[0], gain_ref[...])
    for c in range(w_ref.shape[1] // QKV_COL_TILE):
        acc = _dot(xn_ref[...], w_ref[:, c * QKV_COL_TILE:(c + 1) * QKV_COL_TILE])
        for hd in range(QKV_COL_TILE // HEAD_DIM):
            lo = hd * HEAD_DIM
            cols = slice(c * QKV_COL_TILE + lo, c * QKV_COL_TILE + lo + HEAD_DIM)
            head = acc[:, lo:lo + HEAD_DIM]
            if c * QKV_COL_TILE + lo < n_norm_cols:
                head = _rmsnorm_rows(head, hgain_ref[:, cols])
            out_ref[:, cols] = head.astype(_BF16)


def _qkv_proj(h, gain, w_qkv, layer, head_gain_cols, tm):
    m, d = h.shape
    n_out = w_qkv.shape[2]
    return pl.pallas_call(
        functools.partial(_qkv_kernel, D_MODEL + KV_DIM),
        grid=(m // tm,),
        in_specs=[
            pl.BlockSpec((tm, d), lambda i: (i, 0)),
            pl.BlockSpec((1, d), lambda i: (0, 0)),
            pl.BlockSpec((None, d, n_out), lambda i: (layer, 0, 0), pipeline_mode=pl.Buffered(1)),
            pl.BlockSpec((1, n_out), lambda i: (0, 0)),
        ],
        out_specs=pl.BlockSpec((tm, n_out), lambda i: (i, 0)),
        out_shape=jax.ShapeDtypeStruct((m, n_out), _BF16),
        scratch_shapes=[pltpu.VMEM((tm, d), _BF16)],
        compiler_params=pltpu.CompilerParams(
            dimension_semantics=("arbitrary",), vmem_limit_bytes=V7X_VMEM_LIMIT_BYTES),
    )(h, gain.reshape(1, d), w_qkv, head_gain_cols)


def _bias_kernel(band_codes_ref, tail_codes_ref, tbl_ref, out_ref):
    head = pl.program_id(0)

    def lookup(codes):
        acc = jnp.zeros(codes.shape, _F32)
        for code in range(N_BUCKETS + 2):
            acc = jnp.where(codes == code, tbl_ref[head, code] * LOG2_E, acc)
        return acc

    band = lookup(band_codes_ref[...])
    tails = [lookup(tail_codes_ref[v]) for v in range(3)]
    masked = jnp.full((BLOCK, BLOCK), NEG_INF, _F32)
    for variant, valid_blocks, tail in ((0, (2,), 0), (1, (1, 2), 1), (2, (0, 1, 2), 2), (3, (0, 1), 2)):
        for blk in range(3):
            cols = slice(blk * BLOCK, (blk + 1) * BLOCK)
            out_ref[variant, 0, :, cols] = band[:, cols] if blk in valid_blocks else masked
        out_ref[variant, 0, :, 3 * BLOCK:] = tails[tail]


def _attn_bias(band_codes, tail_codes, table):
    return pl.pallas_call(
        _bias_kernel,
        grid=(N_HEADS,),
        in_specs=[
            pl.BlockSpec((BLOCK, 3 * BLOCK), lambda h: (0, 0)),
            pl.BlockSpec((3, BLOCK, BLOCK), lambda h: (0, 0, 0)),
            pl.BlockSpec(memory_space=pltpu.SMEM),
        ],
        out_specs=pl.BlockSpec((N_BIAS_VARIANTS, 1, BLOCK, KEYS), lambda h: (0, h, 0, 0)),
        out_shape=jax.ShapeDtypeStruct((N_BIAS_VARIANTS, N_HEADS, BLOCK, KEYS), _F32),
    )(band_codes, tail_codes, table)


def _attn_chain(q_ref, k_refs, v_refs, bias_ref, out_ref, slab, k_slabs, g):
    kv_cols = slice(g * HEAD_DIM, (g + 1) * HEAD_DIM)
    dead = jnp.zeros((BLOCK - N_META, HEAD_DIM), _BF16)
    q = jnp.concatenate(
        [q_ref[slab, :, (g * GROUP + r) * HEAD_DIM:(g * GROUP + r + 1) * HEAD_DIM] for r in range(GROUP)],
        axis=0)
    k = jnp.concatenate([ref[ks, :, kv_cols] for ref, ks in zip(k_refs, k_slabs)] + [dead], axis=0)
    v = jnp.concatenate([ref[ks, :, kv_cols] for ref, ks in zip(v_refs, k_slabs)] + [dead], axis=0)
    s = lax.dot_general(q, k, (((1,), (1,)), ((), ())), preferred_element_type=_F32)
    s = s + bias_ref[0, g * GROUP:(g + 1) * GROUP].reshape(GROUP * BLOCK, KEYS)
    e = jnp.exp2(s - jnp.max(s, axis=-1, keepdims=True))
    denom = jnp.sum(e, axis=-1, keepdims=True)
    o = _dot(e.astype(_BF16), v) / denom
    for r in range(GROUP):
        out_ref[slab, :, (g * GROUP + r) * HEAD_DIM:(g * GROUP + r + 1) * HEAD_DIM] = (
            o[r * BLOCK:(r + 1) * BLOCK].astype(_BF16))


def _attn_kernel(n_casts, q_ref, kp_ref, kc_ref, kn_ref, km_ref, vp_ref, vc_ref, vn_ref, vm_ref, bias_ref,
                 *rest):
    out_ref = rest[n_casts]
    _run_casts(rest[:n_casts] + rest[n_casts + 1:])
    n = pl.program_id(0)
    n_slabs = q_ref.shape[0]
    k_refs = (kp_ref, kc_ref, kn_ref, km_ref)
    v_refs = (vp_ref, vc_ref, vn_ref, vm_ref)

    def all_chains(first_block):
        for slab in range(n_slabs):
            if first_block:
                k_slabs = (slab, slab, (slab + 1) % n_slabs, slab)
            else:
                k_slabs = (slab, slab, slab, (slab - 1) % n_slabs)
            for g in range(N_KV_HEADS):
                _attn_chain(q_ref, k_refs, v_refs, bias_ref, out_ref, slab, k_slabs, g)

    @pl.when(n == 0)
    def _():
        all_chains(True)
        out_ref[:, 0:LEAD, :] = jnp.zeros((n_slabs, LEAD, D_MODEL), _BF16)

    @pl.when(n > 0)
    def _():
        all_chains(False)


def _attention(qkv, bias, nb, cast_weights):
    bsz = qkv.shape[0]
    casts = [_CastPlan(w, l, nb, lambda n: n) for w, l in cast_weights]
    k_col, v_col = D_MODEL // KV_DIM, D_MODEL // KV_DIM + 1

    def slab_block(n):
        return jnp.where(n == 0, nb - 1, n - 1)

    def blk(col, shift):
        return pl.BlockSpec(
            (bsz, BLOCK, KV_DIM), lambda n: (0, slab_block(jnp.clip(n + shift, 0, nb - 1)), col))

    def meta(col):
        return pl.BlockSpec(
            (bsz, N_META, KV_DIM), lambda n: (0, (nb - 1) * (BLOCK // N_META) + LEAD // N_META, col))

    out, *cast_out = pl.pallas_call(
        functools.partial(_attn_kernel, len(casts)),
        grid=(nb,),
        in_specs=[
            pl.BlockSpec((bsz, BLOCK, D_MODEL), lambda n: (0, slab_block(n), 0)),
            blk(k_col, -1), blk(k_col, 0), blk(k_col, 1), meta(k_col),
            blk(v_col, -1), blk(v_col, 0), blk(v_col, 1), meta(v_col),
            pl.BlockSpec((1, N_HEADS, BLOCK, KEYS),
                         lambda n: (jnp.where(n == nb - 1, 3, jnp.minimum(n, 2)), 0, 0, 0)),
            *[c.in_spec for c in casts],
        ],
        out_specs=[pl.BlockSpec((bsz, BLOCK, D_MODEL), lambda n: (0, slab_block(n), 0)),
                   *[c.out_spec for c in casts]],
        out_shape=[jax.ShapeDtypeStruct((bsz, nb * BLOCK, D_MODEL), _BF16),
                   *[c.out_shape for c in casts]],
        compiler_params=pltpu.CompilerParams(
            dimension_semantics=("arbitrary",), vmem_limit_bytes=V7X_VMEM_LIMIT_BYTES),
    )(qkv, qkv, qkv, qkv, qkv, qkv, qkv, qkv, qkv, bias, *[c.operand for c in casts])
    return out, cast_out


def _out_proj_kernel(a_ref, w_ref, h_ref, out_ref):
    out_ref[...] = h_ref[...] + _dot(a_ref[...], w_ref[...])


def _out_proj(a, w, layer, h, tm):
    m, d = h.shape
    return pl.pallas_call(
        _out_proj_kernel,
        grid=(m // tm,),
        in_specs=[
            pl.BlockSpec((tm, d), lambda i: (i, 0)),
            pl.BlockSpec((None, d, d), lambda i: (layer, 0, 0), pipeline_mode=pl.Buffered(1)),
            pl.BlockSpec((tm, d), lambda i: (i, 0)),
        ],
        out_specs=pl.BlockSpec((tm, d), lambda i: (i, 0)),
        out_shape=jax.ShapeDtypeStruct((m, d), _F32),
        compiler_params=pltpu.CompilerParams(
            dimension_semantics=("arbitrary",), vmem_limit_bytes=V7X_VMEM_LIMIT_BYTES),
    )(a, w, h)


def _t5_bucket(rel):
    half = N_BUCKETS // 2
    max_exact = half // 2
    side = jnp.where(rel > 0, half, 0)
    n = jnp.abs(rel)
    nf = jnp.maximum(n, 1).astype(_F32)
    large = max_exact + (jnp.log(nf / max_exact) / math.log(MAX_DISTANCE / max_exact)
                         * (half - max_exact)).astype(jnp.int32)
    large = jnp.minimum(large, half - 1)
    return side + jnp.where(n < max_exact, n, large)


def _bias_codes():
    qi = jnp.arange(BLOCK, dtype=jnp.int32)[:, None]
    col = jnp.arange(3 * BLOCK, dtype=jnp.int32)[None, :]
    rel_band = col - BLOCK - qi
    band = jnp.where(jnp.abs(rel_band) <= WINDOW, _t5_bucket(rel_band), CODE_MASKED)
    tcol = jnp.arange(BLOCK, dtype=jnp.int32)[None, :]
    tails = []
    for n_rep in range(3):
        qpos = n_rep * BLOCK + qi
        codes = jnp.where(tcol < N_META, _t5_bucket(LEAD + tcol - qpos), CODE_MASKED)
        tails.append(jnp.where(tcol == N_META, CODE_SINK, codes))
    return band.astype(jnp.int32), jnp.stack(tails).astype(jnp.int32)


def kernel(x, meta_tokens, rel_bias_table, norm_mix, norm_ffn, conv_in_w, conv_dw, conv_out_w,
           attn_qkv, attn_q_gain, attn_k_gain, attn_sink, attn_o, ffn_up, ffn_dw, ffn_dw_b, ffn_down):
    bsz, seq, d = x.shape
    tp = seq + BLOCK
    nb = tp // BLOCK
    tm = tp // FFN_ROW_TILES_PER_SLAB
    assert tm * FFN_ROW_TILES_PER_SLAB == tp and d == D_MODEL and seq % BLOCK == 0 and nb >= 4

    bf = lambda w: w.astype(_BF16)

    h, (up0, down0, qkv_w, o_w) = _mixer_from_tokens(
        x, meta_tokens.astype(x.dtype), norm_mix[0], bf(conv_in_w), conv_dw, bf(conv_out_w), 0,
        tp // MIXER_ROW_TILES_PER_SLAB, [(ffn_up, 0), (ffn_down, 0), (attn_qkv, 0), (attn_o, 0)])
    h = _conv_ffn(h, norm_ffn[0], up0, ffn_dw, ffn_dw_b, down0, 0, 0, tm, tp)

    q_scale = HEAD_DIM ** -0.5 * LOG2_E
    head_gain_cols = jnp.concatenate([
        jnp.tile(attn_q_gain[0] * q_scale, N_HEADS), jnp.tile(attn_k_gain[0], N_KV_HEADS),
        jnp.ones((KV_DIM,), _F32)]).reshape(1, QKV_DIM)
    h2 = h.reshape(bsz * tp, d)
    qkv = _qkv_proj(h2, norm_mix[1], qkv_w, 0, head_gain_cols, tp // QKV_ROW_TILES_PER_SLAB)
    table = jnp.concatenate([
        rel_bias_table.T.astype(_F32), jnp.full((N_HEADS, 1), NEG_INF, _F32),
        attn_sink[0].astype(_F32)[:, None]], axis=1)
    bias = _attn_bias(*_bias_codes(), table)
    o, (up1, down1) = _attention(qkv.reshape(bsz, tp, QKV_DIM), bias, nb, [(ffn_up, 1), (ffn_down, 1)])
    h2 = _out_proj(o.reshape(bsz * tp, d), o_w, 0, h2, tp // OUT_PROJ_ROW_TILES_PER_SLAB)
    return _conv_ffn(h2.reshape(bsz, tp, d), norm_ffn[1], up1, ffn_dw, ffn_dw_b, down1, 0, 1, tm, seq)
```

```python
import functools
import math

import jax
import jax.numpy as jnp
from jax import lax
from jax.experimental import pallas as pl
from jax.experimental.pallas import tpu as pltpu

D_MODEL = 2048
N_META = 16
N_HEADS = 16
N_KV_HEADS = 4
HEAD_DIM = D_MODEL // N_HEADS
GROUP = N_HEADS // N_KV_HEADS
KV_DIM = N_KV_HEADS * HEAD_DIM
QKV_DIM = D_MODEL + 2 * KV_DIM
WINDOW = 128
BLOCK = 128
N_BUCKETS = 32
MAX_DISTANCE = 128
D_FF = 5632
EPS = 1e-6
LEAD = BLOCK - N_META

HALO = 16
NORM_ROWS = 32
FFN_ROW_TILES_PER_SLAB = 4
MIXER_ROW_TILES_PER_SLAB = 4
QKV_ROW_TILES_PER_SLAB = 4
OUT_PROJ_ROW_TILES_PER_SLAB = 8
MIXER_COL_TILE = 512
FFN_COL_TILE = 512
QKV_COL_TILE = 512
KEYS = 4 * BLOCK
SINK_COL = 3 * BLOCK + N_META
CODE_MASKED = N_BUCKETS
CODE_SINK = N_BUCKETS + 1
N_BIAS_VARIANTS = 4
NEG_INF = -1e30
LOG2_E = math.log2(math.e)
V7X_VMEM_LIMIT_BYTES = 56 * 1024 * 1024

_BF16 = jnp.bfloat16
_F32 = jnp.float32


def _dot(a, b):
    return jnp.dot(a, b, preferred_element_type=_F32)


def _rmsnorm_rows(v, gain):
    return v * lax.rsqrt(jnp.mean(v * v, axis=-1, keepdims=True) + EPS) * gain


def _norm_rows_into(xext_ref, row0, src_ref, n_rows, gain):
    for r in range(0, n_rows, NORM_ROWS):
        rows = min(NORM_ROWS, n_rows - r)
        xext_ref[row0 + r:row0 + r + rows, :] = (
            _rmsnorm_rows(src_ref[r:r + rows, :], gain).astype(_BF16))


def _gelu_exact(v):
    return 0.5 * v * (1.0 + lax.erf(v * math.sqrt(0.5)))


def _row_tile_index(i, tiles):
    return lax.div(i, tiles), lax.rem(i, tiles)


def _tile_copy(op, hbm, acc_ref, sems, tm, tiles, tail_rows, to_hbm, t):
    b, q = _row_tile_index(t, tiles)
    slot = lax.rem(t, 2)

    def copy(rows):
        hbm_rows = hbm.at[b, pl.ds(q * tm, rows), :]
        acc_rows = acc_ref.at[slot, pl.ds(0, rows), :]
        src, dst = (acc_rows, hbm_rows) if to_hbm else (hbm_rows, acc_rows)
        op(pltpu.make_async_copy(src, dst, sems.at[slot]))

    if tail_rows == tm:
        copy(tm)
    else:
        pl.when(q < tiles - 1)(lambda: copy(tm))
        pl.when(q == tiles - 1)(lambda: copy(tail_rows))


def _stream_row_tiles(load, store, body):
    i, j = pl.program_id(0), pl.program_id(1)
    n_i, n_j = pl.num_programs(0), pl.num_programs(1)
    start, wait = (lambda c: c.start()), (lambda c: c.wait())

    pl.when(jnp.logical_and(i == 0, j == 0))(lambda: load(start, t=i))
    pl.when(j == 0)(lambda: load(wait, t=i))
    pl.when(jnp.logical_and(j == 1, i >= 1))(lambda: store(wait, t=i - 1))
    pl.when(jnp.logical_and(j == 1, i + 1 < n_i))(lambda: load(start, t=i + 1))
    body()
    pl.when(j == n_j - 1)(lambda: store(start, t=i))
    pl.when(jnp.logical_and(j == n_j - 1, i == n_i - 1))(lambda: store(wait, t=i))


def _fill_from_slab(hprev_ref, acc, hnext_ref, gain_ref, xext_ref, tm):
    gain = gain_ref[...]
    xext_ref[0:HALO, :] = _rmsnorm_rows(hprev_ref[0], gain).astype(_BF16)
    _norm_rows_into(xext_ref, HALO, acc, tm, gain)
    xext_ref[HALO + tm:, :] = _rmsnorm_rows(hnext_ref[0], gain).astype(_BF16)


def _fill_from_tokens(xprev_ref, acc, xnext_ref, meta_ref, gain_ref, xext_ref, tm, tiles, tail_tokens):
    q = lax.rem(pl.program_id(0), tiles)
    gain = gain_ref[...]
    meta = meta_ref[...]
    meta_n = _rmsnorm_rows(meta, gain).astype(_BF16)

    @pl.when(q == 0)
    def _():
        xext_ref[0:HALO, :] = meta_n

    @pl.when(q > 0)
    def _():
        xext_ref[0:HALO, :] = _rmsnorm_rows(xprev_ref[0], gain).astype(_BF16)

    @pl.when(q < tiles - 1)
    def _():
        _norm_rows_into(xext_ref, HALO, acc, tm, gain)

    @pl.when(q == tiles - 1)
    def _():
        _norm_rows_into(xext_ref, HALO, acc, tail_tokens, gain)
        xext_ref[HALO + tail_tokens:HALO + tail_tokens + LEAD, :] = jnp.zeros((LEAD, D_MODEL), _BF16)
        xext_ref[HALO + tail_tokens + LEAD:HALO + tm, :] = meta_n
        acc[tail_tokens:tail_tokens + LEAD, :] = jnp.zeros((LEAD, D_MODEL), _F32)
        acc[tail_tokens + LEAD:tm, :] = meta

    xext_ref[HALO + tm:, :] = _rmsnorm_rows(xnext_ref[0], gain).astype(_BF16)


class _CastPlan:
    def __init__(self, w, layer, n_steps, step_of_grid):
        _, rows, cols = w.shape
        n_blocks = next(nb for nb in range(min(n_steps, rows // 16), 0, -1)
                        if rows % nb == 0 and (rows // nb) % 16 == 0)
        block_rows = rows // n_blocks

        def block(*grid_idx):
            return lax.div(step_of_grid(*grid_idx) * n_blocks, n_steps)

        self.operand = w
        self.in_spec = pl.BlockSpec((None, block_rows, cols), lambda *g: (layer, block(*g), 0))
        self.out_spec = pl.BlockSpec((None, block_rows, cols), lambda *g: (0, block(*g), 0))
        self.out_shape = jax.ShapeDtypeStruct((1, rows, cols), _BF16)


def _run_casts(refs):
    n = len(refs) // 2
    for src, dst in zip(refs[:n], refs[n:]):
        dst[...] = src[...].astype(_BF16)


def _conv3(v_ext, w, tm):
    return (v_ext[HALO - 1:HALO - 1 + tm] * w[0:1]
            + v_ext[HALO:HALO + tm] * w[1:2]
            + v_ext[HALO + 1:HALO + 1 + tm] * w[2:3])


def _mixer_kernel(tm, tiles, tail_tokens, n_casts, xprev_ref, x_hbm, xnext_ref, meta_ref, gain_ref, wb_ref,
                  wc_ref, wh_ref, cw_ref, wo_ref, *rest):
    cast_srcs, out_hbm, cast_dsts = rest[:n_casts], rest[n_casts], rest[n_casts + 1:2 * n_casts + 1]
    xext_ref, acc_ref, in_sems, out_sems = rest[2 * n_casts + 1:]
    acc = acc_ref.at[lax.rem(pl.program_id(0), 2)]
    _run_casts(cast_srcs + cast_dsts)

    def body():
        @pl.when(pl.program_id(1) == 0)
        def _():
            _fill_from_tokens(xprev_ref, acc, xnext_ref, meta_ref, gain_ref, xext_ref, tm, tiles,
                              tail_tokens)

        x = xext_ref[...]
        ch = _conv3(_dot(x, wc_ref[...]) * _dot(x, wh_ref[...]), cw_ref[...], tm)
        y = _dot(xext_ref[HALO:HALO + tm, :], wb_ref[...]) * ch
        acc[...] += _dot(y.astype(_BF16), wo_ref[...])

    _stream_row_tiles(
        functools.partial(_tile_copy, hbm=x_hbm, acc_ref=acc_ref, sems=in_sems, tm=tm, tiles=tiles,
                          tail_rows=tail_tokens, to_hbm=False),
        functools.partial(_tile_copy, hbm=out_hbm, acc_ref=acc_ref, sems=out_sems, tm=tm, tiles=tiles,
                          tail_rows=tm, to_hbm=True),
        body)


def _ffn_kernel(tm, tiles, out_tail_rows, n_chunks, hprev_ref, h_hbm, hnext_ref, gain_ref, wg_ref,
                wua_ref, wub_ref, cw_ref, cb_ref, wd_ref, out_hbm, xext_ref, acc_ref, in_sems, out_sems):
    c = FFN_COL_TILE
    j = pl.program_id(1)
    acc = acc_ref.at[lax.rem(pl.program_id(0), 2)]

    def act(g_ext, u, lo):
        g = _conv3(g_ext, cw_ref[:, lo:lo + c], tm) + cb_ref[:, lo:lo + c]
        return (_gelu_exact(g) * u).astype(_BF16)

    def body():
        @pl.when(j == 0)
        def _():
            _fill_from_slab(hprev_ref, acc, hnext_ref, gain_ref, xext_ref, tm)

        @pl.when(j < n_chunks // 2)
        def _():
            x, xc = xext_ref[...], xext_ref[HALO:HALO + tm, :]
            g2 = _dot(x, wg_ref[...])
            ua, ub = _dot(xc, wua_ref[...]), _dot(xc, wub_ref[...])
            a = jnp.concatenate([act(g2[:, :c], ua, 0), act(g2[:, c:], ub, c)], axis=1)
            acc[...] += _dot(a, wd_ref[...])

        if n_chunks % 2:
            @pl.when(j == n_chunks // 2)
            def _():
                g = _dot(xext_ref[...], wg_ref[:, 0:c])
                ua = _dot(xext_ref[HALO:HALO + tm, :], wua_ref[...])
                acc[...] += _dot(act(g, ua, 0), wd_ref[0:c, :])

    _stream_row_tiles(
        functools.partial(_tile_copy, hbm=h_hbm, acc_ref=acc_ref, sems=in_sems, tm=tm, tiles=tiles,
                          tail_rows=tm, to_hbm=False),
        functools.partial(_tile_copy, hbm=out_hbm, acc_ref=acc_ref, sems=out_sems, tm=tm, tiles=tiles,
                          tail_rows=out_tail_rows, to_hbm=True),
        body)


def _fused_params():
    return pltpu.CompilerParams(dimension_semantics=("arbitrary", "arbitrary"),
                                vmem_limit_bytes=V7X_VMEM_LIMIT_BYTES)


def _fused_scratch(tm, d):
    return [pltpu.VMEM((tm + 2 * HALO, d), _BF16), pltpu.VMEM((2, tm, d), _F32),
            pltpu.SemaphoreType.DMA((2,)), pltpu.SemaphoreType.DMA((2,))]


def _weight_specs(layer, d, hidden, n_up, col_tile):
    n_j = hidden // col_tile
    ups = [pl.BlockSpec((None, d, col_tile),
                        functools.partial(lambda i, j, k: (layer, 0, j + k * n_j), k=k))
           for k in range(n_up)]
    down = pl.BlockSpec((None, col_tile, d), lambda i, j: (layer, j, 0))
    return ups, down


def _mixer_from_tokens(x, meta_tokens, gain, w_in, conv_w, w_out, layer, tm, cast_weights):
    bsz, seq, d = x.shape
    tp = seq + BLOCK
    tiles = tp // tm
    tail_tokens = seq - (tiles - 1) * tm
    assert tiles * tm == tp and 0 < tail_tokens < tm and tail_tokens % HALO == 0 and tm % HALO == 0
    halo_blocks_per_tile = tm // HALO
    col_tile = MIXER_COL_TILE

    def prev_map(i, j):
        b, q = _row_tile_index(i, tiles)
        return b, jnp.maximum(q * halo_blocks_per_tile - 1, 0), 0

    def next_map(i, j):
        b, q = _row_tile_index(i, tiles)
        wrap = q == tiles - 1
        return (jnp.where(wrap, lax.rem(b + 1, bsz), b),
                jnp.where(wrap, 0, (q + 1) * halo_blocks_per_tile), 0)

    ups, down = _weight_specs(layer, d, d, 3, col_tile)
    n_j = d // col_tile
    casts = [_CastPlan(w, l, bsz * tiles * n_j, lambda i, j: i * n_j + j) for w, l in cast_weights]
    in_specs = [
        pl.BlockSpec((1, HALO, d), prev_map),
        pl.BlockSpec(memory_space=pl.ANY),
        pl.BlockSpec((1, HALO, d), next_map),
        pl.BlockSpec((N_META, d), lambda i, j: (0, 0)),
        pl.BlockSpec((1, d), lambda i, j: (0, 0)),
        *ups,
        pl.BlockSpec((None, 3, col_tile), lambda i, j: (layer, 0, j)),
        down,
        *[c.in_spec for c in casts],
    ]
    out, *cast_out = pl.pallas_call(
        functools.partial(_mixer_kernel, tm, tiles, tail_tokens, len(casts)),
        grid=(bsz * tiles, n_j),
        in_specs=in_specs,
        out_specs=[pl.BlockSpec(memory_space=pl.ANY), *[c.out_spec for c in casts]],
        out_shape=[jax.ShapeDtypeStruct((bsz, tp, d), _F32), *[c.out_shape for c in casts]],
        scratch_shapes=_fused_scratch(tm, d),
        compiler_params=_fused_params(),
    )(x, x, x, meta_tokens, gain.reshape(1, d), w_in, w_in, w_in, conv_w, w_out,
      *[c.operand for c in casts])
    return out, cast_out


def _conv_ffn(h, gain, w_up, conv_w, conv_b, w_down, layer, conv_layer, tm, out_rows):
    bsz, tp, d = h.shape
    hidden = w_down.shape[1]
    tiles = tp // tm
    out_tail_rows = out_rows - (tiles - 1) * tm
    assert tiles * tm == tp and tm % HALO == 0 and 0 < out_tail_rows <= tm and out_tail_rows % 8 == 0
    halo_blocks_per_tile = tm // HALO
    halo_blocks_per_slab = tp // HALO
    n_halo_blocks = bsz * halo_blocks_per_slab
    col_tile = FFN_COL_TILE

    def halo_map(i, j, offset):
        blk = lax.rem(i * halo_blocks_per_tile + offset + n_halo_blocks, n_halo_blocks)
        return lax.div(blk, halo_blocks_per_slab), lax.rem(blk, halo_blocks_per_slab), 0

    n_chunks = hidden // col_tile
    assert n_chunks * col_tile == hidden and w_up.shape[2] == 2 * hidden
    in_specs = [
        pl.BlockSpec((1, HALO, d), functools.partial(halo_map, offset=-1)),
        pl.BlockSpec(memory_space=pl.ANY),
        pl.BlockSpec((1, HALO, d), functools.partial(halo_map, offset=halo_blocks_per_tile)),
        pl.BlockSpec((1, d), lambda i, j: (0, 0)),
        pl.BlockSpec((None, d, 2 * col_tile), lambda i, j: (layer, 0, j)),
        pl.BlockSpec((None, d, col_tile), lambda i, j: (layer, 0, n_chunks + 2 * j)),
        pl.BlockSpec((None, d, col_tile),
                     lambda i, j: (layer, 0, jnp.minimum(n_chunks + 2 * j + 1, 2 * n_chunks - 1))),
        pl.BlockSpec((None, 3, 2 * col_tile), lambda i, j: (conv_layer, 0, j)),
        pl.BlockSpec((None, 1, 2 * col_tile), lambda i, j: (conv_layer, 0, j)),
        pl.BlockSpec((None, 2 * col_tile, d), lambda i, j: (layer, j, 0)),
    ]
    return pl.pallas_call(
        functools.partial(_ffn_kernel, tm, tiles, out_tail_rows, n_chunks),
        grid=(bsz * tiles, (n_chunks + 1) // 2),
        in_specs=in_specs,
        out_specs=pl.BlockSpec(memory_space=pl.ANY),
        out_shape=jax.ShapeDtypeStruct((bsz, out_rows, d), _F32),
        scratch_shapes=_fused_scratch(tm, d),
        compiler_params=_fused_params(),
    )(h, h, h, gain.reshape(1, d), w_up, w_up, w_up, conv_w, conv_b.reshape(conv_b.shape[0], 1, hidden),
      w_down)


def _qkv_kernel(n_norm_cols, h_ref, gain_ref, w_ref, hgain_ref, out_ref, xn_ref):
    _norm_rows_into(xn_ref, 0, h_ref, h_ref.shape[0], gain_ref[...])
    for c in range(w_ref.shape[1] // QKV_COL_TILE):
        acc = _dot(xn_ref[...], w_ref[:, c * QKV_COL_TILE:(c + 1) * QKV_COL_TILE])
        for hd in range(QKV_COL_TILE // HEAD_DIM):
            lo = hd * HEAD_DIM
            cols = slice(c * QKV_COL_TILE + lo, c * QKV_COL_TILE + lo + HEAD_DIM)
            head = acc[:, lo:lo + HEAD_DIM]
            if c * QKV_COL_TILE + lo < n_norm_cols:
                head = _rmsnorm_rows(head, hgain_ref[:, cols])
            out_ref[:, cols] = head.astype(_BF16)


def _qkv_proj(h, gain, w_qkv, layer, head_gain_cols, tm):
    m, d = h.shape
    n_out = w_qkv.shape[2]
    return pl.pallas_call(
        functools.partial(_qkv_kernel, D_MODEL + KV_DIM),
        grid=(m // tm,),
        in_specs=[
            pl.BlockSpec((tm, d), lambda i: (i, 0)),
            pl.BlockSpec((1, d), lambda i: (0, 0)),
            pl.BlockSpec((None, d, n_out), lambda i: (layer, 0, 0), pipeline_mode=pl.Buffered(1)),
            pl.BlockSpec((1, n_out), lambda i: (0, 0)),
        ],
        out_specs=pl.BlockSpec((tm, n_out), lambda i: (i, 0)),
        out_shape=jax.ShapeDtypeStruct((m, n_out), _BF16),
        scratch_shapes=[pltpu.VMEM((tm, d), _BF16)],
        compiler_params=pltpu.CompilerParams(
            dimension_semantics=("arbitrary",), vmem_limit_bytes=V7X_VMEM_LIMIT_BYTES),
    )(h, gain.reshape(1, d), w_qkv, head_gain_cols)


def _bias_kernel(n_casts, band_codes_ref, tail_codes_ref, tbl_ref, *rest):
    out_ref = rest[n_casts]
    _run_casts(rest[:n_casts] + rest[n_casts + 1:])
    head = pl.program_id(0)

    def lookup(codes):
        acc = jnp.zeros(codes.shape, _F32)
        for code in range(N_BUCKETS + 2):
            acc = jnp.where(codes == code, tbl_ref[head, code] * LOG2_E, acc)
        return acc

    band = lookup(band_codes_ref[...])
    tails = [lookup(tail_codes_ref[v]) for v in range(3)]
    masked = jnp.full((BLOCK, BLOCK), NEG_INF, _F32)
    for variant, valid_blocks, tail in ((0, (2,), 0), (1, (1, 2), 1), (2, (0, 1, 2), 2), (3, (0, 1), 2)):
        for blk in range(3):
            cols = slice(blk * BLOCK, (blk + 1) * BLOCK)
            out_ref[variant, 0, :, cols] = band[:, cols] if blk in valid_blocks else masked
        out_ref[variant, 0, :, 3 * BLOCK:] = tails[tail]


def _attn_bias(band_codes, tail_codes, table, cast_weights):
    casts = [_CastPlan(w, l, N_HEADS, lambda h: h) for w, l in cast_weights]
    bias, *cast_out = pl.pallas_call(
        functools.partial(_bias_kernel, len(casts)),
        grid=(N_HEADS,),
        in_specs=[
            pl.BlockSpec((BLOCK, 3 * BLOCK), lambda h: (0, 0)),
            pl.BlockSpec((3, BLOCK, BLOCK), lambda h: (0, 0, 0)),
            pl.BlockSpec(memory_space=pltpu.SMEM),
            *[c.in_spec for c in casts],
        ],
        out_specs=[pl.BlockSpec((N_BIAS_VARIANTS, 1, BLOCK, KEYS), lambda h: (0, h, 0, 0)),
                   *[c.out_spec for c in casts]],
        out_shape=[jax.ShapeDtypeStruct((N_BIAS_VARIANTS, N_HEADS, BLOCK, KEYS), _F32),
                   *[c.out_shape for c in casts]],
    )(band_codes, tail_codes, table, *[c.operand for c in casts])
    return bias, cast_out


def _attn_chain(q_ref, k_refs, v_refs, bias_ref, out_ref, slab, k_slabs, g):
    kv_cols = slice(g * HEAD_DIM, (g + 1) * HEAD_DIM)
    dead = jnp.zeros((BLOCK - N_META, HEAD_DIM), _BF16)
    q = jnp.concatenate(
        [q_ref[slab, :, (g * GROUP + r) * HEAD_DIM:(g * GROUP + r + 1) * HEAD_DIM] for r in range(GROUP)],
        axis=0)
    k = jnp.concatenate([ref[ks, :, kv_cols] for ref, ks in zip(k_refs, k_slabs)] + [dead], axis=0)
    v = jnp.concatenate([ref[ks, :, kv_cols] for ref, ks in zip(v_refs, k_slabs)] + [dead], axis=0)
    s = lax.dot_general(q, k, (((1,), (1,)), ((), ())), preferred_element_type=_F32)
    s = s + bias_ref[0, g * GROUP:(g + 1) * GROUP].reshape(GROUP * BLOCK, KEYS)
    e = jnp.exp2(s - jnp.max(s, axis=-1, keepdims=True))
    denom = jnp.sum(e, axis=-1, keepdims=True)
    o = _dot(e.astype(_BF16), v) / denom
    for r in range(GROUP):
        out_ref[slab, :, (g * GROUP + r) * HEAD_DIM:(g * GROUP + r + 1) * HEAD_DIM] = (
            o[r * BLOCK:(r + 1) * BLOCK].astype(_BF16))


def _attn_kernel(n_casts, q_ref, kp_ref, kc_ref, kn_ref, km_ref, vp_ref, vc_ref, vn_ref, vm_ref, bias_ref,
                 *rest):
    out_ref = rest[n_casts]
    _run_casts(rest[:n_casts] + rest[n_casts + 1:])
    n = pl.program_id(0)
    n_slabs = q_ref.shape[0]
    k_refs = (kp_ref, kc_ref, kn_ref, km_ref)
    v_refs = (vp_ref, vc_ref, vn_ref, vm_ref)

    def all_chains(first_block):
        for slab in range(n_slabs):
            if first_block:
                k_slabs = (slab, slab, (slab + 1) % n_slabs, slab)
            else:
                k_slabs = (slab, slab, slab, (slab - 1) % n_slabs)
            for g in range(N_KV_HEADS):
                _attn_chain(q_ref, k_refs, v_refs, bias_ref, out_ref, slab, k_slabs, g)

    @pl.when(n == 0)
    def _():
        all_chains(True)
        out_ref[:, 0:LEAD, :] = jnp.zeros((n_slabs, LEAD, D_MODEL), _BF16)

    @pl.when(n > 0)
    def _():
        all_chains(False)


def _attention(qkv, bias, nb, cast_weights):
    bsz = qkv.shape[0]
    casts = [_CastPlan(w, l, nb, lambda n: n) for w, l in cast_weights]
    k_col, v_col = D_MODEL // KV_DIM, D_MODEL // KV_DIM + 1

    def slab_block(n):
        return jnp.where(n == 0, nb - 1, n - 1)

    def blk(col, shift):
        return pl.BlockSpec(
            (bsz, BLOCK, KV_DIM), lambda n: (0, slab_block(jnp.clip(n + shift, 0, nb - 1)), col))

    def meta(col):
        return pl.BlockSpec(
            (bsz, N_META, KV_DIM), lambda n: (0, (nb - 1) * (BLOCK // N_META) + LEAD // N_META, col))

    out, *cast_out = pl.pallas_call(
        functools.partial(_attn_kernel, len(casts)),
        grid=(nb,),
        in_specs=[
            pl.BlockSpec((bsz, BLOCK, D_MODEL), lambda n: (0, slab_block(n), 0)),
            blk(k_col, -1), blk(k_col, 0), blk(k_col, 1), meta(k_col),
            blk(v_col, -1), blk(v_col, 0), blk(v_col, 1), meta(v_col),
            pl.BlockSpec((1, N_HEADS, BLOCK, KEYS),
                         lambda n: (jnp.where(n == nb - 1, 3, jnp.minimum(n, 2)), 0, 0, 0)),
            *[c.in_spec for c in casts],
        ],
        out_specs=[pl.BlockSpec((bsz, BLOCK, D_MODEL), lambda n: (0, slab_block(n), 0)),
                   *[c.out_spec for c in casts]],
        out_shape=[jax.ShapeDtypeStruct((bsz, nb * BLOCK, D_MODEL), _BF16),
                   *[c.out_shape for c in casts]],
        compiler_params=pltpu.CompilerParams(
            dimension_semantics=("arbitrary",), vmem_limit_bytes=V7X_VMEM_LIMIT_BYTES),
    )(qkv, qkv, qkv, qkv, qkv, qkv, qkv, qkv, qkv, bias, *[c.operand for c in casts])
    return out, cast_out


def _out_proj_kernel(a_ref, w_ref, h_ref, out_ref):
    out_ref[...] = h_ref[...] + _dot(a_ref[...], w_ref[...])


def _out_proj(a, w, layer, h, tm):
    m, d = h.shape
    return pl.pallas_call(
        _out_proj_kernel,
        grid=(m // tm,),
        in_specs=[
            pl.BlockSpec((tm, d), lambda i: (i, 0)),
            pl.BlockSpec((None, d, d), lambda i: (layer, 0, 0), pipeline_mode=pl.Buffered(1)),
            pl.BlockSpec((tm, d), lambda i: (i, 0)),
        ],
        out_specs=pl.BlockSpec((tm, d), lambda i: (i, 0)),
        out_shape=jax.ShapeDtypeStruct((m, d), _F32),
        compiler_params=pltpu.CompilerParams(
            dimension_semantics=("arbitrary",), vmem_limit_bytes=V7X_VMEM_LIMIT_BYTES),
    )(a, w, h)


def _t5_bucket(rel):
    half = N_BUCKETS // 2
    max_exact = half // 2
    side = jnp.where(rel > 0, half, 0)
    n = jnp.abs(rel)
    nf = jnp.maximum(n, 1).astype(_F32)
    large = max_exact + (jnp.log(nf / max_exact) / math.log(MAX_DISTANCE / max_exact)
                         * (half - max_exact)).astype(jnp.int32)
    large = jnp.minimum(large, half - 1)
    return side + jnp.where(n < max_exact, n, large)


def _bias_codes():
    qi = jnp.arange(BLOCK, dtype=jnp.int32)[:, None]
    col = jnp.arange(3 * BLOCK, dtype=jnp.int32)[None, :]
    rel_band = col - BLOCK - qi
    band = jnp.where(jnp.abs(rel_band) <= WINDOW, _t5_bucket(rel_band), CODE_MASKED)
    tcol = jnp.arange(BLOCK, dtype=jnp.int32)[None, :]
    tails = []
    for n_rep in range(3):
        qpos = n_rep * BLOCK + qi
        codes = jnp.where(tcol < N_META, _t5_bucket(LEAD + tcol - qpos), CODE_MASKED)
        tails.append(jnp.where(tcol == N_META, CODE_SINK, codes))
    return band.astype(jnp.int32), jnp.stack(tails).astype(jnp.int32)


def kernel(x, meta_tokens, rel_bias_table, norm_mix, norm_ffn, conv_in_w, conv_dw, conv_out_w,
           attn_qkv, attn_q_gain, attn_k_gain, attn_sink, attn_o, ffn_up, ffn_dw, ffn_dw_b, ffn_down):
    bsz, seq, d = x.shape
    tp = seq + BLOCK
    nb = tp // BLOCK
    tm = tp // FFN_ROW_TILES_PER_SLAB
    assert tm * FFN_ROW_TILES_PER_SLAB == tp and d == D_MODEL and seq % BLOCK == 0 and nb >= 4

    table = jnp.concatenate([
        rel_bias_table.T.astype(_F32), jnp.full((N_HEADS, 1), NEG_INF, _F32),
        attn_sink[0].astype(_F32)[:, None]], axis=1)
    bias, (in_w, out_w) = _attn_bias(*_bias_codes(), table, [(conv_in_w, 0), (conv_out_w, 0)])

    h, (up0, down0, qkv_w, o_w) = _mixer_from_tokens(
        x, meta_tokens.astype(x.dtype), norm_mix[0], in_w, conv_dw, out_w, 0,
        tp // MIXER_ROW_TILES_PER_SLAB, [(ffn_up, 0), (ffn_down, 0), (attn_qkv, 0), (attn_o, 0)])
    h = _conv_ffn(h, norm_ffn[0], up0, ffn_dw, ffn_dw_b, down0, 0, 0, tm, tp)

    q_scale = HEAD_DIM ** -0.5 * LOG2_E
    head_gain_cols = jnp.concatenate([
        jnp.tile(attn_q_gain[0] * q_scale, N_HEADS), jnp.tile(attn_k_gain[0], N_KV_HEADS),
        jnp.ones((KV_DIM,), _F32)]).reshape(1, QKV_DIM)
    h2 = h.reshape(bsz * tp, d)
    qkv = _qkv_proj(h2, norm_mix[1], qkv_w, 0, head_gain_cols, tp // QKV_ROW_TILES_PER_SLAB)
    o, (up1, down1) = _attention(qkv.reshape(bsz, tp, QKV_DIM), bias, nb, [(ffn_up, 1), (ffn_down, 1)])
    h2 = _out_proj(o.reshape(bsz * tp, d), o_w, 0, h2, tp // OUT_PROJ_ROW_TILES_PER_SLAB)
    return _conv_ffn(h2.reshape(bsz, tp, d), norm_ffn[1], up1, ffn_dw, ffn_dw_b, down1, 0, 1, tm, seq)
```

```python
import functools
import math

import jax
import jax.numpy as jnp
from jax import lax
from jax.experimental import pallas as pl
from jax.experimental.pallas import tpu as pltpu

D_MODEL = 2048
N_META = 16
N_HEADS = 16
N_KV_HEADS = 4
HEAD_DIM = D_MODEL // N_HEADS
GROUP = N_HEADS // N_KV_HEADS
KV_DIM = N_KV_HEADS * HEAD_DIM
QKV_DIM = D_MODEL + 2 * KV_DIM
WINDOW = 128
BLOCK = 128
N_BUCKETS = 32
MAX_DISTANCE = 128
D_FF = 5632
EPS = 1e-6
LEAD = BLOCK - N_META

F32_SUBLANES = 8
BF16_SUBLANES = 16
HALO = BF16_SUBLANES
NORM_ROWS = 32
FFN_ROW_TILES_PER_SLAB = 4
MIXER_ROW_TILES_PER_SLAB = 4
QKV_ROW_TILES_PER_SLAB = 4
OUT_PROJ_ROW_TILES_PER_SLAB = 8
MIXER_COL_TILE = 512
FFN_COL_TILE = 512
QKV_COL_TILE = 512
KEYS = 4 * BLOCK
SINK_COL = 3 * BLOCK + N_META
CODE_MASKED = N_BUCKETS
CODE_SINK = N_BUCKETS + 1
N_BIAS_VARIANTS = 4
NEG_INF = -1e30
LOG2_E = math.log2(math.e)
V7X_VMEM_LIMIT_BYTES = 56 * 1024 * 1024

_BF16 = jnp.bfloat16
_F32 = jnp.float32


def _dot(a, b):
    return jnp.dot(a, b, preferred_element_type=_F32)


def _rmsnorm_rows(v, gain):
    return v * lax.rsqrt(jnp.mean(v * v, axis=-1, keepdims=True) + EPS) * gain


def _norm_rows_into(xext_ref, row0, src_ref, n_rows, gain):
    for r in range(0, n_rows, NORM_ROWS):
        rows = min(NORM_ROWS, n_rows - r)
        xext_ref[row0 + r:row0 + r + rows, :] = (
            _rmsnorm_rows(src_ref[r:r + rows, :], gain).astype(_BF16))


def _gelu_exact(v):
    return 0.5 * v * (1.0 + lax.erf(v * math.sqrt(0.5)))


def _row_tile_index(i, tiles):
    return lax.div(i, tiles), lax.rem(i, tiles)


def _tile_copy(op, hbm, acc_ref, sems, tm, tiles, tail_rows, to_hbm, t):
    b, q = _row_tile_index(t, tiles)
    slot = lax.rem(t, 2)

    def copy(rows):
        hbm_rows = hbm.at[b, pl.ds(q * tm, rows), :]
        acc_rows = acc_ref.at[slot, pl.ds(0, rows), :]
        src, dst = (acc_rows, hbm_rows) if to_hbm else (hbm_rows, acc_rows)
        op(pltpu.make_async_copy(src, dst, sems.at[slot]))

    if tail_rows == tm:
        copy(tm)
    else:
        pl.when(q < tiles - 1)(lambda: copy(tm))
        pl.when(q == tiles - 1)(lambda: copy(tail_rows))


def _stream_row_tiles(load, store, body):
    i, j = pl.program_id(0), pl.program_id(1)
    n_i, n_j = pl.num_programs(0), pl.num_programs(1)
    start, wait = (lambda c: c.start()), (lambda c: c.wait())

    pl.when(jnp.logical_and(i == 0, j == 0))(lambda: load(start, t=i))
    pl.when(j == 0)(lambda: load(wait, t=i))
    pl.when(jnp.logical_and(j == 1, i >= 1))(lambda: store(wait, t=i - 1))
    pl.when(jnp.logical_and(j == 1, i + 1 < n_i))(lambda: load(start, t=i + 1))
    body()
    pl.when(j == n_j - 1)(lambda: store(start, t=i))
    pl.when(jnp.logical_and(j == n_j - 1, i == n_i - 1))(lambda: store(wait, t=i))


def _fill_from_slab(hprev_ref, acc, hnext_ref, gain_ref, xext_ref, tm):
    gain = gain_ref[...]
    xext_ref[0:HALO, :] = _rmsnorm_rows(hprev_ref[0], gain).astype(_BF16)
    _norm_rows_into(xext_ref, HALO, acc, tm, gain)
    xext_ref[HALO + tm:, :] = _rmsnorm_rows(hnext_ref[0], gain).astype(_BF16)


def _fill_from_tokens(xprev_ref, acc, xnext_ref, meta_ref, gain_ref, xext_ref, tm, tiles, tail_tokens):
    q = lax.rem(pl.program_id(0), tiles)
    gain = gain_ref[...]
    meta = meta_ref[...]
    meta_n = _rmsnorm_rows(meta, gain).astype(_BF16)

    @pl.when(q == 0)
    def _():
        xext_ref[0:HALO, :] = meta_n

    @pl.when(q > 0)
    def _():
        xext_ref[0:HALO, :] = _rmsnorm_rows(xprev_ref[0], gain).astype(_BF16)

    @pl.when(q < tiles - 1)
    def _():
        _norm_rows_into(xext_ref, HALO, acc, tm, gain)

    @pl.when(q == tiles - 1)
    def _():
        _norm_rows_into(xext_ref, HALO, acc, tail_tokens, gain)
        xext_ref[HALO + tail_tokens:HALO + tail_tokens + LEAD, :] = jnp.zeros((LEAD, D_MODEL), _BF16)
        xext_ref[HALO + tail_tokens + LEAD:HALO + tm, :] = meta_n
        acc[tail_tokens:tail_tokens + LEAD, :] = jnp.zeros((LEAD, D_MODEL), _F32)
        acc[tail_tokens + LEAD:tm, :] = meta

    xext_ref[HALO + tm:, :] = _rmsnorm_rows(xnext_ref[0], gain).astype(_BF16)


class _CastPlan:
    def __init__(self, w, layer, n_steps, step_of_grid):
        _, rows, cols = w.shape
        n_blocks = next(nb for nb in range(min(n_steps, rows // BF16_SUBLANES), 0, -1)
                        if rows % nb == 0 and (rows // nb) % BF16_SUBLANES == 0)
        block_rows = rows // n_blocks

        def block(*grid_idx):
            return lax.div(step_of_grid(*grid_idx) * n_blocks, n_steps)

        self.operand = w
        self.in_spec = pl.BlockSpec((None, block_rows, cols), lambda *g: (layer, block(*g), 0))
        self.out_spec = pl.BlockSpec((None, block_rows, cols), lambda *g: (0, block(*g), 0))
        self.out_shape = jax.ShapeDtypeStruct((1, rows, cols), _BF16)


def _run_casts(refs):
    n = len(refs) // 2
    for src, dst in zip(refs[:n], refs[n:]):
        dst[...] = src[...].astype(_BF16)


def _conv3(v_ext, w, tm):
    return (v_ext[HALO - 1:HALO - 1 + tm] * w[0:1]
            + v_ext[HALO:HALO + tm] * w[1:2]
            + v_ext[HALO + 1:HALO + 1 + tm] * w[2:3])


def _mixer_kernel(tm, tiles, tail_tokens, n_casts, xprev_ref, x_hbm, xnext_ref, meta_ref, gain_ref, wb_ref,
                  wc_ref, wh_ref, cw_ref, wo_ref, *rest):
    cast_srcs, out_hbm, cast_dsts = rest[:n_casts], rest[n_casts], rest[n_casts + 1:2 * n_casts + 1]
    xext_ref, acc_ref, in_sems, out_sems = rest[2 * n_casts + 1:]
    acc = acc_ref.at[lax.rem(pl.program_id(0), 2)]
    _run_casts(cast_srcs + cast_dsts)

    def body():
        @pl.when(pl.program_id(1) == 0)
        def _():
            _fill_from_tokens(xprev_ref, acc, xnext_ref, meta_ref, gain_ref, xext_ref, tm, tiles,
                              tail_tokens)

        x = xext_ref[...]
        ch = _conv3(_dot(x, wc_ref[...]) * _dot(x, wh_ref[...]), cw_ref[...], tm)
        y = _dot(xext_ref[HALO:HALO + tm, :], wb_ref[...]) * ch
        acc[...] += _dot(y.astype(_BF16), wo_ref[...])

    _stream_row_tiles(
        functools.partial(_tile_copy, hbm=x_hbm, acc_ref=acc_ref, sems=in_sems, tm=tm, tiles=tiles,
                          tail_rows=tail_tokens, to_hbm=False),
        functools.partial(_tile_copy, hbm=out_hbm, acc_ref=acc_ref, sems=out_sems, tm=tm, tiles=tiles,
                          tail_rows=tm, to_hbm=True),
        body)


def _ffn_kernel(tm, tiles, out_tail_rows, n_chunks, hprev_ref, h_hbm, hnext_ref, gain_ref, wg_ref,
                wua_ref, wub_ref, cw_ref, cb_ref, wd_ref, out_hbm, xext_ref, acc_ref, in_sems, out_sems):
    c = FFN_COL_TILE
    j = pl.program_id(1)
    acc = acc_ref.at[lax.rem(pl.program_id(0), 2)]

    def act(g_ext, u, lo):
        g = _conv3(g_ext, cw_ref[:, lo:lo + c], tm) + cb_ref[:, lo:lo + c]
        return (_gelu_exact(g) * u).astype(_BF16)

    def body():
        @pl.when(j == 0)
        def _():
            _fill_from_slab(hprev_ref, acc, hnext_ref, gain_ref, xext_ref, tm)

        @pl.when(j < n_chunks // 2)
        def _():
            x, xc = xext_ref[...], xext_ref[HALO:HALO + tm, :]
            g2 = _dot(x, wg_ref[...])
            ua, ub = _dot(xc, wua_ref[...]), _dot(xc, wub_ref[...])
            a = jnp.concatenate([act(g2[:, :c], ua, 0), act(g2[:, c:], ub, c)], axis=1)
            acc[...] += _dot(a, wd_ref[...])

        if n_chunks % 2:
            @pl.when(j == n_chunks // 2)
            def _():
                g = _dot(xext_ref[...], wg_ref[:, 0:c])
                ua = _dot(xext_ref[HALO:HALO + tm, :], wua_ref[...])
                acc[...] += _dot(act(g, ua, 0), wd_ref[0:c, :])

    _stream_row_tiles(
        functools.partial(_tile_copy, hbm=h_hbm, acc_ref=acc_ref, sems=in_sems, tm=tm, tiles=tiles,
                          tail_rows=tm, to_hbm=False),
        functools.partial(_tile_copy, hbm=out_hbm, acc_ref=acc_ref, sems=out_sems, tm=tm, tiles=tiles,
                          tail_rows=out_tail_rows, to_hbm=True),
        body)


def _fused_params():
    return pltpu.CompilerParams(dimension_semantics=("arbitrary", "arbitrary"),
                                vmem_limit_bytes=V7X_VMEM_LIMIT_BYTES)


def _fused_scratch(tm, d):
    return [pltpu.VMEM((tm + 2 * HALO, d), _BF16), pltpu.VMEM((2, tm, d), _F32),
            pltpu.SemaphoreType.DMA((2,)), pltpu.SemaphoreType.DMA((2,))]


def _weight_specs(layer, d, hidden, n_up, col_tile):
    n_j = hidden // col_tile
    ups = [pl.BlockSpec((None, d, col_tile),
                        functools.partial(lambda i, j, k: (layer, 0, j + k * n_j), k=k))
           for k in range(n_up)]
    down = pl.BlockSpec((None, col_tile, d), lambda i, j: (layer, j, 0))
    return ups, down


def _mixer_from_tokens(x, meta_tokens, gain, w_in, conv_w, w_out, layer, tm, cast_weights):
    bsz, seq, d = x.shape
    tp = seq + BLOCK
    tiles = tp // tm
    tail_tokens = seq - (tiles - 1) * tm
    assert tiles * tm == tp and 0 < tail_tokens < tm and tail_tokens % HALO == 0 and tm % HALO == 0
    halo_blocks_per_tile = tm // HALO
    col_tile = MIXER_COL_TILE

    def prev_map(i, j):
        b, q = _row_tile_index(i, tiles)
        return b, jnp.maximum(q * halo_blocks_per_tile - 1, 0), 0

    def next_map(i, j):
        b, q = _row_tile_index(i, tiles)
        wrap = q == tiles - 1
        return (jnp.where(wrap, lax.rem(b + 1, bsz), b),
                jnp.where(wrap, 0, (q + 1) * halo_blocks_per_tile), 0)

    ups, down = _weight_specs(layer, d, d, 3, col_tile)
    n_j = d // col_tile
    casts = [_CastPlan(w, l, bsz * tiles * n_j, lambda i, j: i * n_j + j) for w, l in cast_weights]
    in_specs = [
        pl.BlockSpec((1, HALO, d), prev_map),
        pl.BlockSpec(memory_space=pl.ANY),
        pl.BlockSpec((1, HALO, d), next_map),
        pl.BlockSpec((N_META, d), lambda i, j: (0, 0)),
        pl.BlockSpec((1, d), lambda i, j: (0, 0)),
        *ups,
        pl.BlockSpec((None, 3, col_tile), lambda i, j: (layer, 0, j)),
        down,
        *[c.in_spec for c in casts],
    ]
    out, *cast_out = pl.pallas_call(
        functools.partial(_mixer_kernel, tm, tiles, tail_tokens, len(casts)),
        grid=(bsz * tiles, n_j),
        in_specs=in_specs,
        out_specs=[pl.BlockSpec(memory_space=pl.ANY), *[c.out_spec for c in casts]],
        out_shape=[jax.ShapeDtypeStruct((bsz, tp, d), _F32), *[c.out_shape for c in casts]],
        scratch_shapes=_fused_scratch(tm, d),
        compiler_params=_fused_params(),
    )(x, x, x, meta_tokens, gain.reshape(1, d), w_in, w_in, w_in, conv_w, w_out,
      *[c.operand for c in casts])
    return out, cast_out


def _conv_ffn(h, gain, w_up, conv_w, conv_b, w_down, layer, conv_layer, tm, out_rows):
    bsz, tp, d = h.shape
    hidden = w_down.shape[1]
    tiles = tp // tm
    out_tail_rows = out_rows - (tiles - 1) * tm
    assert tiles * tm == tp and tm % HALO == 0 and 0 < out_tail_rows <= tm and out_tail_rows % F32_SUBLANES == 0
    halo_blocks_per_tile = tm // HALO
    halo_blocks_per_slab = tp // HALO
    n_halo_blocks = bsz * halo_blocks_per_slab
    col_tile = FFN_COL_TILE

    def halo_map(i, j, offset):
        blk = lax.rem(i * halo_blocks_per_tile + offset + n_halo_blocks, n_halo_blocks)
        return lax.div(blk, halo_blocks_per_slab), lax.rem(blk, halo_blocks_per_slab), 0

    n_chunks = hidden // col_tile
    assert n_chunks * col_tile == hidden and w_up.shape[2] == 2 * hidden
    in_specs = [
        pl.BlockSpec((1, HALO, d), functools.partial(halo_map, offset=-1)),
        pl.BlockSpec(memory_space=pl.ANY),
        pl.BlockSpec((1, HALO, d), functools.partial(halo_map, offset=halo_blocks_per_tile)),
        pl.BlockSpec((1, d), lambda i, j: (0, 0)),
        pl.BlockSpec((None, d, 2 * col_tile), lambda i, j: (layer, 0, j)),
        pl.BlockSpec((None, d, col_tile), lambda i, j: (layer, 0, n_chunks + 2 * j)),
        pl.BlockSpec((None, d, col_tile),
                     lambda i, j: (layer, 0, jnp.minimum(n_chunks + 2 * j + 1, 2 * n_chunks - 1))),
        pl.BlockSpec((None, 3, 2 * col_tile), lambda i, j: (conv_layer, 0, j)),
        pl.BlockSpec((None, 1, 2 * col_tile), lambda i, j: (conv_layer, 0, j)),
        pl.BlockSpec((None, 2 * col_tile, d), lambda i, j: (layer, j, 0)),
    ]
    return pl.pallas_call(
        functools.partial(_ffn_kernel, tm, tiles, out_tail_rows, n_chunks),
        grid=(bsz * tiles, (n_chunks + 1) // 2),
        in_specs=in_specs,
        out_specs=pl.BlockSpec(memory_space=pl.ANY),
        out_shape=jax.ShapeDtypeStruct((bsz, out_rows, d), _F32),
        scratch_shapes=_fused_scratch(tm, d),
        compiler_params=_fused_params(),
    )(h, h, h, gain.reshape(1, d), w_up, w_up, w_up, conv_w, conv_b.reshape(conv_b.shape[0], 1, hidden),
      w_down)


def _qkv_kernel(n_norm_cols, h_ref, gain_ref, w_ref, hgain_ref, out_ref, xn_ref):
    _norm_rows_into(xn_ref, 0, h_ref, h_ref.shape[0], gain_ref[...])
    for c in range(w_ref.shape[1] // QKV_COL_TILE):
        acc = _dot(xn_ref[...], w_ref[:, c * QKV_COL_TILE:(c + 1) * QKV_COL_TILE])
        for hd in range(QKV_COL_TILE // HEAD_DIM):
            lo = hd * HEAD_DIM
            cols = slice(c * QKV_COL_TILE + lo, c * QKV_COL_TILE + lo + HEAD_DIM)
            head = acc[:, lo:lo + HEAD_DIM]
            if c * QKV_COL_TILE + lo < n_norm_cols:
                head = _rmsnorm_rows(head, hgain_ref[:, cols])
            out_ref[:, cols] = head.astype(_BF16)


def _qkv_proj(h, gain, w_qkv, layer, head_gain_cols, tm):
    m, d = h.shape
    n_out = w_qkv.shape[2]
    return pl.pallas_call(
        functools.partial(_qkv_kernel, D_MODEL + KV_DIM),
        grid=(m // tm,),
        in_specs=[
            pl.BlockSpec((tm, d), lambda i: (i, 0)),
            pl.BlockSpec((1, d), lambda i: (0, 0)),
            pl.BlockSpec((None, d, n_out), lambda i: (layer, 0, 0), pipeline_mode=pl.Buffered(1)),
            pl.BlockSpec((1, n_out), lambda i: (0, 0)),
        ],
        out_specs=pl.BlockSpec((tm, n_out), lambda i: (i, 0)),
        out_shape=jax.ShapeDtypeStruct((m, n_out), _BF16),
        scratch_shapes=[pltpu.VMEM((tm, d), _BF16)],
        compiler_params=pltpu.CompilerParams(
            dimension_semantics=("arbitrary",), vmem_limit_bytes=V7X_VMEM_LIMIT_BYTES),
    )(h, gain.reshape(1, d), w_qkv, head_gain_cols)


def _bias_kernel(n_casts, band_codes_ref, tail_codes_ref, tbl_ref, *rest):
    out_ref = rest[n_casts]
    _run_casts(rest[:n_casts] + rest[n_casts + 1:])
    head = pl.program_id(0)

    def lookup(codes):
        acc = jnp.zeros(codes.shape, _F32)
        for code in range(N_BUCKETS + 2):
            acc = jnp.where(codes == code, tbl_ref[head, code] * LOG2_E, acc)
        return acc

    band = lookup(band_codes_ref[...])
    tails = [lookup(tail_codes_ref[v]) for v in range(3)]
    masked = jnp.full((BLOCK, BLOCK), NEG_INF, _F32)
    for variant, valid_blocks, tail in ((0, (2,), 0), (1, (1, 2), 1), (2, (0, 1, 2), 2), (3, (0, 1), 2)):
        for blk in range(3):
            cols = slice(blk * BLOCK, (blk + 1) * BLOCK)
            out_ref[variant, 0, :, cols] = band[:, cols] if blk in valid_blocks else masked
        out_ref[variant, 0, :, 3 * BLOCK:] = tails[tail]


def _attn_bias(band_codes, tail_codes, table, cast_weights):
    casts = [_CastPlan(w, l, N_HEADS, lambda h: h) for w, l in cast_weights]
    bias, *cast_out = pl.pallas_call(
        functools.partial(_bias_kernel, len(casts)),
        grid=(N_HEADS,),
        in_specs=[
            pl.BlockSpec((BLOCK, 3 * BLOCK), lambda h: (0, 0)),
            pl.BlockSpec((3, BLOCK, BLOCK), lambda h: (0, 0, 0)),
            pl.BlockSpec(memory_space=pltpu.SMEM),
            *[c.in_spec for c in casts],
        ],
        out_specs=[pl.BlockSpec((N_BIAS_VARIANTS, 1, BLOCK, KEYS), lambda h: (0, h, 0, 0)),
                   *[c.out_spec for c in casts]],
        out_shape=[jax.ShapeDtypeStruct((N_BIAS_VARIANTS, N_HEADS, BLOCK, KEYS), _F32),
                   *[c.out_shape for c in casts]],
    )(band_codes, tail_codes, table, *[c.operand for c in casts])
    return bias, cast_out


def _attn_chain(q_ref, k_refs, v_refs, bias_ref, out_ref, slab, k_slabs, g):
    kv_cols = slice(g * HEAD_DIM, (g + 1) * HEAD_DIM)
    dead = jnp.zeros((BLOCK - N_META, HEAD_DIM), _BF16)
    q = jnp.concatenate(
        [q_ref[slab, :, (g * GROUP + r) * HEAD_DIM:(g * GROUP + r + 1) * HEAD_DIM] for r in range(GROUP)],
        axis=0)
    k = jnp.concatenate([ref[ks, :, kv_cols] for ref, ks in zip(k_refs, k_slabs)] + [dead], axis=0)
    v = jnp.concatenate([ref[ks, :, kv_cols] for ref, ks in zip(v_refs, k_slabs)] + [dead], axis=0)
    s = lax.dot_general(q, k, (((1,), (1,)), ((), ())), preferred_element_type=_F32)
    s = s + bias_ref[0, g * GROUP:(g + 1) * GROUP].reshape(GROUP * BLOCK, KEYS)
    e = jnp.exp2(s - jnp.max(s, axis=-1, keepdims=True))
    denom = jnp.sum(e, axis=-1, keepdims=True)
    o = _dot(e.astype(_BF16), v) / denom
    for r in range(GROUP):
        out_ref[slab, :, (g * GROUP + r) * HEAD_DIM:(g * GROUP + r + 1) * HEAD_DIM] = (
            o[r * BLOCK:(r + 1) * BLOCK].astype(_BF16))


def _attn_kernel(n_casts, q_ref, kp_ref, kc_ref, kn_ref, km_ref, vp_ref, vc_ref, vn_ref, vm_ref, bias_ref,
                 *rest):
    out_ref = rest[n_casts]
    _run_casts(rest[:n_casts] + rest[n_casts + 1:])
    n = pl.program_id(0)
    n_slabs = q_ref.shape[0]
    k_refs = (kp_ref, kc_ref, kn_ref, km_ref)
    v_refs = (vp_ref, vc_ref, vn_ref, vm_ref)

    def all_chains(first_block):
        for slab in range(n_slabs):
            if first_block:
                k_slabs = (slab, slab, (slab + 1) % n_slabs, slab)
            else:
                k_slabs = (slab, slab, slab, (slab - 1) % n_slabs)
            for g in range(N_KV_HEADS):
                _attn_chain(q_ref, k_refs, v_refs, bias_ref, out_ref, slab, k_slabs, g)

    @pl.when(n == 0)
    def _():
        all_chains(True)
        out_ref[:, 0:LEAD, :] = jnp.zeros((n_slabs, LEAD, D_MODEL), _BF16)

    @pl.when(n > 0)
    def _():
        all_chains(False)


def _attention(qkv, bias, nb, cast_weights):
    bsz = qkv.shape[0]
    casts = [_CastPlan(w, l, nb, lambda n: n) for w, l in cast_weights]
    k_col, v_col = D_MODEL // KV_DIM, D_MODEL // KV_DIM + 1

    def slab_block(n):
        return jnp.where(n == 0, nb - 1, n - 1)

    def blk(col, shift):
        return pl.BlockSpec(
            (bsz, BLOCK, KV_DIM), lambda n: (0, slab_block(jnp.clip(n + shift, 0, nb - 1)), col))

    def meta(col):
        return pl.BlockSpec(
            (bsz, N_META, KV_DIM), lambda n: (0, (nb - 1) * (BLOCK // N_META) + LEAD // N_META, col))

    out, *cast_out = pl.pallas_call(
        functools.partial(_attn_kernel, len(casts)),
        grid=(nb,),
        in_specs=[
            pl.BlockSpec((bsz, BLOCK, D_MODEL), lambda n: (0, slab_block(n), 0)),
            blk(k_col, -1), blk(k_col, 0), blk(k_col, 1), meta(k_col),
            blk(v_col, -1), blk(v_col, 0), blk(v_col, 1), meta(v_col),
            pl.BlockSpec((1, N_HEADS, BLOCK, KEYS),
                         lambda n: (jnp.where(n == nb - 1, 3, jnp.minimum(n, 2)), 0, 0, 0)),
            *[c.in_spec for c in casts],
        ],
        out_specs=[pl.BlockSpec((bsz, BLOCK, D_MODEL), lambda n: (0, slab_block(n), 0)),
                   *[c.out_spec for c in casts]],
        out_shape=[jax.ShapeDtypeStruct((bsz, nb * BLOCK, D_MODEL), _BF16),
                   *[c.out_shape for c in casts]],
        compiler_params=pltpu.CompilerParams(
            dimension_semantics=("arbitrary",), vmem_limit_bytes=V7X_VMEM_LIMIT_BYTES),
    )(qkv, qkv, qkv, qkv, qkv, qkv, qkv, qkv, qkv, bias, *[c.operand for c in casts])
    return out, cast_out


def _out_proj_kernel(a_ref, w_ref, h_ref, out_ref):
    out_ref[...] = h_ref[...] + _dot(a_ref[...], w_ref[...])


def _out_proj(a, w, layer, h, tm):
    m, d = h.shape
    return pl.pallas_call(
        _out_proj_kernel,
        grid=(m // tm,),
        in_specs=[
            pl.BlockSpec((tm, d), lambda i: (i, 0)),
            pl.BlockSpec((None, d, d), lambda i: (layer, 0, 0), pipeline_mode=pl.Buffered(1)),
            pl.BlockSpec((tm, d), lambda i: (i, 0)),
        ],
        out_specs=pl.BlockSpec((tm, d), lambda i: (i, 0)),
        out_shape=jax.ShapeDtypeStruct((m, d), _F32),
        compiler_params=pltpu.CompilerParams(
            dimension_semantics=("arbitrary",), vmem_limit_bytes=V7X_VMEM_LIMIT_BYTES),
    )(a, w, h)


def _t5_bucket(rel):
    half = N_BUCKETS // 2
    max_exact = half // 2
    side = jnp.where(rel > 0, half, 0)
    n = jnp.abs(rel)
    nf = jnp.maximum(n, 1).astype(_F32)
    large = max_exact + (jnp.log(nf / max_exact) / math.log(MAX_DISTANCE / max_exact)
                         * (half - max_exact)).astype(jnp.int32)
    large = jnp.minimum(large, half - 1)
    return side + jnp.where(n < max_exact, n, large)


def _bias_codes():
    qi = jnp.arange(BLOCK, dtype=jnp.int32)[:, None]
    col = jnp.arange(3 * BLOCK, dtype=jnp.int32)[None, :]
    rel_band = col - BLOCK - qi
    band = jnp.where(jnp.abs(rel_band) <= WINDOW, _t5_bucket(rel_band), CODE_MASKED)
    tcol = jnp.arange(BLOCK, dtype=jnp.int32)[None, :]
    tails = []
    for n_rep in range(3):
        qpos = n_rep * BLOCK + qi
        codes = jnp.where(tcol < N_META, _t5_bucket(LEAD + tcol - qpos), CODE_MASKED)
        tails.append(jnp.where(tcol == N_META, CODE_SINK, codes))
    return band.astype(jnp.int32), jnp.stack(tails).astype(jnp.int32)


def kernel(x, meta_tokens, rel_bias_table, norm_mix, norm_ffn, conv_in_w, conv_dw, conv_out_w,
           attn_qkv, attn_q_gain, attn_k_gain, attn_sink, attn_o, ffn_up, ffn_dw, ffn_dw_b, ffn_down):
    bsz, seq, d = x.shape
    tp = seq + BLOCK
    nb = tp // BLOCK
    tm = tp // FFN_ROW_TILES_PER_SLAB
    assert tm * FFN_ROW_TILES_PER_SLAB == tp and d == D_MODEL and seq % BLOCK == 0 and nb >= 4

    table = jnp.concatenate([
        rel_bias_table.T.astype(_F32), jnp.full((N_HEADS, 1), NEG_INF, _F32),
        attn_sink[0].astype(_F32)[:, None]], axis=1)
    bias, (in_w, out_w) = _attn_bias(*_bias_codes(), table, [(conv_in_w, 0), (conv_out_w, 0)])

    h, (up0, down0, qkv_w, o_w) = _mixer_from_tokens(
        x, meta_tokens.astype(x.dtype), norm_mix[0], in_w, conv_dw, out_w, 0,
        tp // MIXER_ROW_TILES_PER_SLAB, [(ffn_up, 0), (ffn_down, 0), (attn_qkv, 0), (attn_o, 0)])
    h = _conv_ffn(h, norm_ffn[0], up0, ffn_dw, ffn_dw_b, down0, 0, 0, tm, tp)

    q_scale = HEAD_DIM ** -0.5 * LOG2_E
    head_gain_cols = jnp.concatenate([
        jnp.tile(attn_q_gain[0] * q_scale, N_HEADS), jnp.tile(attn_k_gain[0], N_KV_HEADS),
        jnp.ones((KV_DIM,), _F32)]).reshape(1, QKV_DIM)
    h2 = h.reshape(bsz * tp, d)
    qkv = _qkv_proj(h2, norm_mix[1], qkv_w, 0, head_gain_cols, tp // QKV_ROW_TILES_PER_SLAB)
    o, (up1, down1) = _attention(qkv.reshape(bsz, tp, QKV_DIM), bias, nb, [(ffn_up, 1), (ffn_down, 1)])
    h2 = _out_proj(o.reshape(bsz * tp, d), o_w, 0, h2, tp // OUT_PROJ_ROW_TILES_PER_SLAB)
    return _conv_ffn(h2.reshape(bsz, tp, d), norm_ffn[1], up1, ffn_dw, ffn_dw_b, down1, 0, 1, tm, seq)
```

```python
import functools
import math

import jax
import jax.numpy as jnp
from jax import lax
from jax.experimental import pallas as pl
from jax.experimental.pallas import tpu as pltpu

D_MODEL = 2048
N_META = 16
N_HEADS = 16
N_KV_HEADS = 4
HEAD_DIM = D_MODEL // N_HEADS
GROUP = N_HEADS // N_KV_HEADS
KV_DIM = N_KV_HEADS * HEAD_DIM
QKV_DIM = D_MODEL + 2 * KV_DIM
WINDOW = 128
BLOCK = 128
N_BUCKETS = 32
MAX_DISTANCE = 128
D_FF = 5632
EPS = 1e-6
LEAD = BLOCK - N_META

F32_SUBLANES = 8
BF16_SUBLANES = 16
HALO = BF16_SUBLANES
NORM_ROWS = 32
FFN_ROW_TILES_PER_SLAB = 4
MIXER_ROW_TILES_PER_SLAB = 4
QKV_ROW_TILES_PER_SLAB = 4
OUT_PROJ_ROW_TILES_PER_SLAB = 8
MIXER_COL_TILE = 512
FFN_COL_TILE = 512
QKV_COL_TILE = 512
KEYS = 4 * BLOCK
SINK_COL = 3 * BLOCK + N_META
CODE_MASKED = N_BUCKETS
CODE_SINK = N_BUCKETS + 1
N_BIAS_VARIANTS = 4
NEG_INF = -1e30
LOG2_E = math.log2(math.e)
V7X_VMEM_LIMIT_BYTES = 56 * 1024 * 1024

_BF16 = jnp.bfloat16
_F32 = jnp.float32


def _dot(a, b):
    return jnp.dot(a, b, preferred_element_type=_F32)


def _rmsnorm_rows(v, gain):
    return v * lax.rsqrt(jnp.mean(v * v, axis=-1, keepdims=True) + EPS) * gain


def _norm_rows_into(xext_ref, row0, src_ref, n_rows, gain):
    for r in range(0, n_rows, NORM_ROWS):
        rows = min(NORM_ROWS, n_rows - r)
        xext_ref[row0 + r:row0 + r + rows, :] = (
            _rmsnorm_rows(src_ref[r:r + rows, :], gain).astype(_BF16))


def _gelu_exact(v):
    return 0.5 * v * (1.0 + lax.erf(v * math.sqrt(0.5)))


def _row_tile_index(i, tiles):
    return lax.div(i, tiles), lax.rem(i, tiles)


def _tile_copy(op, hbm, acc_ref, sems, tm, tiles, tail_rows, to_hbm, t):
    b, q = _row_tile_index(t, tiles)
    slot = lax.rem(t, 2)

    def copy(rows):
        hbm_rows = hbm.at[b, pl.ds(q * tm, rows), :]
        acc_rows = acc_ref.at[slot, pl.ds(0, rows), :]
        src, dst = (acc_rows, hbm_rows) if to_hbm else (hbm_rows, acc_rows)
        op(pltpu.make_async_copy(src, dst, sems.at[slot]))

    if tail_rows == tm:
        copy(tm)
    else:
        pl.when(q < tiles - 1)(lambda: copy(tm))
        pl.when(q == tiles - 1)(lambda: copy(tail_rows))


def _stream_row_tiles(load, store, body):
    i, j = pl.program_id(0), pl.program_id(1)
    n_i, n_j = pl.num_programs(0), pl.num_programs(1)
    start, wait = (lambda c: c.start()), (lambda c: c.wait())

    pl.when(jnp.logical_and(i == 0, j == 0))(lambda: load(start, t=i))
    pl.when(j == 0)(lambda: load(wait, t=i))
    pl.when(jnp.logical_and(j == 1, i >= 1))(lambda: store(wait, t=i - 1))
    pl.when(jnp.logical_and(j == 1, i + 1 < n_i))(lambda: load(start, t=i + 1))
    body()
    pl.when(j == n_j - 1)(lambda: store(start, t=i))
    pl.when(jnp.logical_and(j == n_j - 1, i == n_i - 1))(lambda: store(wait, t=i))


def _fill_from_slab(hprev_ref, acc, hnext_ref, gain_ref, xext_ref, tm):
    gain = gain_ref[...]
    xext_ref[0:HALO, :] = _rmsnorm_rows(hprev_ref[0], gain).astype(_BF16)
    _norm_rows_into(xext_ref, HALO, acc, tm, gain)
    xext_ref[HALO + tm:, :] = _rmsnorm_rows(hnext_ref[0], gain).astype(_BF16)


def _fill_compact_from_slab(hprev_ref, acc, hnext_ref, gain_ref, xext_ref, tm, tok_rows):
    gain = gain_ref[...]
    row = HALO + tok_rows
    xext_ref[0:HALO, :] = _rmsnorm_rows(hprev_ref[0], gain).astype(_BF16)
    _norm_rows_into(xext_ref, HALO, acc, tok_rows, gain)
    xext_ref[row:row + HALO, :] = jnp.zeros((HALO, D_MODEL), _BF16)
    xext_ref[row + HALO:row + HALO + N_META, :] = (
        _rmsnorm_rows(acc[tm - N_META:tm, :], gain).astype(_BF16))
    xext_ref[row + HALO + N_META:row + 2 * HALO + N_META, :] = (
        _rmsnorm_rows(hnext_ref[0], gain).astype(_BF16))


def _fill_from_tokens(xprev_ref, acc, xnext_ref, meta_ref, gain_ref, xext_ref, tm, tiles, tail_tokens):
    q = lax.rem(pl.program_id(0), tiles)
    gain = gain_ref[...]
    meta = meta_ref[...]
    meta_n = _rmsnorm_rows(meta, gain).astype(_BF16)

    @pl.when(q == 0)
    def _():
        xext_ref[0:HALO, :] = meta_n

    @pl.when(q > 0)
    def _():
        xext_ref[0:HALO, :] = _rmsnorm_rows(xprev_ref[0], gain).astype(_BF16)

    @pl.when(q < tiles - 1)
    def _():
        _norm_rows_into(xext_ref, HALO, acc, tm, gain)

    @pl.when(q == tiles - 1)
    def _():
        _norm_rows_into(xext_ref, HALO, acc, tail_tokens, gain)
        xext_ref[HALO + tail_tokens:HALO + tail_tokens + LEAD, :] = jnp.zeros((LEAD, D_MODEL), _BF16)
        xext_ref[HALO + tail_tokens + LEAD:HALO + tm, :] = meta_n
        acc[tail_tokens:tail_tokens + LEAD, :] = jnp.zeros((LEAD, D_MODEL), _F32)
        acc[tail_tokens + LEAD:tm, :] = meta

    xext_ref[HALO + tm:, :] = _rmsnorm_rows(xnext_ref[0], gain).astype(_BF16)


class _CastPlan:
    def __init__(self, w, layer, n_steps, step_of_grid):
        _, rows, cols = w.shape
        n_blocks = next(nb for nb in range(min(n_steps, rows // BF16_SUBLANES), 0, -1)
                        if rows % nb == 0 and (rows // nb) % BF16_SUBLANES == 0)
        block_rows = rows // n_blocks

        def block(*grid_idx):
            return lax.div(step_of_grid(*grid_idx) * n_blocks, n_steps)

        self.operand = w
        self.in_spec = pl.BlockSpec((None, block_rows, cols), lambda *g: (layer, block(*g), 0))
        self.out_spec = pl.BlockSpec((None, block_rows, cols), lambda *g: (0, block(*g), 0))
        self.out_shape = jax.ShapeDtypeStruct((1, rows, cols), _BF16)


def _run_casts(refs):
    n = len(refs) // 2
    for src, dst in zip(refs[:n], refs[n:]):
        dst[...] = src[...].astype(_BF16)


def _conv3(v_ext, w, tm):
    return (v_ext[HALO - 1:HALO - 1 + tm] * w[0:1]
            + v_ext[HALO:HALO + tm] * w[1:2]
            + v_ext[HALO + 1:HALO + 1 + tm] * w[2:3])


def _mixer_kernel(tm, tiles, tail_tokens, n_casts, xprev_ref, x_hbm, xnext_ref, meta_ref, gain_ref, wb_ref,
                  wc_ref, wh_ref, cw_ref, wo_ref, *rest):
    cast_srcs, out_hbm, cast_dsts = rest[:n_casts], rest[n_casts], rest[n_casts + 1:2 * n_casts + 1]
    xext_ref, acc_ref, in_sems, out_sems = rest[2 * n_casts + 1:]
    acc = acc_ref.at[lax.rem(pl.program_id(0), 2)]
    _run_casts(cast_srcs + cast_dsts)

    def body():
        @pl.when(pl.program_id(1) == 0)
        def _():
            _fill_from_tokens(xprev_ref, acc, xnext_ref, meta_ref, gain_ref, xext_ref, tm, tiles,
                              tail_tokens)

        x = xext_ref[...]
        ch = _conv3(_dot(x, wc_ref[...]) * _dot(x, wh_ref[...]), cw_ref[...], tm)
        y = _dot(xext_ref[HALO:HALO + tm, :], wb_ref[...]) * ch
        acc[...] += _dot(y.astype(_BF16), wo_ref[...])

    _stream_row_tiles(
        functools.partial(_tile_copy, hbm=x_hbm, acc_ref=acc_ref, sems=in_sems, tm=tm, tiles=tiles,
                          tail_rows=tail_tokens, to_hbm=False),
        functools.partial(_tile_copy, hbm=out_hbm, acc_ref=acc_ref, sems=out_sems, tm=tm, tiles=tiles,
                          tail_rows=tm, to_hbm=True),
        body)


def _ffn_kernel(tm, tiles, out_tail_rows, n_chunks, tok_rows, hprev_ref, h_hbm, hnext_ref, gain_ref,
                wg_ref, wua_ref, wub_ref, cw_ref, cb_ref, wd_ref, out_hbm, xext_ref, acc_ref, in_sems,
                out_sems):
    c = FFN_COL_TILE
    i, j = pl.program_id(0), pl.program_id(1)
    acc = acc_ref.at[lax.rem(i, 2)]
    last_tile = lax.rem(i, tiles) == tiles - 1
    n_pairs, odd = n_chunks // 2, n_chunks % 2
    compact_rows = tok_rows + HALO + N_META

    def add_all(r):
        acc[...] += r

    def add_compact(r):
        acc[0:tok_rows, :] += r[0:tok_rows]
        acc[tm - N_META:tm, :] += r[tok_rows + HALO:compact_rows]

    def steps(m, add):
        def act(g_ext, u, lo):
            g = _conv3(g_ext, cw_ref[:, lo:lo + c], m) + cb_ref[:, lo:lo + c]
            return (_gelu_exact(g) * u).astype(_BF16)

        def pair_step():
            x, xc = xext_ref[0:m + 2 * HALO, :], xext_ref[HALO:HALO + m, :]
            g2 = _dot(x, wg_ref[...])
            ua, ub = _dot(xc, wua_ref[...]), _dot(xc, wub_ref[...])
            a = jnp.concatenate([act(g2[:, :c], ua, 0), act(g2[:, c:], ub, c)], axis=1)
            add(_dot(a, wd_ref[...]))

        def single_step():
            g = _dot(xext_ref[0:m + 2 * HALO, :], wg_ref[:, 0:c])
            ua = _dot(xext_ref[HALO:HALO + m, :], wua_ref[...])
            add(_dot(act(g, ua, 0), wd_ref[0:c, :]))

        return pair_step, single_step

    def body():
        @pl.when(jnp.logical_and(j == 0, jnp.logical_not(last_tile)))
        def _():
            _fill_from_slab(hprev_ref, acc, hnext_ref, gain_ref, xext_ref, tm)

        @pl.when(jnp.logical_and(j == 0, last_tile))
        def _():
            _fill_compact_from_slab(hprev_ref, acc, hnext_ref, gain_ref, xext_ref, tm, tok_rows)

        for this_tile, (pair_step, single_step) in (
                (jnp.logical_not(last_tile), steps(tm, add_all)),
                (last_tile, steps(compact_rows, add_compact))):
            pl.when(jnp.logical_and(this_tile, j < n_pairs))(pair_step)
            if odd:
                pl.when(jnp.logical_and(this_tile, j == n_pairs))(single_step)

    _stream_row_tiles(
        functools.partial(_tile_copy, hbm=h_hbm, acc_ref=acc_ref, sems=in_sems, tm=tm, tiles=tiles,
                          tail_rows=tm, to_hbm=False),
        functools.partial(_tile_copy, hbm=out_hbm, acc_ref=acc_ref, sems=out_sems, tm=tm, tiles=tiles,
                          tail_rows=out_tail_rows, to_hbm=True),
        body)


def _fused_params():
    return pltpu.CompilerParams(dimension_semantics=("arbitrary", "arbitrary"),
                                vmem_limit_bytes=V7X_VMEM_LIMIT_BYTES)


def _fused_scratch(tm, d):
    return [pltpu.VMEM((tm + 2 * HALO, d), _BF16), pltpu.VMEM((2, tm, d), _F32),
            pltpu.SemaphoreType.DMA((2,)), pltpu.SemaphoreType.DMA((2,))]


def _weight_specs(layer, d, hidden, n_up, col_tile):
    n_j = hidden // col_tile
    ups = [pl.BlockSpec((None, d, col_tile),
                        functools.partial(lambda i, j, k: (layer, 0, j + k * n_j), k=k))
           for k in range(n_up)]
    down = pl.BlockSpec((None, col_tile, d), lambda i, j: (layer, j, 0))
    return ups, down


def _mixer_from_tokens(x, meta_tokens, gain, w_in, conv_w, w_out, layer, tm, cast_weights):
    bsz, seq, d = x.shape
    tp = seq + BLOCK
    tiles = tp // tm
    tail_tokens = seq - (tiles - 1) * tm
    assert tiles * tm == tp and 0 < tail_tokens < tm and tail_tokens % HALO == 0 and tm % HALO == 0
    halo_blocks_per_tile = tm // HALO
    col_tile = MIXER_COL_TILE

    def prev_map(i, j):
        b, q = _row_tile_index(i, tiles)
        return b, jnp.maximum(q * halo_blocks_per_tile - 1, 0), 0

    def next_map(i, j):
        b, q = _row_tile_index(i, tiles)
        wrap = q == tiles - 1
        return (jnp.where(wrap, lax.rem(b + 1, bsz), b),
                jnp.where(wrap, 0, (q + 1) * halo_blocks_per_tile), 0)

    ups, down = _weight_specs(layer, d, d, 3, col_tile)
    n_j = d // col_tile
    casts = [_CastPlan(w, l, bsz * tiles * n_j, lambda i, j: i * n_j + j) for w, l in cast_weights]
    in_specs = [
        pl.BlockSpec((1, HALO, d), prev_map),
        pl.BlockSpec(memory_space=pl.ANY),
        pl.BlockSpec((1, HALO, d), next_map),
        pl.BlockSpec((N_META, d), lambda i, j: (0, 0)),
        pl.BlockSpec((1, d), lambda i, j: (0, 0)),
        *ups,
        pl.BlockSpec((None, 3, col_tile), lambda i, j: (layer, 0, j)),
        down,
        *[c.in_spec for c in casts],
    ]
    out, *cast_out = pl.pallas_call(
        functools.partial(_mixer_kernel, tm, tiles, tail_tokens, len(casts)),
        grid=(bsz * tiles, n_j),
        in_specs=in_specs,
        out_specs=[pl.BlockSpec(memory_space=pl.ANY), *[c.out_spec for c in casts]],
        out_shape=[jax.ShapeDtypeStruct((bsz, tp, d), _F32), *[c.out_shape for c in casts]],
        scratch_shapes=_fused_scratch(tm, d),
        compiler_params=_fused_params(),
    )(x, x, x, meta_tokens, gain.reshape(1, d), w_in, w_in, w_in, conv_w, w_out,
      *[c.operand for c in casts])
    return out, cast_out


def _conv_ffn(h, gain, w_up, conv_w, conv_b, w_down, layer, conv_layer, tm, out_rows):
    bsz, tp, d = h.shape
    hidden = w_down.shape[1]
    tiles = tp // tm
    out_tail_rows = out_rows - (tiles - 1) * tm
    tok_rows = tm - BLOCK
    assert 0 < tok_rows and tok_rows % NORM_ROWS == 0 and LEAD >= HALO
    assert tiles * tm == tp and tm % HALO == 0 and 0 < out_tail_rows <= tm and out_tail_rows % F32_SUBLANES == 0
    halo_blocks_per_tile = tm // HALO
    halo_blocks_per_slab = tp // HALO
    n_halo_blocks = bsz * halo_blocks_per_slab
    col_tile = FFN_COL_TILE

    def halo_map(i, j, offset):
        blk = lax.rem(i * halo_blocks_per_tile + offset + n_halo_blocks, n_halo_blocks)
        return lax.div(blk, halo_blocks_per_slab), lax.rem(blk, halo_blocks_per_slab), 0

    n_chunks = hidden // col_tile
    assert n_chunks * col_tile == hidden and w_up.shape[2] == 2 * hidden
    in_specs = [
        pl.BlockSpec((1, HALO, d), functools.partial(halo_map, offset=-1)),
        pl.BlockSpec(memory_space=pl.ANY),
        pl.BlockSpec((1, HALO, d), functools.partial(halo_map, offset=halo_blocks_per_tile)),
        pl.BlockSpec((1, d), lambda i, j: (0, 0)),
        pl.BlockSpec((None, d, 2 * col_tile), lambda i, j: (layer, 0, j)),
        pl.BlockSpec((None, d, col_tile), lambda i, j: (layer, 0, n_chunks + 2 * j)),
        pl.BlockSpec((None, d, col_tile),
                     lambda i, j: (layer, 0, jnp.minimum(n_chunks + 2 * j + 1, 2 * n_chunks - 1))),
        pl.BlockSpec((None, 3, 2 * col_tile), lambda i, j: (conv_layer, 0, j)),
        pl.BlockSpec((None, 1, 2 * col_tile), lambda i, j: (conv_layer, 0, j)),
        pl.BlockSpec((None, 2 * col_tile, d), lambda i, j: (layer, j, 0)),
    ]
    return pl.pallas_call(
        functools.partial(_ffn_kernel, tm, tiles, out_tail_rows, n_chunks, tok_rows),
        grid=(bsz * tiles, (n_chunks + 1) // 2),
        in_specs=in_specs,
        out_specs=pl.BlockSpec(memory_space=pl.ANY),
        out_shape=jax.ShapeDtypeStruct((bsz, out_rows, d), _F32),
        scratch_shapes=_fused_scratch(tm, d),
        compiler_params=_fused_params(),
    )(h, h, h, gain.reshape(1, d), w_up, w_up, w_up, conv_w, conv_b.reshape(conv_b.shape[0], 1, hidden),
      w_down)


def _qkv_kernel(n_norm_cols, h_ref, gain_ref, w_ref, hgain_ref, out_ref, xn_ref):
    _norm_rows_into(xn_ref, 0, h_ref, h_ref.shape[0], gain_ref[...])
    for c in range(w_ref.shape[1] // QKV_COL_TILE):
        acc = _dot(xn_ref[...], w_ref[:, c * QKV_COL_TILE:(c + 1) * QKV_COL_TILE])
        for hd in range(QKV_COL_TILE // HEAD_DIM):
            lo = hd * HEAD_DIM
            cols = slice(c * QKV_COL_TILE + lo, c * QKV_COL_TILE + lo + HEAD_DIM)
            head = acc[:, lo:lo + HEAD_DIM]
            if c * QKV_COL_TILE + lo < n_norm_cols:
                head = _rmsnorm_rows(head, hgain_ref[:, cols])
            out_ref[:, cols] = head.astype(_BF16)


def _qkv_proj(h, gain, w_qkv, layer, head_gain_cols, tm):
    m, d = h.shape
    n_out = w_qkv.shape[2]
    return pl.pallas_call(
        functools.partial(_qkv_kernel, D_MODEL + KV_DIM),
        grid=(m // tm,),
        in_specs=[
            pl.BlockSpec((tm, d), lambda i: (i, 0)),
            pl.BlockSpec((1, d), lambda i: (0, 0)),
            pl.BlockSpec((None, d, n_out), lambda i: (layer, 0, 0), pipeline_mode=pl.Buffered(1)),
            pl.BlockSpec((1, n_out), lambda i: (0, 0)),
        ],
        out_specs=pl.BlockSpec((tm, n_out), lambda i: (i, 0)),
        out_shape=jax.ShapeDtypeStruct((m, n_out), _BF16),
        scratch_shapes=[pltpu.VMEM((tm, d), _BF16)],
        compiler_params=pltpu.CompilerParams(
            dimension_semantics=("arbitrary",), vmem_limit_bytes=V7X_VMEM_LIMIT_BYTES),
    )(h, gain.reshape(1, d), w_qkv, head_gain_cols)


def _bias_kernel(n_casts, band_codes_ref, tail_codes_ref, tbl_ref, *rest):
    out_ref = rest[n_casts]
    _run_casts(rest[:n_casts] + rest[n_casts + 1:])
    head = pl.program_id(0)

    def lookup(codes):
        acc = jnp.zeros(codes.shape, _F32)
        for code in range(N_BUCKETS + 2):
            acc = jnp.where(codes == code, tbl_ref[head, code] * LOG2_E, acc)
        return acc

    band = lookup(band_codes_ref[...])
    tails = [lookup(tail_codes_ref[v]) for v in range(3)]
    masked = jnp.full((BLOCK, BLOCK), NEG_INF, _F32)
    for variant, valid_blocks, tail in ((0, (2,), 0), (1, (1, 2), 1), (2, (0, 1, 2), 2), (3, (0, 1), 2)):
        for blk in range(3):
            cols = slice(blk * BLOCK, (blk + 1) * BLOCK)
            out_ref[variant, 0, :, cols] = band[:, cols] if blk in valid_blocks else masked
        out_ref[variant, 0, :, 3 * BLOCK:] = tails[tail]


def _attn_bias(band_codes, tail_codes, table, cast_weights):
    casts = [_CastPlan(w, l, N_HEADS, lambda h: h) for w, l in cast_weights]
    bias, *cast_out = pl.pallas_call(
        functools.partial(_bias_kernel, len(casts)),
        grid=(N_HEADS,),
        in_specs=[
            pl.BlockSpec((BLOCK, 3 * BLOCK), lambda h: (0, 0)),
            pl.BlockSpec((3, BLOCK, BLOCK), lambda h: (0, 0, 0)),
            pl.BlockSpec(memory_space=pltpu.SMEM),
            *[c.in_spec for c in casts],
        ],
        out_specs=[pl.BlockSpec((N_BIAS_VARIANTS, 1, BLOCK, KEYS), lambda h: (0, h, 0, 0)),
                   *[c.out_spec for c in casts]],
        out_shape=[jax.ShapeDtypeStruct((N_BIAS_VARIANTS, N_HEADS, BLOCK, KEYS), _F32),
                   *[c.out_shape for c in casts]],
    )(band_codes, tail_codes, table, *[c.operand for c in casts])
    return bias, cast_out


def _attn_chain(q_ref, k_refs, v_refs, bias_ref, out_ref, slab, k_slabs, g):
    kv_cols = slice(g * HEAD_DIM, (g + 1) * HEAD_DIM)
    dead = jnp.zeros((BLOCK - N_META, HEAD_DIM), _BF16)
    q = jnp.concatenate(
        [q_ref[slab, :, (g * GROUP + r) * HEAD_DIM:(g * GROUP + r + 1) * HEAD_DIM] for r in range(GROUP)],
        axis=0)
    k = jnp.concatenate([ref[ks, :, kv_cols] for ref, ks in zip(k_refs, k_slabs)] + [dead], axis=0)
    v = jnp.concatenate([ref[ks, :, kv_cols] for ref, ks in zip(v_refs, k_slabs)] + [dead], axis=0)
    s = lax.dot_general(q, k, (((1,), (1,)), ((), ())), preferred_element_type=_F32)
    s = s + bias_ref[0, g * GROUP:(g + 1) * GROUP].reshape(GROUP * BLOCK, KEYS)
    e = jnp.exp2(s - jnp.max(s, axis=-1, keepdims=True))
    denom = jnp.sum(e, axis=-1, keepdims=True)
    o = _dot(e.astype(_BF16), v) / denom
    for r in range(GROUP):
        out_ref[slab, :, (g * GROUP + r) * HEAD_DIM:(g * GROUP + r + 1) * HEAD_DIM] = (
            o[r * BLOCK:(r + 1) * BLOCK].astype(_BF16))


def _attn_kernel(n_casts, q_ref, kp_ref, kc_ref, kn_ref, km_ref, vp_ref, vc_ref, vn_ref, vm_ref, bias_ref,
                 *rest):
    out_ref = rest[n_casts]
    _run_casts(rest[:n_casts] + rest[n_casts + 1:])
    n = pl.program_id(0)
    n_slabs = q_ref.shape[0]
    k_refs = (kp_ref, kc_ref, kn_ref, km_ref)
    v_refs = (vp_ref, vc_ref, vn_ref, vm_ref)

    def all_chains(first_block):
        for slab in range(n_slabs):
            if first_block:
                k_slabs = (slab, slab, (slab + 1) % n_slabs, slab)
            else:
                k_slabs = (slab, slab, slab, (slab - 1) % n_slabs)
            for g in range(N_KV_HEADS):
                _attn_chain(q_ref, k_refs, v_refs, bias_ref, out_ref, slab, k_slabs, g)

    @pl.when(n == 0)
    def _():
        all_chains(True)
        out_ref[:, 0:LEAD, :] = jnp.zeros((n_slabs, LEAD, D_MODEL), _BF16)

    @pl.when(n > 0)
    def _():
        all_chains(False)


def _attention(qkv, bias, nb, cast_weights):
    bsz = qkv.shape[0]
    casts = [_CastPlan(w, l, nb, lambda n: n) for w, l in cast_weights]
    k_col, v_col = D_MODEL // KV_DIM, D_MODEL // KV_DIM + 1

    def slab_block(n):
        return jnp.where(n == 0, nb - 1, n - 1)

    def blk(col, shift):
        return pl.BlockSpec(
            (bsz, BLOCK, KV_DIM), lambda n: (0, slab_block(jnp.clip(n + shift, 0, nb - 1)), col))

    def meta(col):
        return pl.BlockSpec(
            (bsz, N_META, KV_DIM), lambda n: (0, (nb - 1) * (BLOCK // N_META) + LEAD // N_META, col))

    out, *cast_out = pl.pallas_call(
        functools.partial(_attn_kernel, len(casts)),
        grid=(nb,),
        in_specs=[
            pl.BlockSpec((bsz, BLOCK, D_MODEL), lambda n: (0, slab_block(n), 0)),
            blk(k_col, -1), blk(k_col, 0), blk(k_col, 1), meta(k_col),
            blk(v_col, -1), blk(v_col, 0), blk(v_col, 1), meta(v_col),
            pl.BlockSpec((1, N_HEADS, BLOCK, KEYS),
                         lambda n: (jnp.where(n == nb - 1, 3, jnp.minimum(n, 2)), 0, 0, 0)),
            *[c.in_spec for c in casts],
        ],
        out_specs=[pl.BlockSpec((bsz, BLOCK, D_MODEL), lambda n: (0, slab_block(n), 0)),
                   *[c.out_spec for c in casts]],
        out_shape=[jax.ShapeDtypeStruct((bsz, nb * BLOCK, D_MODEL), _BF16),
                   *[c.out_shape for c in casts]],
        compiler_params=pltpu.CompilerParams(
            dimension_semantics=("arbitrary",), vmem_limit_bytes=V7X_VMEM_LIMIT_BYTES),
    )(qkv, qkv, qkv, qkv, qkv, qkv, qkv, qkv, qkv, bias, *[c.operand for c in casts])
    return out, cast_out


def _out_proj_kernel(a_ref, w_ref, h_ref, out_ref):
    out_ref[...] = h_ref[...] + _dot(a_ref[...], w_ref[...])


def _out_proj(a, w, layer, h, tm):
    m, d = h.shape
    return pl.pallas_call(
        _out_proj_kernel,
        grid=(m // tm,),
        in_specs=[
            pl.BlockSpec((tm, d), lambda i: (i, 0)),
            pl.BlockSpec((None, d, d), lambda i: (layer, 0, 0), pipeline_mode=pl.Buffered(1)),
            pl.BlockSpec((tm, d), lambda i: (i, 0)),
        ],
        out_specs=pl.BlockSpec((tm, d), lambda i: (i, 0)),
        out_shape=jax.ShapeDtypeStruct((m, d), _F32),
        compiler_params=pltpu.CompilerParams(
            dimension_semantics=("arbitrary",), vmem_limit_bytes=V7X_VMEM_LIMIT_BYTES),
    )(a, w, h)


def _t5_bucket(rel):
    half = N_BUCKETS // 2
    max_exact = half // 2
    side = jnp.where(rel > 0, half, 0)
    n = jnp.abs(rel)
    nf = jnp.maximum(n, 1).astype(_F32)
    large = max_exact + (jnp.log(nf / max_exact) / math.log(MAX_DISTANCE / max_exact)
                         * (half - max_exact)).astype(jnp.int32)
    large = jnp.minimum(large, half - 1)
    return side + jnp.where(n < max_exact, n, large)


def _bias_codes():
    qi = jnp.arange(BLOCK, dtype=jnp.int32)[:, None]
    col = jnp.arange(3 * BLOCK, dtype=jnp.int32)[None, :]
    rel_band = col - BLOCK - qi
    band = jnp.where(jnp.abs(rel_band) <= WINDOW, _t5_bucket(rel_band), CODE_MASKED)
    tcol = jnp.arange(BLOCK, dtype=jnp.int32)[None, :]
    tails = []
    for n_rep in range(3):
        qpos = n_rep * BLOCK + qi
        codes = jnp.where(tcol < N_META, _t5_bucket(LEAD + tcol - qpos), CODE_MASKED)
        tails.append(jnp.where(tcol == N_META, CODE_SINK, codes))
    return band.astype(jnp.int32), jnp.stack(tails).astype(jnp.int32)


def kernel(x, meta_tokens, rel_bias_table, norm_mix, norm_ffn, conv_in_w, conv_dw, conv_out_w,
           attn_qkv, attn_q_gain, attn_k_gain, attn_sink, attn_o, ffn_up, ffn_dw, ffn_dw_b, ffn_down):
    bsz, seq, d = x.shape
    tp = seq + BLOCK
    nb = tp // BLOCK
    tm = tp // FFN_ROW_TILES_PER_SLAB
    assert tm * FFN_ROW_TILES_PER_SLAB == tp and d == D_MODEL and seq % BLOCK == 0 and nb >= 4

    table = jnp.concatenate([
        rel_bias_table.T.astype(_F32), jnp.full((N_HEADS, 1), NEG_INF, _F32),
        attn_sink[0].astype(_F32)[:, None]], axis=1)
    bias, (in_w, out_w) = _attn_bias(*_bias_codes(), table, [(conv_in_w, 0), (conv_out_w, 0)])

    h, (up0, down0, qkv_w, o_w) = _mixer_from_tokens(
        x, meta_tokens.astype(x.dtype), norm_mix[0], in_w, conv_dw, out_w, 0,
        tp // MIXER_ROW_TILES_PER_SLAB, [(ffn_up, 0), (ffn_down, 0), (attn_qkv, 0), (attn_o, 0)])
    h = _conv_ffn(h, norm_ffn[0], up0, ffn_dw, ffn_dw_b, down0, 0, 0, tm, tp)

    q_scale = HEAD_DIM ** -0.5 * LOG2_E
    head_gain_cols = jnp.concatenate([
        jnp.tile(attn_q_gain[0] * q_scale, N_HEADS), jnp.tile(attn_k_gain[0], N_KV_HEADS),
        jnp.ones((KV_DIM,), _F32)]).reshape(1, QKV_DIM)
    h2 = h.reshape(bsz * tp, d)
    qkv = _qkv_proj(h2, norm_mix[1], qkv_w, 0, head_gain_cols, tp // QKV_ROW_TILES_PER_SLAB)
    o, (up1, down1) = _attention(qkv.reshape(bsz, tp, QKV_DIM), bias, nb, [(ffn_up, 1), (ffn_down, 1)])
    h2 = _out_proj(o.reshape(bsz * tp, d), o_w, 0, h2, tp // OUT_PROJ_ROW_TILES_PER_SLAB)
    return _conv_ffn(h2.reshape(bsz, tp, d), norm_ffn[1], up1, ffn_dw, ffn_dw_b, down1, 0, 1, tm, seq)
```

```python
import functools
import math

import jax
import jax.numpy as jnp
from jax import lax
from jax.experimental import pallas as pl
from jax.experimental.pallas import tpu as pltpu

D_MODEL = 2048
N_META = 16
N_HEADS = 16
N_KV_HEADS = 4
HEAD_DIM = D_MODEL // N_HEADS
GROUP = N_HEADS // N_KV_HEADS
KV_DIM = N_KV_HEADS * HEAD_DIM
QKV_DIM = D_MODEL + 2 * KV_DIM
WINDOW = 128
BLOCK = 128
N_BUCKETS = 32
MAX_DISTANCE = 128
D_FF = 5632
EPS = 1e-6
LEAD = BLOCK - N_META

F32_SUBLANES = 8
BF16_SUBLANES = 16
HALO = BF16_SUBLANES
NORM_ROWS = 32
FFN_ROW_TILES_PER_SLAB = 4
MIXER_ROW_TILES_PER_SLAB = 4
QKV_ROW_TILES_PER_SLAB = 4
OUT_PROJ_ROW_TILES_PER_SLAB = 8
MIXER_COL_TILE = 512
FFN_COL_TILE = 512
QKV_COL_TILE = 512
KEYS = 4 * BLOCK
SINK_COL = 3 * BLOCK + N_META
CODE_MASKED = N_BUCKETS
CODE_SINK = N_BUCKETS + 1
N_BIAS_VARIANTS = 4
NEG_INF = -1e30
LOG2_E = math.log2(math.e)
V7X_VMEM_LIMIT_BYTES = 56 * 1024 * 1024

_BF16 = jnp.bfloat16
_F32 = jnp.float32


def _dot(a, b):
    return jnp.dot(a, b, preferred_element_type=_F32)


def _rmsnorm_rows(v, gain):
    return v * lax.rsqrt(jnp.mean(v * v, axis=-1, keepdims=True) + EPS) * gain


def _norm_rows_into(xext_ref, row0, src_ref, n_rows, gain):
    for r in range(0, n_rows, NORM_ROWS):
        rows = min(NORM_ROWS, n_rows - r)
        xext_ref[row0 + r:row0 + r + rows, :] = (
            _rmsnorm_rows(src_ref[r:r + rows, :], gain).astype(_BF16))


def _gelu_exact(v):
    return 0.5 * v * (1.0 + lax.erf(v * math.sqrt(0.5)))


def _row_tile_index(i, tiles):
    return lax.div(i, tiles), lax.rem(i, tiles)


def _tile_copy(op, hbm, acc_ref, sems, tm, tiles, tail_rows, to_hbm, t):
    b, q = _row_tile_index(t, tiles)
    slot = lax.rem(t, 2)

    def copy(rows):
        hbm_rows = hbm.at[b, pl.ds(q * tm, rows), :]
        acc_rows = acc_ref.at[slot, pl.ds(0, rows), :]
        src, dst = (acc_rows, hbm_rows) if to_hbm else (hbm_rows, acc_rows)
        op(pltpu.make_async_copy(src, dst, sems.at[slot]))

    if tail_rows == tm:
        copy(tm)
    else:
        pl.when(q < tiles - 1)(lambda: copy(tm))
        pl.when(q == tiles - 1)(lambda: copy(tail_rows))


def _stream_row_tiles(load, store, body):
    i, j = pl.program_id(0), pl.program_id(1)
    n_i, n_j = pl.num_programs(0), pl.num_programs(1)
    start, wait = (lambda c: c.start()), (lambda c: c.wait())

    pl.when(jnp.logical_and(i == 0, j == 0))(lambda: load(start, t=i))
    pl.when(j == 0)(lambda: load(wait, t=i))
    pl.when(jnp.logical_and(j == 1, i >= 1))(lambda: store(wait, t=i - 1))
    pl.when(jnp.logical_and(j == 1, i + 1 < n_i))(lambda: load(start, t=i + 1))
    body()
    pl.when(j == n_j - 1)(lambda: store(start, t=i))
    pl.when(jnp.logical_and(j == n_j - 1, i == n_i - 1))(lambda: store(wait, t=i))


def _fill_from_slab(hprev_ref, acc, hnext_ref, gain_ref, xext_ref, tm):
    gain = gain_ref[...]
    xext_ref[0:HALO, :] = _rmsnorm_rows(hprev_ref[0], gain).astype(_BF16)
    _norm_rows_into(xext_ref, HALO, acc, tm, gain)
    xext_ref[HALO + tm:, :] = _rmsnorm_rows(hnext_ref[0], gain).astype(_BF16)


def _fill_compact_from_slab(hprev_ref, acc, hnext_ref, gain_ref, xext_ref, tm, tok_rows):
    gain = gain_ref[...]
    row = HALO + tok_rows
    xext_ref[0:HALO, :] = _rmsnorm_rows(hprev_ref[0], gain).astype(_BF16)
    _norm_rows_into(xext_ref, HALO, acc, tok_rows, gain)
    xext_ref[row:row + HALO, :] = jnp.zeros((HALO, D_MODEL), _BF16)
    xext_ref[row + HALO:row + HALO + N_META, :] = (
        _rmsnorm_rows(acc[tm - N_META:tm, :], gain).astype(_BF16))
    xext_ref[row + HALO + N_META:row + 2 * HALO + N_META, :] = (
        _rmsnorm_rows(hnext_ref[0], gain).astype(_BF16))


def _fill_from_tokens(xprev_ref, acc, xnext_ref, meta_ref, gain_ref, xext_ref, tm, tiles, tail_tokens):
    q = lax.rem(pl.program_id(0), tiles)
    gain = gain_ref[...]
    meta = meta_ref[...]
    meta_n = _rmsnorm_rows(meta, gain).astype(_BF16)

    @pl.when(q == 0)
    def _():
        xext_ref[0:HALO, :] = meta_n

    @pl.when(q > 0)
    def _():
        xext_ref[0:HALO, :] = _rmsnorm_rows(xprev_ref[0], gain).astype(_BF16)

    @pl.when(q < tiles - 1)
    def _():
        _norm_rows_into(xext_ref, HALO, acc, tm, gain)
        xext_ref[HALO + tm:, :] = _rmsnorm_rows(xnext_ref[0], gain).astype(_BF16)

    @pl.when(q == tiles - 1)
    def _():
        row = HALO + tail_tokens
        _norm_rows_into(xext_ref, HALO, acc, tail_tokens, gain)
        xext_ref[row:row + HALO, :] = jnp.zeros((HALO, D_MODEL), _BF16)
        xext_ref[row + HALO:row + HALO + N_META, :] = meta_n
        xext_ref[row + HALO + N_META:row + 2 * HALO + N_META, :] = (
            _rmsnorm_rows(xnext_ref[0], gain).astype(_BF16))
        acc[tail_tokens:tail_tokens + LEAD, :] = jnp.zeros((LEAD, D_MODEL), _F32)
        acc[tail_tokens + LEAD:tm, :] = meta


class _CastPlan:
    def __init__(self, w, layer, n_steps, step_of_grid):
        _, rows, cols = w.shape
        n_blocks = next(nb for nb in range(min(n_steps, rows // BF16_SUBLANES), 0, -1)
                        if rows % nb == 0 and (rows // nb) % BF16_SUBLANES == 0)
        block_rows = rows // n_blocks

        def block(*grid_idx):
            return lax.div(step_of_grid(*grid_idx) * n_blocks, n_steps)

        self.operand = w
        self.in_spec = pl.BlockSpec((None, block_rows, cols), lambda *g: (layer, block(*g), 0))
        self.out_spec = pl.BlockSpec((None, block_rows, cols), lambda *g: (0, block(*g), 0))
        self.out_shape = jax.ShapeDtypeStruct((1, rows, cols), _BF16)


def _run_casts(refs):
    n = len(refs) // 2
    for src, dst in zip(refs[:n], refs[n:]):
        dst[...] = src[...].astype(_BF16)


def _conv3(v_ext, w, tm):
    return (v_ext[HALO - 1:HALO - 1 + tm] * w[0:1]
            + v_ext[HALO:HALO + tm] * w[1:2]
            + v_ext[HALO + 1:HALO + 1 + tm] * w[2:3])


def _mixer_kernel(tm, tiles, tail_tokens, n_casts, xprev_ref, x_hbm, xnext_ref, meta_ref, gain_ref, wb_ref,
                  wc_ref, wh_ref, cw_ref, wo_ref, *rest):
    cast_srcs, out_hbm, cast_dsts = rest[:n_casts], rest[n_casts], rest[n_casts + 1:2 * n_casts + 1]
    xext_ref, acc_ref, in_sems, out_sems = rest[2 * n_casts + 1:]
    acc = acc_ref.at[lax.rem(pl.program_id(0), 2)]
    last_tile = lax.rem(pl.program_id(0), tiles) == tiles - 1
    compact_rows = tail_tokens + HALO + N_META
    _run_casts(cast_srcs + cast_dsts)

    def add_all(r):
        acc[...] += r

    def add_compact(r):
        acc[0:tail_tokens, :] += r[0:tail_tokens]
        acc[tm - N_META:tm, :] += r[tail_tokens + HALO:compact_rows]

    def step(m, add):
        x = xext_ref[0:m + 2 * HALO, :]
        ch = _conv3(_dot(x, wc_ref[...]) * _dot(x, wh_ref[...]), cw_ref[...], m)
        y = _dot(xext_ref[HALO:HALO + m, :], wb_ref[...]) * ch
        add(_dot(y.astype(_BF16), wo_ref[...]))

    def body():
        @pl.when(pl.program_id(1) == 0)
        def _():
            _fill_from_tokens(xprev_ref, acc, xnext_ref, meta_ref, gain_ref, xext_ref, tm, tiles,
                              tail_tokens)

        pl.when(jnp.logical_not(last_tile))(functools.partial(step, tm, add_all))
        pl.when(last_tile)(functools.partial(step, compact_rows, add_compact))

    _stream_row_tiles(
        functools.partial(_tile_copy, hbm=x_hbm, acc_ref=acc_ref, sems=in_sems, tm=tm, tiles=tiles,
                          tail_rows=tail_tokens, to_hbm=False),
        functools.partial(_tile_copy, hbm=out_hbm, acc_ref=acc_ref, sems=out_sems, tm=tm, tiles=tiles,
                          tail_rows=tm, to_hbm=True),
        body)


def _ffn_kernel(tm, tiles, out_tail_rows, n_chunks, tok_rows, hprev_ref, h_hbm, hnext_ref, gain_ref,
                wg_ref, wua_ref, wub_ref, cw_ref, cb_ref, wd_ref, out_hbm, xext_ref, acc_ref, in_sems,
                out_sems):
    c = FFN_COL_TILE
    i, j = pl.program_id(0), pl.program_id(1)
    acc = acc_ref.at[lax.rem(i, 2)]
    last_tile = lax.rem(i, tiles) == tiles - 1
    n_pairs, odd = n_chunks // 2, n_chunks % 2
    compact_rows = tok_rows + HALO + N_META

    def add_all(r):
        acc[...] += r

    def add_compact(r):
        acc[0:tok_rows, :] += r[0:tok_rows]
        acc[tm - N_META:tm, :] += r[tok_rows + HALO:compact_rows]

    def steps(m, add):
        def act(g_ext, u, lo):
            g = _conv3(g_ext, cw_ref[:, lo:lo + c], m) + cb_ref[:, lo:lo + c]
            return (_gelu_exact(g) * u).astype(_BF16)

        def pair_step():
            x, xc = xext_ref[0:m + 2 * HALO, :], xext_ref[HALO:HALO + m, :]
            g2 = _dot(x, wg_ref[...])
            ua, ub = _dot(xc, wua_ref[...]), _dot(xc, wub_ref[...])
            a = jnp.concatenate([act(g2[:, :c], ua, 0), act(g2[:, c:], ub, c)], axis=1)
            add(_dot(a, wd_ref[...]))

        def single_step():
            g = _dot(xext_ref[0:m + 2 * HALO, :], wg_ref[:, 0:c])
            ua = _dot(xext_ref[HALO:HALO + m, :], wua_ref[...])
            add(_dot(act(g, ua, 0), wd_ref[0:c, :]))

        return pair_step, single_step

    def body():
        @pl.when(jnp.logical_and(j == 0, jnp.logical_not(last_tile)))
        def _():
            _fill_from_slab(hprev_ref, acc, hnext_ref, gain_ref, xext_ref, tm)

        @pl.when(jnp.logical_and(j == 0, last_tile))
        def _():
            _fill_compact_from_slab(hprev_ref, acc, hnext_ref, gain_ref, xext_ref, tm, tok_rows)

        for this_tile, (pair_step, single_step) in (
                (jnp.logical_not(last_tile), steps(tm, add_all)),
                (last_tile, steps(compact_rows, add_compact))):
            pl.when(jnp.logical_and(this_tile, j < n_pairs))(pair_step)
            if odd:
                pl.when(jnp.logical_and(this_tile, j == n_pairs))(single_step)

    _stream_row_tiles(
        functools.partial(_tile_copy, hbm=h_hbm, acc_ref=acc_ref, sems=in_sems, tm=tm, tiles=tiles,
                          tail_rows=tm, to_hbm=False),
        functools.partial(_tile_copy, hbm=out_hbm, acc_ref=acc_ref, sems=out_sems, tm=tm, tiles=tiles,
                          tail_rows=out_tail_rows, to_hbm=True),
        body)


def _fused_params():
    return pltpu.CompilerParams(dimension_semantics=("arbitrary", "arbitrary"),
                                vmem_limit_bytes=V7X_VMEM_LIMIT_BYTES)


def _fused_scratch(tm, d):
    return [pltpu.VMEM((tm + 2 * HALO, d), _BF16), pltpu.VMEM((2, tm, d), _F32),
            pltpu.SemaphoreType.DMA((2,)), pltpu.SemaphoreType.DMA((2,))]


def _weight_specs(layer, d, hidden, n_up, col_tile):
    n_j = hidden // col_tile
    ups = [pl.BlockSpec((None, d, col_tile),
                        functools.partial(lambda i, j, k: (layer, 0, j + k * n_j), k=k))
           for k in range(n_up)]
    down = pl.BlockSpec((None, col_tile, d), lambda i, j: (layer, j, 0))
    return ups, down


def _mixer_from_tokens(x, meta_tokens, gain, w_in, conv_w, w_out, layer, tm, cast_weights):
    bsz, seq, d = x.shape
    tp = seq + BLOCK
    tiles = tp // tm
    tail_tokens = seq - (tiles - 1) * tm
    assert tiles * tm == tp and 0 < tail_tokens < tm and tail_tokens % HALO == 0 and tm % HALO == 0
    halo_blocks_per_tile = tm // HALO
    col_tile = MIXER_COL_TILE

    def prev_map(i, j):
        b, q = _row_tile_index(i, tiles)
        return b, jnp.maximum(q * halo_blocks_per_tile - 1, 0), 0

    def next_map(i, j):
        b, q = _row_tile_index(i, tiles)
        wrap = q == tiles - 1
        return (jnp.where(wrap, lax.rem(b + 1, bsz), b),
                jnp.where(wrap, 0, (q + 1) * halo_blocks_per_tile), 0)

    ups, down = _weight_specs(layer, d, d, 3, col_tile)
    n_j = d // col_tile
    casts = [_CastPlan(w, l, bsz * tiles * n_j, lambda i, j: i * n_j + j) for w, l in cast_weights]
    in_specs = [
        pl.BlockSpec((1, HALO, d), prev_map),
        pl.BlockSpec(memory_space=pl.ANY),
        pl.BlockSpec((1, HALO, d), next_map),
        pl.BlockSpec((N_META, d), lambda i, j: (0, 0)),
        pl.BlockSpec((1, d), lambda i, j: (0, 0)),
        *ups,
        pl.BlockSpec((None, 3, col_tile), lambda i, j: (layer, 0, j)),
        down,
        *[c.in_spec for c in casts],
    ]
    out, *cast_out = pl.pallas_call(
        functools.partial(_mixer_kernel, tm, tiles, tail_tokens, len(casts)),
        grid=(bsz * tiles, n_j),
        in_specs=in_specs,
        out_specs=[pl.BlockSpec(memory_space=pl.ANY), *[c.out_spec for c in casts]],
        out_shape=[jax.ShapeDtypeStruct((bsz, tp, d), _F32), *[c.out_shape for c in casts]],
        scratch_shapes=_fused_scratch(tm, d),
        compiler_params=_fused_params(),
    )(x, x, x, meta_tokens, gain.reshape(1, d), w_in, w_in, w_in, conv_w, w_out,
      *[c.operand for c in casts])
    return out, cast_out


def _conv_ffn(h, gain, w_up, conv_w, conv_b, w_down, layer, conv_layer, tm, out_rows):
    bsz, tp, d = h.shape
    hidden = w_down.shape[1]
    tiles = tp // tm
    out_tail_rows = out_rows - (tiles - 1) * tm
    tok_rows = tm - BLOCK
    assert 0 < tok_rows and tok_rows % NORM_ROWS == 0 and LEAD >= HALO
    assert tiles * tm == tp and tm % HALO == 0 and 0 < out_tail_rows <= tm and out_tail_rows % F32_SUBLANES == 0
    halo_blocks_per_tile = tm // HALO
    halo_blocks_per_slab = tp // HALO
    n_halo_blocks = bsz * halo_blocks_per_slab
    col_tile = FFN_COL_TILE

    def halo_map(i, j, offset):
        blk = lax.rem(i * halo_blocks_per_tile + offset + n_halo_blocks, n_halo_blocks)
        return lax.div(blk, halo_blocks_per_slab), lax.rem(blk, halo_blocks_per_slab), 0

    n_chunks = hidden // col_tile
    assert n_chunks * col_tile == hidden and w_up.shape[2] == 2 * hidden
    in_specs = [
        pl.BlockSpec((1, HALO, d), functools.partial(halo_map, offset=-1)),
        pl.BlockSpec(memory_space=pl.ANY),
        pl.BlockSpec((1, HALO, d), functools.partial(halo_map, offset=halo_blocks_per_tile)),
        pl.BlockSpec((1, d), lambda i, j: (0, 0)),
        pl.BlockSpec((None, d, 2 * col_tile), lambda i, j: (layer, 0, j)),
        pl.BlockSpec((None, d, col_tile), lambda i, j: (layer, 0, n_chunks + 2 * j)),
        pl.BlockSpec((None, d, col_tile),
                     lambda i, j: (layer, 0, jnp.minimum(n_chunks + 2 * j + 1, 2 * n_chunks - 1))),
        pl.BlockSpec((None, 3, 2 * col_tile), lambda i, j: (conv_layer, 0, j)),
        pl.BlockSpec((None, 1, 2 * col_tile), lambda i, j: (conv_layer, 0, j)),
        pl.BlockSpec((None, 2 * col_tile, d), lambda i, j: (layer, j, 0)),
    ]
    return pl.pallas_call(
        functools.partial(_ffn_kernel, tm, tiles, out_tail_rows, n_chunks, tok_rows),
        grid=(bsz * tiles, (n_chunks + 1) // 2),
        in_specs=in_specs,
        out_specs=pl.BlockSpec(memory_space=pl.ANY),
        out_shape=jax.ShapeDtypeStruct((bsz, out_rows, d), _F32),
        scratch_shapes=_fused_scratch(tm, d),
        compiler_params=_fused_params(),
    )(h, h, h, gain.reshape(1, d), w_up, w_up, w_up, conv_w, conv_b.reshape(conv_b.shape[0], 1, hidden),
      w_down)


def _qkv_kernel(n_norm_cols, h_ref, gain_ref, w_ref, hgain_ref, out_ref, xn_ref):
    _norm_rows_into(xn_ref, 0, h_ref, h_ref.shape[0], gain_ref[...])
    for c in range(w_ref.shape[1] // QKV_COL_TILE):
        acc = _dot(xn_ref[...], w_ref[:, c * QKV_COL_TILE:(c + 1) * QKV_COL_TILE])
        for hd in range(QKV_COL_TILE // HEAD_DIM):
            lo = hd * HEAD_DIM
            cols = slice(c * QKV_COL_TILE + lo, c * QKV_COL_TILE + lo + HEAD_DIM)
            head = acc[:, lo:lo + HEAD_DIM]
            if c * QKV_COL_TILE + lo < n_norm_cols:
                head = _rmsnorm_rows(head, hgain_ref[:, cols])
            out_ref[:, cols] = head.astype(_BF16)


def _qkv_proj(h, gain, w_qkv, layer, head_gain_cols, tm):
    m, d = h.shape
    n_out = w_qkv.shape[2]
    return pl.pallas_call(
        functools.partial(_qkv_kernel, D_MODEL + KV_DIM),
        grid=(m // tm,),
        in_specs=[
            pl.BlockSpec((tm, d), lambda i: (i, 0)),
            pl.BlockSpec((1, d), lambda i: (0, 0)),
            pl.BlockSpec((None, d, n_out), lambda i: (layer, 0, 0), pipeline_mode=pl.Buffered(1)),
            pl.BlockSpec((1, n_out), lambda i: (0, 0)),
        ],
        out_specs=pl.BlockSpec((tm, n_out), lambda i: (i, 0)),
        out_shape=jax.ShapeDtypeStruct((m, n_out), _BF16),
        scratch_shapes=[pltpu.VMEM((tm, d), _BF16)],
        compiler_params=pltpu.CompilerParams(
            dimension_semantics=("arbitrary",), vmem_limit_bytes=V7X_VMEM_LIMIT_BYTES),
    )(h, gain.reshape(1, d), w_qkv, head_gain_cols)


def _bias_kernel(n_casts, band_codes_ref, tail_codes_ref, tbl_ref, *rest):
    out_ref = rest[n_casts]
    _run_casts(rest[:n_casts] + rest[n_casts + 1:])
    head = pl.program_id(0)

    def lookup(codes):
        acc = jnp.zeros(codes.shape, _F32)
        for code in range(N_BUCKETS + 2):
            acc = jnp.where(codes == code, tbl_ref[head, code] * LOG2_E, acc)
        return acc

    band = lookup(band_codes_ref[...])
    tails = [lookup(tail_codes_ref[v]) for v in range(3)]
    masked = jnp.full((BLOCK, BLOCK), NEG_INF, _F32)
    for variant, valid_blocks, tail in ((0, (2,), 0), (1, (1, 2), 1), (2, (0, 1, 2), 2), (3, (0, 1), 2)):
        for blk in range(3):
            cols = slice(blk * BLOCK, (blk + 1) * BLOCK)
            out_ref[variant, 0, :, cols] = band[:, cols] if blk in valid_blocks else masked
        out_ref[variant, 0, :, 3 * BLOCK:] = tails[tail]


def _attn_bias(band_codes, tail_codes, table, cast_weights):
    casts = [_CastPlan(w, l, N_HEADS, lambda h: h) for w, l in cast_weights]
    bias, *cast_out = pl.pallas_call(
        functools.partial(_bias_kernel, len(casts)),
        grid=(N_HEADS,),
        in_specs=[
            pl.BlockSpec((BLOCK, 3 * BLOCK), lambda h: (0, 0)),
            pl.BlockSpec((3, BLOCK, BLOCK), lambda h: (0, 0, 0)),
            pl.BlockSpec(memory_space=pltpu.SMEM),
            *[c.in_spec for c in casts],
        ],
        out_specs=[pl.BlockSpec((N_BIAS_VARIANTS, 1, BLOCK, KEYS), lambda h: (0, h, 0, 0)),
                   *[c.out_spec for c in casts]],
        out_shape=[jax.ShapeDtypeStruct((N_BIAS_VARIANTS, N_HEADS, BLOCK, KEYS), _F32),
                   *[c.out_shape for c in casts]],
    )(band_codes, tail_codes, table, *[c.operand for c in casts])
    return bias, cast_out


def _attn_chain(q_ref, k_refs, v_refs, bias_ref, out_ref, slab, k_slabs, g):
    kv_cols = slice(g * HEAD_DIM, (g + 1) * HEAD_DIM)
    dead = jnp.zeros((BLOCK - N_META, HEAD_DIM), _BF16)
    q = jnp.concatenate(
        [q_ref[slab, :, (g * GROUP + r) * HEAD_DIM:(g * GROUP + r + 1) * HEAD_DIM] for r in range(GROUP)],
        axis=0)
    k = jnp.concatenate([ref[ks, :, kv_cols] for ref, ks in zip(k_refs, k_slabs)] + [dead], axis=0)
    v = jnp.concatenate([ref[ks, :, kv_cols] for ref, ks in zip(v_refs, k_slabs)] + [dead], axis=0)
    s = lax.dot_general(q, k, (((1,), (1,)), ((), ())), preferred_element_type=_F32)
    s = s + bias_ref[0, g * GROUP:(g + 1) * GROUP].reshape(GROUP * BLOCK, KEYS)
    e = jnp.exp2(s - jnp.max(s, axis=-1, keepdims=True))
    denom = jnp.sum(e, axis=-1, keepdims=True)
    o = _dot(e.astype(_BF16), v) / denom
    for r in range(GROUP):
        out_ref[slab, :, (g * GROUP + r) * HEAD_DIM:(g * GROUP + r + 1) * HEAD_DIM] = (
            o[r * BLOCK:(r + 1) * BLOCK].astype(_BF16))


def _attn_kernel(n_casts, q_ref, kp_ref, kc_ref, kn_ref, km_ref, vp_ref, vc_ref, vn_ref, vm_ref, bias_ref,
                 *rest):
    out_ref = rest[n_casts]
    _run_casts(rest[:n_casts] + rest[n_casts + 1:])
    n = pl.program_id(0)
    n_slabs = q_ref.shape[0]
    k_refs = (kp_ref, kc_ref, kn_ref, km_ref)
    v_refs = (vp_ref, vc_ref, vn_ref, vm_ref)

    def all_chains(first_block):
        for slab in range(n_slabs):
            if first_block:
                k_slabs = (slab, slab, (slab + 1) % n_slabs, slab)
            else:
                k_slabs = (slab, slab, slab, (slab - 1) % n_slabs)
            for g in range(N_KV_HEADS):
                _attn_chain(q_ref, k_refs, v_refs, bias_ref, out_ref, slab, k_slabs, g)

    @pl.when(n == 0)
    def _():
        all_chains(True)
        out_ref[:, 0:LEAD, :] = jnp.zeros((n_slabs, LEAD, D_MODEL), _BF16)

    @pl.when(n > 0)
    def _():
        all_chains(False)


def _attention(qkv, bias, nb, cast_weights):
    bsz = qkv.shape[0]
    casts = [_CastPlan(w, l, nb, lambda n: n) for w, l in cast_weights]
    k_col, v_col = D_MODEL // KV_DIM, D_MODEL // KV_DIM + 1

    def slab_block(n):
        return jnp.where(n == 0, nb - 1, n - 1)

    def blk(col, shift):
        return pl.BlockSpec(
            (bsz, BLOCK, KV_DIM), lambda n: (0, slab_block(jnp.clip(n + shift, 0, nb - 1)), col))

    def meta(col):
        return pl.BlockSpec(
            (bsz, N_META, KV_DIM), lambda n: (0, (nb - 1) * (BLOCK // N_META) + LEAD // N_META, col))

    out, *cast_out = pl.pallas_call(
        functools.partial(_attn_kernel, len(casts)),
        grid=(nb,),
        in_specs=[
            pl.BlockSpec((bsz, BLOCK, D_MODEL), lambda n: (0, slab_block(n), 0)),
            blk(k_col, -1), blk(k_col, 0), blk(k_col, 1), meta(k_col),
            blk(v_col, -1), blk(v_col, 0), blk(v_col, 1), meta(v_col),
            pl.BlockSpec((1, N_HEADS, BLOCK, KEYS),
                         lambda n: (jnp.where(n == nb - 1, 3, jnp.minimum(n, 2)), 0, 0, 0)),
            *[c.in_spec for c in casts],
        ],
        out_specs=[pl.BlockSpec((bsz, BLOCK, D_MODEL), lambda n: (0, slab_block(n), 0)),
                   *[c.out_spec for c in casts]],
        out_shape=[jax.ShapeDtypeStruct((bsz, nb * BLOCK, D_MODEL), _BF16),
                   *[c.out_shape for c in casts]],
        compiler_params=pltpu.CompilerParams(
            dimension_semantics=("arbitrary",), vmem_limit_bytes=V7X_VMEM_LIMIT_BYTES),
    )(qkv, qkv, qkv, qkv, qkv, qkv, qkv, qkv, qkv, bias, *[c.operand for c in casts])
    return out, cast_out


def _out_proj_kernel(a_ref, w_ref, h_ref, out_ref):
    out_ref[...] = h_ref[...] + _dot(a_ref[...], w_ref[...])


def _out_proj(a, w, layer, h, tm):
    m, d = h.shape
    return pl.pallas_call(
        _out_proj_kernel,
        grid=(m // tm,),
        in_specs=[
            pl.BlockSpec((tm, d), lambda i: (i, 0)),
            pl.BlockSpec((None, d, d), lambda i: (layer, 0, 0), pipeline_mode=pl.Buffered(1)),
            pl.BlockSpec((tm, d), lambda i: (i, 0)),
        ],
        out_specs=pl.BlockSpec((tm, d), lambda i: (i, 0)),
        out_shape=jax.ShapeDtypeStruct((m, d), _F32),
        compiler_params=pltpu.CompilerParams(
            dimension_semantics=("arbitrary",), vmem_limit_bytes=V7X_VMEM_LIMIT_BYTES),
    )(a, w, h)


def _t5_bucket(rel):
    half = N_BUCKETS // 2
    max_exact = half // 2
    side = jnp.where(rel > 0, half, 0)
    n = jnp.abs(rel)
    nf = jnp.maximum(n, 1).astype(_F32)
    large = max_exact + (jnp.log(nf / max_exact) / math.log(MAX_DISTANCE / max_exact)
                         * (half - max_exact)).astype(jnp.int32)
    large = jnp.minimum(large, half - 1)
    return side + jnp.where(n < max_exact, n, large)


def _bias_codes():
    qi = jnp.arange(BLOCK, dtype=jnp.int32)[:, None]
    col = jnp.arange(3 * BLOCK, dtype=jnp.int32)[None, :]
    rel_band = col - BLOCK - qi
    band = jnp.where(jnp.abs(rel_band) <= WINDOW, _t5_bucket(rel_band), CODE_MASKED)
    tcol = jnp.arange(BLOCK, dtype=jnp.int32)[None, :]
    tails = []
    for n_rep in range(3):
        qpos = n_rep * BLOCK + qi
        codes = jnp.where(tcol < N_META, _t5_bucket(LEAD + tcol - qpos), CODE_MASKED)
        tails.append(jnp.where(tcol == N_META, CODE_SINK, codes))
    return band.astype(jnp.int32), jnp.stack(tails).astype(jnp.int32)


def kernel(x, meta_tokens, rel_bias_table, norm_mix, norm_ffn, conv_in_w, conv_dw, conv_out_w,
           attn_qkv, attn_q_gain, attn_k_gain, attn_sink, attn_o, ffn_up, ffn_dw, ffn_dw_b, ffn_down):
    bsz, seq, d = x.shape
    tp = seq + BLOCK
    nb = tp // BLOCK
    tm = tp // FFN_ROW_TILES_PER_SLAB
    assert tm * FFN_ROW_TILES_PER_SLAB == tp and d == D_MODEL and seq % BLOCK == 0 and nb >= 4

    table = jnp.concatenate([
        rel_bias_table.T.astype(_F32), jnp.full((N_HEADS, 1), NEG_INF, _F32),
        attn_sink[0].astype(_F32)[:, None]], axis=1)
    bias, (in_w, out_w) = _attn_bias(*_bias_codes(), table, [(conv_in_w, 0), (conv_out_w, 0)])

    h, (up0, down0, qkv_w, o_w) = _mixer_from_tokens(
        x, meta_tokens.astype(x.dtype), norm_mix[0], in_w, conv_dw, out_w, 0,
        tp // MIXER_ROW_TILES_PER_SLAB, [(ffn_up, 0), (ffn_down, 0), (attn_qkv, 0), (attn_o, 0)])
    h = _conv_ffn(h, norm_ffn[0], up0, ffn_dw, ffn_dw_b, down0, 0, 0, tm, tp)

    q_scale = HEAD_DIM ** -0.5 * LOG2_E
    head_gain_cols = jnp.concatenate([
        jnp.tile(attn_q_gain[0] * q_scale, N_HEADS), jnp.tile(attn_k_gain[0], N_KV_HEADS),
        jnp.ones((KV_DIM,), _F32)]).reshape(1, QKV_DIM)
    h2 = h.reshape(bsz * tp, d)
    qkv = _qkv_proj(h2, norm_mix[1], qkv_w, 0, head_gain_cols, tp // QKV_ROW_TILES_PER_SLAB)
    o, (up1, down1) = _attention(qkv.reshape(bsz, tp, QKV_DIM), bias, nb, [(ffn_up, 1), (ffn_down, 1)])
    h2 = _out_proj(o.reshape(bsz * tp, d), o_w, 0, h2, tp // OUT_PROJ_ROW_TILES_PER_SLAB)
    return _conv_ffn(h2.reshape(bsz, tp, d), norm_ffn[1], up1, ffn_dw, ffn_dw_b, down1, 0, 1, tm, seq)
```

```python
import functools
import math

import jax
import jax.numpy as jnp
from jax import lax
from jax.experimental import pallas as pl
from jax.experimental.pallas import tpu as pltpu

D_MODEL = 2048
N_META = 16
N_HEADS = 16
N_KV_HEADS = 4
HEAD_DIM = D_MODEL // N_HEADS
GROUP = N_HEADS // N_KV_HEADS
KV_DIM = N_KV_HEADS * HEAD_DIM
QKV_DIM = D_MODEL + 2 * KV_DIM
WINDOW = 128
BLOCK = 128
N_BUCKETS = 32
MAX_DISTANCE = 128
EPS = 1e-6
LEAD = BLOCK - N_META

F32_SUBLANES = 8
BF16_SUBLANES = 16
HALO = BF16_SUBLANES
NORM_ROWS = 32
FFN_ROW_TILES_PER_SLAB = 4
MIXER_ROW_TILES_PER_SLAB = 4
QKV_ROW_TILES_PER_SLAB = 4
OUT_PROJ_ROW_TILES_PER_SLAB = 8
MIXER_COL_TILE = 512
FFN_COL_TILE = 512
QKV_COL_TILE = 512
KEYS = 4 * BLOCK
CODE_MASKED = N_BUCKETS
CODE_SINK = N_BUCKETS + 1
N_BIAS_VARIANTS = 4
NEG_INF = -1e30
LOG2_E = math.log2(math.e)
V7X_VMEM_LIMIT_BYTES = 56 * 1024 * 1024

_BF16 = jnp.bfloat16
_F32 = jnp.float32


def _dot(a, b):
    return jnp.dot(a, b, preferred_element_type=_F32)


def _rmsnorm_rows(v, gain):
    return v * lax.rsqrt(jnp.mean(v * v, axis=-1, keepdims=True) + EPS) * gain


def _norm_rows_into(xext_ref, row0, src_ref, n_rows, gain):
    for r in range(0, n_rows, NORM_ROWS):
        rows = min(NORM_ROWS, n_rows - r)
        xext_ref[row0 + r:row0 + r + rows, :] = (
            _rmsnorm_rows(src_ref[r:r + rows, :], gain).astype(_BF16))


def _gelu_exact(v):
    return 0.5 * v * (1.0 + lax.erf(v * math.sqrt(0.5)))


def _row_tile_index(i, tiles):
    return lax.div(i, tiles), lax.rem(i, tiles)


def _tile_copy(op, hbm, acc_ref, sems, tm, tiles, tail_rows, to_hbm, t):
    b, q = _row_tile_index(t, tiles)
    slot = lax.rem(t, 2)

    def copy(rows):
        hbm_rows = hbm.at[b, pl.ds(q * tm, rows), :]
        acc_rows = acc_ref.at[slot, pl.ds(0, rows), :]
        src, dst = (acc_rows, hbm_rows) if to_hbm else (hbm_rows, acc_rows)
        op(pltpu.make_async_copy(src, dst, sems.at[slot]))

    if tail_rows == tm:
        copy(tm)
    else:
        pl.when(q < tiles - 1)(lambda: copy(tm))
        pl.when(q == tiles - 1)(lambda: copy(tail_rows))


def _stream_row_tiles(load, store, body):
    i, j = pl.program_id(0), pl.program_id(1)
    n_i, n_j = pl.num_programs(0), pl.num_programs(1)
    start, wait = (lambda c: c.start()), (lambda c: c.wait())

    pl.when(jnp.logical_and(i == 0, j == 0))(lambda: load(start, t=i))
    pl.when(j == 0)(lambda: load(wait, t=i))
    pl.when(jnp.logical_and(j == 1, i >= 1))(lambda: store(wait, t=i - 1))
    pl.when(jnp.logical_and(j == 1, i + 1 < n_i))(lambda: load(start, t=i + 1))
    body()
    pl.when(j == n_j - 1)(lambda: store(start, t=i))
    pl.when(jnp.logical_and(j == n_j - 1, i == n_i - 1))(lambda: store(wait, t=i))


def _fill_from_slab(hprev_ref, acc, hnext_ref, gain_ref, xext_ref, tm):
    gain = gain_ref[...]
    xext_ref[0:HALO, :] = _rmsnorm_rows(hprev_ref[0], gain).astype(_BF16)
    _norm_rows_into(xext_ref, HALO, acc, tm, gain)
    xext_ref[HALO + tm:, :] = _rmsnorm_rows(hnext_ref[0], gain).astype(_BF16)


def _fill_compact_from_slab(hprev_ref, acc, hnext_ref, gain_ref, xext_ref, tm, tok_rows):
    gain = gain_ref[...]
    row = HALO + tok_rows
    xext_ref[0:HALO, :] = _rmsnorm_rows(hprev_ref[0], gain).astype(_BF16)
    _norm_rows_into(xext_ref, HALO, acc, tok_rows, gain)
    xext_ref[row:row + HALO, :] = jnp.zeros((HALO, D_MODEL), _BF16)
    xext_ref[row + HALO:row + HALO + N_META, :] = (
        _rmsnorm_rows(acc[tm - N_META:tm, :], gain).astype(_BF16))
    xext_ref[row + HALO + N_META:row + 2 * HALO + N_META, :] = (
        _rmsnorm_rows(hnext_ref[0], gain).astype(_BF16))


def _fill_from_tokens(xprev_ref, acc, xnext_ref, meta_ref, gain_ref, xext_ref, tm, tiles, tail_tokens):
    q = lax.rem(pl.program_id(0), tiles)
    gain = gain_ref[...]
    meta = meta_ref[...]
    meta_n = _rmsnorm_rows(meta, gain).astype(_BF16)

    @pl.when(q == 0)
    def _():
        xext_ref[0:HALO, :] = meta_n

    @pl.when(q > 0)
    def _():
        xext_ref[0:HALO, :] = _rmsnorm_rows(xprev_ref[0], gain).astype(_BF16)

    @pl.when(q < tiles - 1)
    def _():
        _norm_rows_into(xext_ref, HALO, acc, tm, gain)
        xext_ref[HALO + tm:, :] = _rmsnorm_rows(xnext_ref[0], gain).astype(_BF16)

    @pl.when(q == tiles - 1)
    def _():
        row = HALO + tail_tokens
        _norm_rows_into(xext_ref, HALO, acc, tail_tokens, gain)
        xext_ref[row:row + HALO, :] = jnp.zeros((HALO, D_MODEL), _BF16)
        xext_ref[row + HALO:row + HALO + N_META, :] = meta_n
        xext_ref[row + HALO + N_META:row + 2 * HALO + N_META, :] = (
            _rmsnorm_rows(xnext_ref[0], gain).astype(_BF16))
        acc[tail_tokens:tail_tokens + LEAD, :] = jnp.zeros((LEAD, D_MODEL), _F32)
        acc[tail_tokens + LEAD:tm, :] = meta


class _CastPlan:
    def __init__(self, w, layer, n_steps, step_of_grid):
        _, rows, cols = w.shape
        n_blocks = next(nb for nb in range(min(n_steps, rows // BF16_SUBLANES), 0, -1)
                        if rows % nb == 0 and (rows // nb) % BF16_SUBLANES == 0)
        block_rows = rows // n_blocks

        def block(*grid_idx):
            return lax.div(step_of_grid(*grid_idx) * n_blocks, n_steps)

        self.operand = w
        self.in_spec = pl.BlockSpec((None, block_rows, cols), lambda *g: (layer, block(*g), 0))
        self.out_spec = pl.BlockSpec((None, block_rows, cols), lambda *g: (0, block(*g), 0))
        self.out_shape = jax.ShapeDtypeStruct((1, rows, cols), _BF16)


def _run_casts(refs):
    n = len(refs) // 2
    for src, dst in zip(refs[:n], refs[n:]):
        dst[...] = src[...].astype(_BF16)


def _conv3(v_ext, w, tm):
    return (v_ext[HALO - 1:HALO - 1 + tm] * w[0:1]
            + v_ext[HALO:HALO + tm] * w[1:2]
            + v_ext[HALO + 1:HALO + 1 + tm] * w[2:3])


def _mixer_kernel(tm, tiles, tail_tokens, n_casts, xprev_ref, x_hbm, xnext_ref, meta_ref, gain_ref, wb_ref,
                  wc_ref, wh_ref, cw_ref, wo_ref, *rest):
    cast_srcs, out_hbm, cast_dsts = rest[:n_casts], rest[n_casts], rest[n_casts + 1:2 * n_casts + 1]
    xext_ref, acc_ref, in_sems, out_sems = rest[2 * n_casts + 1:]
    acc = acc_ref.at[lax.rem(pl.program_id(0), 2)]
    last_tile = lax.rem(pl.program_id(0), tiles) == tiles - 1
    compact_rows = tail_tokens + HALO + N_META
    _run_casts(cast_srcs + cast_dsts)

    def add_all(r):
        acc[...] += r

    def add_compact(r):
        acc[0:tail_tokens, :] += r[0:tail_tokens]
        acc[tm - N_META:tm, :] += r[tail_tokens + HALO:compact_rows]

    def step(m, add):
        x = xext_ref[0:m + 2 * HALO, :]
        ch = _conv3(_dot(x, wc_ref[...]) * _dot(x, wh_ref[...]), cw_ref[...], m)
        y = _dot(xext_ref[HALO:HALO + m, :], wb_ref[...]) * ch
        add(_dot(y.astype(_BF16), wo_ref[...]))

    def body():
        @pl.when(pl.program_id(1) == 0)
        def _():
            _fill_from_tokens(xprev_ref, acc, xnext_ref, meta_ref, gain_ref, xext_ref, tm, tiles,
                              tail_tokens)

        pl.when(jnp.logical_not(last_tile))(functools.partial(step, tm, add_all))
        pl.when(last_tile)(functools.partial(step, compact_rows, add_compact))

    _stream_row_tiles(
        functools.partial(_tile_copy, hbm=x_hbm, acc_ref=acc_ref, sems=in_sems, tm=tm, tiles=tiles,
                          tail_rows=tail_tokens, to_hbm=False),
        functools.partial(_tile_copy, hbm=out_hbm, acc_ref=acc_ref, sems=out_sems, tm=tm, tiles=tiles,
                          tail_rows=tm, to_hbm=True),
        body)


def _ffn_kernel(tm, tiles, out_tail_rows, n_chunks, tok_rows, hprev_ref, h_hbm, hnext_ref, gain_ref,
                wg_ref, wua_ref, wub_ref, cw_ref, cb_ref, wd_ref, out_hbm, xext_ref, acc_ref, in_sems,
                out_sems):
    c = FFN_COL_TILE
    i, j = pl.program_id(0), pl.program_id(1)
    acc = acc_ref.at[lax.rem(i, 2)]
    last_tile = lax.rem(i, tiles) == tiles - 1
    n_pairs, odd = n_chunks // 2, n_chunks % 2
    compact_rows = tok_rows + HALO + N_META

    def add_all(r):
        acc[...] += r

    def add_compact(r):
        acc[0:tok_rows, :] += r[0:tok_rows]
        acc[tm - N_META:tm, :] += r[tok_rows + HALO:compact_rows]

    def steps(m, add):
        def act(g_ext, u, lo):
            g = _conv3(g_ext, cw_ref[:, lo:lo + c], m) + cb_ref[:, lo:lo + c]
            return (_gelu_exact(g) * u).astype(_BF16)

        def pair_step():
            x, xc = xext_ref[0:m + 2 * HALO, :], xext_ref[HALO:HALO + m, :]
            g2 = _dot(x, wg_ref[...])
            ua, ub = _dot(xc, wua_ref[...]), _dot(xc, wub_ref[...])
            a = jnp.concatenate([act(g2[:, :c], ua, 0), act(g2[:, c:], ub, c)], axis=1)
            add(_dot(a, wd_ref[...]))

        def single_step():
            g = _dot(xext_ref[0:m + 2 * HALO, :], wg_ref[:, 0:c])
            ua = _dot(xext_ref[HALO:HALO + m, :], wua_ref[...])
            add(_dot(act(g, ua, 0), wd_ref[0:c, :]))

        return pair_step, single_step

    def body():
        @pl.when(jnp.logical_and(j == 0, jnp.logical_not(last_tile)))
        def _():
            _fill_from_slab(hprev_ref, acc, hnext_ref, gain_ref, xext_ref, tm)

        @pl.when(jnp.logical_and(j == 0, last_tile))
        def _():
            _fill_compact_from_slab(hprev_ref, acc, hnext_ref, gain_ref, xext_ref, tm, tok_rows)

        for this_tile, (pair_step, single_step) in (
                (jnp.logical_not(last_tile), steps(tm, add_all)),
                (last_tile, steps(compact_rows, add_compact))):
            pl.when(jnp.logical_and(this_tile, j < n_pairs))(pair_step)
            if odd:
                pl.when(jnp.logical_and(this_tile, j == n_pairs))(single_step)

    _stream_row_tiles(
        functools.partial(_tile_copy, hbm=h_hbm, acc_ref=acc_ref, sems=in_sems, tm=tm, tiles=tiles,
                          tail_rows=tm, to_hbm=False),
        functools.partial(_tile_copy, hbm=out_hbm, acc_ref=acc_ref, sems=out_sems, tm=tm, tiles=tiles,
                          tail_rows=out_tail_rows, to_hbm=True),
        body)


def _fused_params():
    return pltpu.CompilerParams(dimension_semantics=("arbitrary", "arbitrary"),
                                vmem_limit_bytes=V7X_VMEM_LIMIT_BYTES)


def _fused_scratch(tm, d):
    return [pltpu.VMEM((tm + 2 * HALO, d), _BF16), pltpu.VMEM((2, tm, d), _F32),
            pltpu.SemaphoreType.DMA((2,)), pltpu.SemaphoreType.DMA((2,))]


def _weight_specs(layer, d, hidden, n_up, col_tile):
    n_j = hidden // col_tile
    ups = [pl.BlockSpec((None, d, col_tile),
                        functools.partial(lambda i, j, k: (layer, 0, j + k * n_j), k=k))
           for k in range(n_up)]
    down = pl.BlockSpec((None, col_tile, d), lambda i, j: (layer, j, 0))
    return ups, down


def _mixer_from_tokens(x, meta_tokens, gain, w_in, conv_w, w_out, layer, tm, cast_weights):
    bsz, seq, d = x.shape
    tp = seq + BLOCK
    tiles = tp // tm
    tail_tokens = seq - (tiles - 1) * tm
    assert tiles * tm == tp and 0 < tail_tokens < tm and tail_tokens % HALO == 0 and tm % HALO == 0
    halo_blocks_per_tile = tm // HALO
    col_tile = MIXER_COL_TILE

    def prev_map(i, j):
        b, q = _row_tile_index(i, tiles)
        return b, jnp.maximum(q * halo_blocks_per_tile - 1, 0), 0

    def next_map(i, j):
        b, q = _row_tile_index(i, tiles)
        wrap = q == tiles - 1
        return (jnp.where(wrap, lax.rem(b + 1, bsz), b),
                jnp.where(wrap, 0, (q + 1) * halo_blocks_per_tile), 0)

    ups, down = _weight_specs(layer, d, d, 3, col_tile)
    n_j = d // col_tile
    casts = [_CastPlan(w, l, bsz * tiles * n_j, lambda i, j: i * n_j + j) for w, l in cast_weights]
    in_specs = [
        pl.BlockSpec((1, HALO, d), prev_map),
        pl.BlockSpec(memory_space=pl.ANY),
        pl.BlockSpec((1, HALO, d), next_map),
        pl.BlockSpec((N_META, d), lambda i, j: (0, 0)),
        pl.BlockSpec((1, d), lambda i, j: (0, 0)),
        *ups,
        pl.BlockSpec((None, 3, col_tile), lambda i, j: (layer, 0, j)),
        down,
        *[c.in_spec for c in casts],
    ]
    out, *cast_out = pl.pallas_call(
        functools.partial(_mixer_kernel, tm, tiles, tail_tokens, len(casts)),
        grid=(bsz * tiles, n_j),
        in_specs=in_specs,
        out_specs=[pl.BlockSpec(memory_space=pl.ANY), *[c.out_spec for c in casts]],
        out_shape=[jax.ShapeDtypeStruct((bsz, tp, d), _F32), *[c.out_shape for c in casts]],
        scratch_shapes=_fused_scratch(tm, d),
        compiler_params=_fused_params(),
    )(x, x, x, meta_tokens, gain.reshape(1, d), w_in, w_in, w_in, conv_w, w_out,
      *[c.operand for c in casts])
    return out, cast_out


def _conv_ffn(h, gain, w_up, conv_w, conv_b, w_down, layer, conv_layer, tm, out_rows):
    bsz, tp, d = h.shape
    hidden = w_down.shape[1]
    tiles = tp // tm
    out_tail_rows = out_rows - (tiles - 1) * tm
    tok_rows = tm - BLOCK
    assert 0 < tok_rows and tok_rows % NORM_ROWS == 0 and LEAD >= HALO
    assert tiles * tm == tp and tm % HALO == 0 and 0 < out_tail_rows <= tm and out_tail_rows % F32_SUBLANES == 0
    halo_blocks_per_tile = tm // HALO
    halo_blocks_per_slab = tp // HALO
    n_halo_blocks = bsz * halo_blocks_per_slab
    col_tile = FFN_COL_TILE

    def halo_map(i, j, offset):
        blk = lax.rem(i * halo_blocks_per_tile + offset + n_halo_blocks, n_halo_blocks)
        return lax.div(blk, halo_blocks_per_slab), lax.rem(blk, halo_blocks_per_slab), 0

    n_chunks = hidden // col_tile
    assert n_chunks * col_tile == hidden and w_up.shape[2] == 2 * hidden
    in_specs = [
        pl.BlockSpec((1, HALO, d), functools.partial(halo_map, offset=-1)),
        pl.BlockSpec(memory_space=pl.ANY),
        pl.BlockSpec((1, HALO, d), functools.partial(halo_map, offset=halo_blocks_per_tile)),
        pl.BlockSpec((1, d), lambda i, j: (0, 0)),
        pl.BlockSpec((None, d, 2 * col_tile), lambda i, j: (layer, 0, j)),
        pl.BlockSpec((None, d, col_tile), lambda i, j: (layer, 0, n_chunks + 2 * j)),
        pl.BlockSpec((None, d, col_tile),
                     lambda i, j: (layer, 0, jnp.minimum(n_chunks + 2 * j + 1, 2 * n_chunks - 1))),
        pl.BlockSpec((None, 3, 2 * col_tile), lambda i, j: (conv_layer, 0, j)),
        pl.BlockSpec((None, 1, 2 * col_tile), lambda i, j: (conv_layer, 0, j)),
        pl.BlockSpec((None, 2 * col_tile, d), lambda i, j: (layer, j, 0)),
    ]
    return pl.pallas_call(
        functools.partial(_ffn_kernel, tm, tiles, out_tail_rows, n_chunks, tok_rows),
        grid=(bsz * tiles, (n_chunks + 1) // 2),
        in_specs=in_specs,
        out_specs=pl.BlockSpec(memory_space=pl.ANY),
        out_shape=jax.ShapeDtypeStruct((bsz, out_rows, d), _F32),
        scratch_shapes=_fused_scratch(tm, d),
        compiler_params=_fused_params(),
    )(h, h, h, gain.reshape(1, d), w_up, w_up, w_up, conv_w, conv_b.reshape(conv_b.shape[0], 1, hidden),
      w_down)


def _qkv_kernel(n_norm_cols, h_ref, gain_ref, w_ref, hgain_ref, out_ref, xn_ref):
    _norm_rows_into(xn_ref, 0, h_ref, h_ref.shape[0], gain_ref[...])
    for c in range(w_ref.shape[1] // QKV_COL_TILE):
        acc = _dot(xn_ref[...], w_ref[:, c * QKV_COL_TILE:(c + 1) * QKV_COL_TILE])
        for hd in range(QKV_COL_TILE // HEAD_DIM):
            lo = hd * HEAD_DIM
            cols = slice(c * QKV_COL_TILE + lo, c * QKV_COL_TILE + lo + HEAD_DIM)
            head = acc[:, lo:lo + HEAD_DIM]
            if c * QKV_COL_TILE + lo < n_norm_cols:
                head = _rmsnorm_rows(head, hgain_ref[:, cols])
            out_ref[:, cols] = head.astype(_BF16)


def _qkv_proj(h, gain, w_qkv, layer, head_gain_cols, tm):
    m, d = h.shape
    n_out = w_qkv.shape[2]
    return pl.pallas_call(
        functools.partial(_qkv_kernel, D_MODEL + KV_DIM),
        grid=(m // tm,),
        in_specs=[
            pl.BlockSpec((tm, d), lambda i: (i, 0)),
            pl.BlockSpec((1, d), lambda i: (0, 0)),
            pl.BlockSpec((None, d, n_out), lambda i: (layer, 0, 0), pipeline_mode=pl.Buffered(1)),
            pl.BlockSpec((1, n_out), lambda i: (0, 0)),
        ],
        out_specs=pl.BlockSpec((tm, n_out), lambda i: (i, 0)),
        out_shape=jax.ShapeDtypeStruct((m, n_out), _BF16),
        scratch_shapes=[pltpu.VMEM((tm, d), _BF16)],
        compiler_params=pltpu.CompilerParams(
            dimension_semantics=("arbitrary",), vmem_limit_bytes=V7X_VMEM_LIMIT_BYTES),
    )(h, gain.reshape(1, d), w_qkv, head_gain_cols)


def _bias_kernel(n_casts, band_codes_ref, tail_codes_ref, tbl_ref, *rest):
    out_ref = rest[n_casts]
    _run_casts(rest[:n_casts] + rest[n_casts + 1:])
    head = pl.program_id(0)

    def lookup(codes):
        acc = jnp.zeros(codes.shape, _F32)
        for code in range(N_BUCKETS + 2):
            acc = jnp.where(codes == code, tbl_ref[head, code] * LOG2_E, acc)
        return acc

    band = lookup(band_codes_ref[...])
    tails = [lookup(tail_codes_ref[v]) for v in range(3)]
    masked = jnp.full((BLOCK, BLOCK), NEG_INF, _F32)
    for variant, valid_blocks, tail in ((0, (2,), 0), (1, (1, 2), 1), (2, (0, 1, 2), 2), (3, (0, 1), 2)):
        for blk in range(3):
            cols = slice(blk * BLOCK, (blk + 1) * BLOCK)
            out_ref[variant, 0, :, cols] = band[:, cols] if blk in valid_blocks else masked
        out_ref[variant, 0, :, 3 * BLOCK:] = tails[tail]


def _attn_bias(band_codes, tail_codes, table, cast_weights):
    casts = [_CastPlan(w, l, N_HEADS, lambda h: h) for w, l in cast_weights]
    bias, *cast_out = pl.pallas_call(
        functools.partial(_bias_kernel, len(casts)),
        grid=(N_HEADS,),
        in_specs=[
            pl.BlockSpec((BLOCK, 3 * BLOCK), lambda h: (0, 0)),
            pl.BlockSpec((3, BLOCK, BLOCK), lambda h: (0, 0, 0)),
            pl.BlockSpec(memory_space=pltpu.SMEM),
            *[c.in_spec for c in casts],
        ],
        out_specs=[pl.BlockSpec((N_BIAS_VARIANTS, 1, BLOCK, KEYS), lambda h: (0, h, 0, 0)),
                   *[c.out_spec for c in casts]],
        out_shape=[jax.ShapeDtypeStruct((N_BIAS_VARIANTS, N_HEADS, BLOCK, KEYS), _F32),
                   *[c.out_shape for c in casts]],
    )(band_codes, tail_codes, table, *[c.operand for c in casts])
    return bias, cast_out


def _attn_chain(q_ref, k_refs, v_refs, bias_ref, out_ref, slab, k_slabs, g, row0):
    rows = BLOCK - row0
    kv_cols = slice(g * HEAD_DIM, (g + 1) * HEAD_DIM)
    head_cols = [slice((g * GROUP + r) * HEAD_DIM, (g * GROUP + r + 1) * HEAD_DIM) for r in range(GROUP)]
    dead = jnp.zeros((BLOCK - N_META, HEAD_DIM), _BF16)
    q = jnp.concatenate([q_ref[slab, row0:, cols] for cols in head_cols], axis=0)
    k = jnp.concatenate([ref[ks, :, kv_cols] for ref, ks in zip(k_refs, k_slabs)] + [dead], axis=0)
    v = jnp.concatenate([ref[ks, :, kv_cols] for ref, ks in zip(v_refs, k_slabs)] + [dead], axis=0)
    s = lax.dot_general(q, k, (((1,), (1,)), ((), ())), preferred_element_type=_F32)
    s = s + jnp.concatenate([bias_ref[0, g * GROUP + r, row0:, :] for r in range(GROUP)], axis=0)
    e = jnp.exp2(s - jnp.max(s, axis=-1, keepdims=True))
    denom = jnp.sum(e, axis=-1, keepdims=True)
    o = _dot(e.astype(_BF16), v) / denom
    for r, cols in enumerate(head_cols):
        out_ref[slab, row0:, cols] = o[r * rows:(r + 1) * rows].astype(_BF16)


def _attn_kernel(n_casts, q_ref, kp_ref, kc_ref, kn_ref, km_ref, vp_ref, vc_ref, vn_ref, vm_ref, bias_ref,
                 *rest):
    out_ref = rest[n_casts]
    _run_casts(rest[:n_casts] + rest[n_casts + 1:])
    n = pl.program_id(0)
    n_slabs = q_ref.shape[0]
    k_refs = (kp_ref, kc_ref, kn_ref, km_ref)
    v_refs = (vp_ref, vc_ref, vn_ref, vm_ref)

    def all_chains(first_block):
        for slab in range(n_slabs):
            if first_block:
                k_slabs = (slab, slab, (slab + 1) % n_slabs, slab)
            else:
                k_slabs = (slab, slab, slab, (slab - 1) % n_slabs)
            for g in range(N_KV_HEADS):
                _attn_chain(q_ref, k_refs, v_refs, bias_ref, out_ref, slab, k_slabs, g,
                            LEAD if first_block else 0)

    @pl.when(n == 0)
    def _():
        all_chains(True)
        out_ref[:, 0:LEAD, :] = jnp.zeros((n_slabs, LEAD, D_MODEL), _BF16)

    @pl.when(n > 0)
    def _():
        all_chains(False)


def _attention(qkv, bias, nb, cast_weights):
    bsz = qkv.shape[0]
    casts = [_CastPlan(w, l, nb, lambda n: n) for w, l in cast_weights]
    k_col, v_col = D_MODEL // KV_DIM, D_MODEL // KV_DIM + 1

    def slab_block(n):
        return jnp.where(n == 0, nb - 1, n - 1)

    def blk(col, shift):
        return pl.BlockSpec(
            (bsz, BLOCK, KV_DIM), lambda n: (0, slab_block(jnp.clip(n + shift, 0, nb - 1)), col))

    def meta(col):
        return pl.BlockSpec(
            (bsz, N_META, KV_DIM), lambda n: (0, (nb - 1) * (BLOCK // N_META) + LEAD // N_META, col))

    out, *cast_out = pl.pallas_call(
        functools.partial(_attn_kernel, len(casts)),
        grid=(nb,),
        in_specs=[
            pl.BlockSpec((bsz, BLOCK, D_MODEL), lambda n: (0, slab_block(n), 0)),
            blk(k_col, -1), blk(k_col, 0), blk(k_col, 1), meta(k_col),
            blk(v_col, -1), blk(v_col, 0), blk(v_col, 1), meta(v_col),
            pl.BlockSpec((1, N_HEADS, BLOCK, KEYS),
                         lambda n: (jnp.where(n == nb - 1, 3, jnp.minimum(n, 2)), 0, 0, 0)),
            *[c.in_spec for c in casts],
        ],
        out_specs=[pl.BlockSpec((bsz, BLOCK, D_MODEL), lambda n: (0, slab_block(n), 0)),
                   *[c.out_spec for c in casts]],
        out_shape=[jax.ShapeDtypeStruct((bsz, nb * BLOCK, D_MODEL), _BF16),
                   *[c.out_shape for c in casts]],
        compiler_params=pltpu.CompilerParams(
            dimension_semantics=("arbitrary",), vmem_limit_bytes=V7X_VMEM_LIMIT_BYTES),
    )(qkv, qkv, qkv, qkv, qkv, qkv, qkv, qkv, qkv, bias, *[c.operand for c in casts])
    return out, cast_out


def _out_proj_kernel(a_ref, w_ref, h_ref, out_ref):
    out_ref[...] = h_ref[...] + _dot(a_ref[...], w_ref[...])


def _out_proj(a, w, layer, h, tm):
    m, d = h.shape
    return pl.pallas_call(
        _out_proj_kernel,
        grid=(m // tm,),
        in_specs=[
            pl.BlockSpec((tm, d), lambda i: (i, 0)),
            pl.BlockSpec((None, d, d), lambda i: (layer, 0, 0), pipeline_mode=pl.Buffered(1)),
            pl.BlockSpec((tm, d), lambda i: (i, 0)),
        ],
        out_specs=pl.BlockSpec((tm, d), lambda i: (i, 0)),
        out_shape=jax.ShapeDtypeStruct((m, d), _F32),
        compiler_params=pltpu.CompilerParams(
            dimension_semantics=("arbitrary",), vmem_limit_bytes=V7X_VMEM_LIMIT_BYTES),
    )(a, w, h)


def _t5_bucket(rel):
    half = N_BUCKETS // 2
    max_exact = half // 2
    side = jnp.where(rel > 0, half, 0)
    n = jnp.abs(rel)
    nf = jnp.maximum(n, 1).astype(_F32)
    large = max_exact + (jnp.log(nf / max_exact) / math.log(MAX_DISTANCE / max_exact)
                         * (half - max_exact)).astype(jnp.int32)
    large = jnp.minimum(large, half - 1)
    return side + jnp.where(n < max_exact, n, large)


def _bias_codes():
    qi = jnp.arange(BLOCK, dtype=jnp.int32)[:, None]
    col = jnp.arange(3 * BLOCK, dtype=jnp.int32)[None, :]
    rel_band = col - BLOCK - qi
    band = jnp.where(jnp.abs(rel_band) <= WINDOW, _t5_bucket(rel_band), CODE_MASKED)
    tcol = jnp.arange(BLOCK, dtype=jnp.int32)[None, :]
    tails = []
    for n_rep in range(3):
        qpos = n_rep * BLOCK + qi
        codes = jnp.where(tcol < N_META, _t5_bucket(LEAD + tcol - qpos), CODE_MASKED)
        tails.append(jnp.where(tcol == N_META, CODE_SINK, codes))
    return band.astype(jnp.int32), jnp.stack(tails).astype(jnp.int32)


def kernel(x, meta_tokens, rel_bias_table, norm_mix, norm_ffn, conv_in_w, conv_dw, conv_out_w,
           attn_qkv, attn_q_gain, attn_k_gain, attn_sink, attn_o, ffn_up, ffn_dw, ffn_dw_b, ffn_down):
    bsz, seq, d = x.shape
    tp = seq + BLOCK
    nb = tp // BLOCK
    tm = tp // FFN_ROW_TILES_PER_SLAB
    assert tm * FFN_ROW_TILES_PER_SLAB == tp and d == D_MODEL and seq % BLOCK == 0 and nb >= 4

    table = jnp.concatenate([
        rel_bias_table.T.astype(_F32), jnp.full((N_HEADS, 1), NEG_INF, _F32),
        attn_sink[0].astype(_F32)[:, None]], axis=1)
    bias, (in_w, out_w) = _attn_bias(*_bias_codes(), table, [(conv_in_w, 0), (conv_out_w, 0)])

    h, (up0, down0, qkv_w, o_w) = _mixer_from_tokens(
        x, meta_tokens.astype(x.dtype), norm_mix[0], in_w, conv_dw, out_w, 0,
        tp // MIXER_ROW_TILES_PER_SLAB, [(ffn_up, 0), (ffn_down, 0), (attn_qkv, 0), (attn_o, 0)])
    h = _conv_ffn(h, norm_ffn[0], up0, ffn_dw, ffn_dw_b, down0, 0, 0, tm, tp)

    q_scale = HEAD_DIM ** -0.5 * LOG2_E
    head_gain_cols = jnp.concatenate([
        jnp.tile(attn_q_gain[0] * q_scale, N_HEADS), jnp.tile(attn_k_gain[0], N_KV_HEADS),
        jnp.ones((KV_DIM,), _F32)]).reshape(1, QKV_DIM)
    h2 = h.reshape(bsz * tp, d)
    qkv = _qkv_proj(h2, norm_mix[1], qkv_w, 0, head_gain_cols, tp // QKV_ROW_TILES_PER_SLAB)
    o, (up1, down1) = _attention(qkv.reshape(bsz, tp, QKV_DIM), bias, nb, [(ffn_up, 1), (ffn_down, 1)])
    h2 = _out_proj(o.reshape(bsz * tp, d), o_w, 0, h2, tp // OUT_PROJ_ROW_TILES_PER_SLAB)
    return _conv_ffn(h2.reshape(bsz, tp, d), norm_ffn[1], up1, ffn_dw, ffn_dw_b, down1, 0, 1, tm, seq)
```

```python
import functools
import math

import jax
import jax.numpy as jnp
from jax import lax
from jax.experimental import pallas as pl
from jax.experimental.pallas import tpu as pltpu

D_MODEL = 2048
N_META = 16
N_HEADS = 16
N_KV_HEADS = 4
HEAD_DIM = D_MODEL // N_HEADS
GROUP = N_HEADS // N_KV_HEADS
KV_DIM = N_KV_HEADS * HEAD_DIM
QKV_DIM = D_MODEL + 2 * KV_DIM
WINDOW = 128
BLOCK = 128
N_BUCKETS = 32
MAX_DISTANCE = 128
EPS = 1e-6
LEAD = BLOCK - N_META

F32_SUBLANES = 8
BF16_SUBLANES = 16
HALO = BF16_SUBLANES
NORM_ROWS = 32
FFN_ROW_TILES_PER_SLAB = 4
MIXER_ROW_TILES_PER_SLAB = 4
QKV_ROW_TILES_PER_SLAB = 4
OUT_PROJ_ROW_TILES_PER_SLAB = 8
MIXER_COL_TILE = 512
FFN_COL_TILE = 512
QKV_COL_TILE = 512
KEYS = 4 * BLOCK
CODE_MASKED = N_BUCKETS
CODE_SINK = N_BUCKETS + 1
N_BIAS_VARIANTS = 4
NEG_INF = -1e30
LOG2_E = math.log2(math.e)
V7X_VMEM_LIMIT_BYTES = 56 * 1024 * 1024

_BF16 = jnp.bfloat16
_F32 = jnp.float32


def _dot(a, b):
    return jnp.dot(a, b, preferred_element_type=_F32)


def _rmsnorm_rows(v, gain=None):
    y = v * lax.rsqrt(jnp.mean(v * v, axis=-1, keepdims=True) + EPS)
    return y if gain is None else y * gain


def _norm_rows_into(xext_ref, row0, src_ref, n_rows):
    for r in range(0, n_rows, NORM_ROWS):
        rows = min(NORM_ROWS, n_rows - r)
        xext_ref[row0 + r:row0 + r + rows, :] = (
            _rmsnorm_rows(src_ref[r:r + rows, :]).astype(_BF16))


def _gelu_exact(v):
    return 0.5 * v * (1.0 + lax.erf(v * math.sqrt(0.5)))


def _row_tile_index(i, tiles):
    return lax.div(i, tiles), lax.rem(i, tiles)


def _tile_copy(op, hbm, acc_ref, sems, tm, tiles, tail_rows, to_hbm, t):
    b, q = _row_tile_index(t, tiles)
    slot = lax.rem(t, 2)

    def copy(rows):
        hbm_rows = hbm.at[b, pl.ds(q * tm, rows), :]
        acc_rows = acc_ref.at[slot, pl.ds(0, rows), :]
        src, dst = (acc_rows, hbm_rows) if to_hbm else (hbm_rows, acc_rows)
        op(pltpu.make_async_copy(src, dst, sems.at[slot]))

    if tail_rows == tm:
        copy(tm)
    else:
        pl.when(q < tiles - 1)(lambda: copy(tm))
        pl.when(q == tiles - 1)(lambda: copy(tail_rows))


def _stream_row_tiles(load, store, body):
    i, j = pl.program_id(0), pl.program_id(1)
    n_i, n_j = pl.num_programs(0), pl.num_programs(1)
    start, wait = (lambda c: c.start()), (lambda c: c.wait())

    pl.when(jnp.logical_and(i == 0, j == 0))(lambda: load(start, t=i))
    pl.when(j == 0)(lambda: load(wait, t=i))
    pl.when(jnp.logical_and(j == 1, i >= 1))(lambda: store(wait, t=i - 1))
    pl.when(jnp.logical_and(j == 1, i + 1 < n_i))(lambda: load(start, t=i + 1))
    body()
    pl.when(j == n_j - 1)(lambda: store(start, t=i))
    pl.when(jnp.logical_and(j == n_j - 1, i == n_i - 1))(lambda: store(wait, t=i))


def _fill_from_slab(hprev_ref, acc, hnext_ref, xext_ref, tm):
    xext_ref[0:HALO, :] = _rmsnorm_rows(hprev_ref[0]).astype(_BF16)
    _norm_rows_into(xext_ref, HALO, acc, tm)
    xext_ref[HALO + tm:, :] = _rmsnorm_rows(hnext_ref[0]).astype(_BF16)


def _fill_compact_from_slab(hprev_ref, acc, hnext_ref, xext_ref, tm, tok_rows):
    row = HALO + tok_rows
    xext_ref[0:HALO, :] = _rmsnorm_rows(hprev_ref[0]).astype(_BF16)
    _norm_rows_into(xext_ref, HALO, acc, tok_rows)
    xext_ref[row:row + HALO, :] = jnp.zeros((HALO, D_MODEL), _BF16)
    xext_ref[row + HALO:row + HALO + N_META, :] = _rmsnorm_rows(acc[tm - N_META:tm, :]).astype(_BF16)
    xext_ref[row + HALO + N_META:row + 2 * HALO + N_META, :] = (
        _rmsnorm_rows(hnext_ref[0]).astype(_BF16))


def _fill_from_tokens(xprev_ref, acc, xnext_ref, meta_ref, xext_ref, tm, tiles, tail_tokens):
    q = lax.rem(pl.program_id(0), tiles)
    meta = meta_ref[...]
    meta_n = _rmsnorm_rows(meta).astype(_BF16)

    @pl.when(q == 0)
    def _():
        xext_ref[0:HALO, :] = meta_n

    @pl.when(q > 0)
    def _():
        xext_ref[0:HALO, :] = _rmsnorm_rows(xprev_ref[0]).astype(_BF16)

    @pl.when(q < tiles - 1)
    def _():
        _norm_rows_into(xext_ref, HALO, acc, tm)
        xext_ref[HALO + tm:, :] = _rmsnorm_rows(xnext_ref[0]).astype(_BF16)

    @pl.when(q == tiles - 1)
    def _():
        row = HALO + tail_tokens
        _norm_rows_into(xext_ref, HALO, acc, tail_tokens)
        xext_ref[row:row + HALO, :] = jnp.zeros((HALO, D_MODEL), _BF16)
        xext_ref[row + HALO:row + HALO + N_META, :] = meta_n
        xext_ref[row + HALO + N_META:row + 2 * HALO + N_META, :] = (
            _rmsnorm_rows(xnext_ref[0]).astype(_BF16))
        acc[tail_tokens:tail_tokens + LEAD, :] = jnp.zeros((LEAD, D_MODEL), _F32)
        acc[tail_tokens + LEAD:tm, :] = meta


class _CastPlan:
    def __init__(self, w, layer, row_scale, n_steps, step_of_grid):
        _, rows, cols = w.shape
        n_blocks = next(nb for nb in range(min(n_steps, rows // BF16_SUBLANES), 0, -1)
                        if rows % nb == 0 and (rows // nb) % BF16_SUBLANES == 0)
        block_rows = rows // n_blocks

        def block(*grid_idx):
            return lax.div(step_of_grid(*grid_idx) * n_blocks, n_steps)

        self.operand = w
        self.scale = None if row_scale is None else row_scale.astype(_F32).reshape(rows, 1)
        self.scale_spec = pl.BlockSpec((block_rows, 1), lambda *g: (block(*g), 0))
        self.in_spec = pl.BlockSpec((None, block_rows, cols), lambda *g: (layer, block(*g), 0))
        self.out_spec = pl.BlockSpec((None, block_rows, cols), lambda *g: (0, block(*g), 0))
        self.out_shape = jax.ShapeDtypeStruct((1, rows, cols), _BF16)


def _cast_operands(casts):
    scaled = [c for c in casts if c.scale is not None]
    return ([c.in_spec for c in casts] + [c.scale_spec for c in scaled],
            [c.operand for c in casts] + [c.scale for c in scaled])


def _split_cast_refs(rest, scaled):
    n, n_scaled = len(scaled), sum(scaled)
    srcs, scales, out = rest[:n], rest[n:n + n_scaled], rest[n + n_scaled]
    dsts, scratch = rest[n + n_scaled + 1:2 * n + n_scaled + 1], rest[2 * n + n_scaled + 1:]
    return srcs, scales, out, dsts, scratch


def _run_casts(srcs, scales, dsts, scaled):
    scales = iter(scales)
    for src, dst, is_scaled in zip(srcs, dsts, scaled):
        w = src[...]
        if is_scaled:
            w = w * next(scales)[...]
        dst[...] = w.astype(_BF16)


def _conv3(v_ext, w, tm):
    return (v_ext[HALO - 1:HALO - 1 + tm] * w[0:1]
            + v_ext[HALO:HALO + tm] * w[1:2]
            + v_ext[HALO + 1:HALO + 1 + tm] * w[2:3])


def _mixer_kernel(tm, tiles, tail_tokens, scaled_casts, xprev_ref, x_hbm, xnext_ref, meta_ref, wb_ref,
                  wc_ref, wh_ref, cw_ref, wo_ref, *rest):
    cast_srcs, cast_scales, out_hbm, cast_dsts, scratch = _split_cast_refs(rest, scaled_casts)
    xext_ref, acc_ref, in_sems, out_sems = scratch
    acc = acc_ref.at[lax.rem(pl.program_id(0), 2)]
    last_tile = lax.rem(pl.program_id(0), tiles) == tiles - 1
    compact_rows = tail_tokens + HALO + N_META
    _run_casts(cast_srcs, cast_scales, cast_dsts, scaled_casts)

    def add_all(r):
        acc[...] += r

    def add_compact(r):
        acc[0:tail_tokens, :] += r[0:tail_tokens]
        acc[tm - N_META:tm, :] += r[tail_tokens + HALO:compact_rows]

    def step(m, add):
        x = xext_ref[0:m + 2 * HALO, :]
        ch = _conv3(_dot(x, wc_ref[...]) * _dot(x, wh_ref[...]), cw_ref[...], m)
        y = _dot(xext_ref[HALO:HALO + m, :], wb_ref[...]) * ch
        add(_dot(y.astype(_BF16), wo_ref[...]))

    def body():
        @pl.when(pl.program_id(1) == 0)
        def _():
            _fill_from_tokens(xprev_ref, acc, xnext_ref, meta_ref, xext_ref, tm, tiles, tail_tokens)

        pl.when(jnp.logical_not(last_tile))(functools.partial(step, tm, add_all))
        pl.when(last_tile)(functools.partial(step, compact_rows, add_compact))

    _stream_row_tiles(
        functools.partial(_tile_copy, hbm=x_hbm, acc_ref=acc_ref, sems=in_sems, tm=tm, tiles=tiles,
                          tail_rows=tail_tokens, to_hbm=False),
        functools.partial(_tile_copy, hbm=out_hbm, acc_ref=acc_ref, sems=out_sems, tm=tm, tiles=tiles,
                          tail_rows=tm, to_hbm=True),
        body)


def _ffn_kernel(tm, tiles, out_tail_rows, n_chunks, tok_rows, hprev_ref, h_hbm, hnext_ref,
                wg_ref, wua_ref, wub_ref, cw_ref, cb_ref, wd_ref, out_hbm, xext_ref, acc_ref, in_sems,
                out_sems):
    c = FFN_COL_TILE
    i, j = pl.program_id(0), pl.program_id(1)
    acc = acc_ref.at[lax.rem(i, 2)]
    last_tile = lax.rem(i, tiles) == tiles - 1
    n_pairs, odd = n_chunks // 2, n_chunks % 2
    compact_rows = tok_rows + HALO + N_META

    def add_all(r):
        acc[...] += r

    def add_compact(r):
        acc[0:tok_rows, :] += r[0:tok_rows]
        acc[tm - N_META:tm, :] += r[tok_rows + HALO:compact_rows]

    def steps(m, add):
        def act(g_ext, u, lo):
            g = _conv3(g_ext, cw_ref[:, lo:lo + c], m) + cb_ref[:, lo:lo + c]
            return (_gelu_exact(g) * u).astype(_BF16)

        def pair_step():
            x, xc = xext_ref[0:m + 2 * HALO, :], xext_ref[HALO:HALO + m, :]
            g2 = _dot(x, wg_ref[...])
            ua, ub = _dot(xc, wua_ref[...]), _dot(xc, wub_ref[...])
            a = jnp.concatenate([act(g2[:, :c], ua, 0), act(g2[:, c:], ub, c)], axis=1)
            add(_dot(a, wd_ref[...]))

        def single_step():
            g = _dot(xext_ref[0:m + 2 * HALO, :], wg_ref[:, 0:c])
            ua = _dot(xext_ref[HALO:HALO + m, :], wua_ref[...])
            add(_dot(act(g, ua, 0), wd_ref[0:c, :]))

        return pair_step, single_step

    def body():
        @pl.when(jnp.logical_and(j == 0, jnp.logical_not(last_tile)))
        def _():
            _fill_from_slab(hprev_ref, acc, hnext_ref, xext_ref, tm)

        @pl.when(jnp.logical_and(j == 0, last_tile))
        def _():
            _fill_compact_from_slab(hprev_ref, acc, hnext_ref, xext_ref, tm, tok_rows)

        for this_tile, (pair_step, single_step) in (
                (jnp.logical_not(last_tile), steps(tm, add_all)),
                (last_tile, steps(compact_rows, add_compact))):
            pl.when(jnp.logical_and(this_tile, j < n_pairs))(pair_step)
            if odd:
                pl.when(jnp.logical_and(this_tile, j == n_pairs))(single_step)

    _stream_row_tiles(
        functools.partial(_tile_copy, hbm=h_hbm, acc_ref=acc_ref, sems=in_sems, tm=tm, tiles=tiles,
                          tail_rows=tm, to_hbm=False),
        functools.partial(_tile_copy, hbm=out_hbm, acc_ref=acc_ref, sems=out_sems, tm=tm, tiles=tiles,
                          tail_rows=out_tail_rows, to_hbm=True),
        body)


def _fused_params():
    return pltpu.CompilerParams(dimension_semantics=("arbitrary", "arbitrary"),
                                vmem_limit_bytes=V7X_VMEM_LIMIT_BYTES)


def _fused_scratch(tm, d):
    return [pltpu.VMEM((tm + 2 * HALO, d), _BF16), pltpu.VMEM((2, tm, d), _F32),
            pltpu.SemaphoreType.DMA((2,)), pltpu.SemaphoreType.DMA((2,))]


def _weight_specs(layer, d, hidden, n_up, col_tile):
    n_j = hidden // col_tile
    ups = [pl.BlockSpec((None, d, col_tile),
                        functools.partial(lambda i, j, k: (layer, 0, j + k * n_j), k=k))
           for k in range(n_up)]
    down = pl.BlockSpec((None, col_tile, d), lambda i, j: (layer, j, 0))
    return ups, down


def _mixer_from_tokens(x, meta_tokens, w_in, conv_w, w_out, layer, tm, cast_weights):
    bsz, seq, d = x.shape
    tp = seq + BLOCK
    tiles = tp // tm
    tail_tokens = seq - (tiles - 1) * tm
    assert tiles * tm == tp and 0 < tail_tokens < tm and tail_tokens % HALO == 0 and tm % HALO == 0
    halo_blocks_per_tile = tm // HALO
    col_tile = MIXER_COL_TILE

    def prev_map(i, j):
        b, q = _row_tile_index(i, tiles)
        return b, jnp.maximum(q * halo_blocks_per_tile - 1, 0), 0

    def next_map(i, j):
        b, q = _row_tile_index(i, tiles)
        wrap = q == tiles - 1
        return (jnp.where(wrap, lax.rem(b + 1, bsz), b),
                jnp.where(wrap, 0, (q + 1) * halo_blocks_per_tile), 0)

    ups, down = _weight_specs(layer, d, d, 3, col_tile)
    n_j = d // col_tile
    casts = [_CastPlan(w, l, sc, bsz * tiles * n_j, lambda i, j: i * n_j + j) for w, l, sc in cast_weights]
    cast_specs, cast_operands = _cast_operands(casts)
    in_specs = [
        pl.BlockSpec((1, HALO, d), prev_map),
        pl.BlockSpec(memory_space=pl.ANY),
        pl.BlockSpec((1, HALO, d), next_map),
        pl.BlockSpec((N_META, d), lambda i, j: (0, 0)),
        *ups,
        pl.BlockSpec((None, 3, col_tile), lambda i, j: (layer, 0, j)),
        down,
        *cast_specs,
    ]
    out, *cast_out = pl.pallas_call(
        functools.partial(_mixer_kernel, tm, tiles, tail_tokens, tuple(c.scale is not None for c in casts)),
        grid=(bsz * tiles, n_j),
        in_specs=in_specs,
        out_specs=[pl.BlockSpec(memory_space=pl.ANY), *[c.out_spec for c in casts]],
        out_shape=[jax.ShapeDtypeStruct((bsz, tp, d), _F32), *[c.out_shape for c in casts]],
        scratch_shapes=_fused_scratch(tm, d),
        compiler_params=_fused_params(),
    )(x, x, x, meta_tokens, w_in, w_in, w_in, conv_w, w_out, *cast_operands)
    return out, cast_out


def _conv_ffn(h, w_up, conv_w, conv_b, w_down, layer, conv_layer, tm, out_rows):
    bsz, tp, d = h.shape
    hidden = w_down.shape[1]
    tiles = tp // tm
    out_tail_rows = out_rows - (tiles - 1) * tm
    tok_rows = tm - BLOCK
    assert 0 < tok_rows and tok_rows % NORM_ROWS == 0 and LEAD >= HALO
    assert tiles * tm == tp and tm % HALO == 0 and 0 < out_tail_rows <= tm and out_tail_rows % F32_SUBLANES == 0
    halo_blocks_per_tile = tm // HALO
    halo_blocks_per_slab = tp // HALO
    n_halo_blocks = bsz * halo_blocks_per_slab
    col_tile = FFN_COL_TILE

    def halo_map(i, j, offset):
        blk = lax.rem(i * halo_blocks_per_tile + offset + n_halo_blocks, n_halo_blocks)
        return lax.div(blk, halo_blocks_per_slab), lax.rem(blk, halo_blocks_per_slab), 0

    n_chunks = hidden // col_tile
    assert n_chunks * col_tile == hidden and w_up.shape[2] == 2 * hidden
    in_specs = [
        pl.BlockSpec((1, HALO, d), functools.partial(halo_map, offset=-1)),
        pl.BlockSpec(memory_space=pl.ANY),
        pl.BlockSpec((1, HALO, d), functools.partial(halo_map, offset=halo_blocks_per_tile)),
        pl.BlockSpec((None, d, 2 * col_tile), lambda i, j: (layer, 0, j)),
        pl.BlockSpec((None, d, col_tile), lambda i, j: (layer, 0, n_chunks + 2 * j)),
        pl.BlockSpec((None, d, col_tile),
                     lambda i, j: (layer, 0, jnp.minimum(n_chunks + 2 * j + 1, 2 * n_chunks - 1))),
        pl.BlockSpec((None, 3, 2 * col_tile), lambda i, j: (conv_layer, 0, j)),
        pl.BlockSpec((None, 1, 2 * col_tile), lambda i, j: (conv_layer, 0, j)),
        pl.BlockSpec((None, 2 * col_tile, d), lambda i, j: (layer, j, 0)),
    ]
    return pl.pallas_call(
        functools.partial(_ffn_kernel, tm, tiles, out_tail_rows, n_chunks, tok_rows),
        grid=(bsz * tiles, (n_chunks + 1) // 2),
        in_specs=in_specs,
        out_specs=pl.BlockSpec(memory_space=pl.ANY),
        out_shape=jax.ShapeDtypeStruct((bsz, out_rows, d), _F32),
        scratch_shapes=_fused_scratch(tm, d),
        compiler_params=_fused_params(),
    )(h, h, h, w_up, w_up, w_up, conv_w, conv_b.reshape(conv_b.shape[0], 1, hidden),
      w_down)


def _qkv_kernel(n_norm_cols, h_ref, w_ref, hgain_ref, out_ref, xn_ref):
    _norm_rows_into(xn_ref, 0, h_ref, h_ref.shape[0])
    for c in range(w_ref.shape[1] // QKV_COL_TILE):
        acc = _dot(xn_ref[...], w_ref[:, c * QKV_COL_TILE:(c + 1) * QKV_COL_TILE])
        for hd in range(QKV_COL_TILE // HEAD_DIM):
            lo = hd * HEAD_DIM
            cols = slice(c * QKV_COL_TILE + lo, c * QKV_COL_TILE + lo + HEAD_DIM)
            head = acc[:, lo:lo + HEAD_DIM]
            if c * QKV_COL_TILE + lo < n_norm_cols:
                head = _rmsnorm_rows(head, hgain_ref[:, cols])
            out_ref[:, cols] = head.astype(_BF16)


def _qkv_proj(h, w_qkv, layer, head_gain_cols, tm):
    m, d = h.shape
    n_out = w_qkv.shape[2]
    return pl.pallas_call(
        functools.partial(_qkv_kernel, D_MODEL + KV_DIM),
        grid=(m // tm,),
        in_specs=[
            pl.BlockSpec((tm, d), lambda i: (i, 0)),
            pl.BlockSpec((None, d, n_out), lambda i: (layer, 0, 0), pipeline_mode=pl.Buffered(1)),
            pl.BlockSpec((1, n_out), lambda i: (0, 0)),
        ],
        out_specs=pl.BlockSpec((tm, n_out), lambda i: (i, 0)),
        out_shape=jax.ShapeDtypeStruct((m, n_out), _BF16),
        scratch_shapes=[pltpu.VMEM((tm, d), _BF16)],
        compiler_params=pltpu.CompilerParams(
            dimension_semantics=("arbitrary",), vmem_limit_bytes=V7X_VMEM_LIMIT_BYTES),
    )(h, w_qkv, head_gain_cols)


def _bias_kernel(scaled_casts, band_codes_ref, tail_codes_ref, tbl_ref, *rest):
    cast_srcs, cast_scales, out_ref, cast_dsts, _ = _split_cast_refs(rest, scaled_casts)
    _run_casts(cast_srcs, cast_scales, cast_dsts, scaled_casts)
    head = pl.program_id(0)

    def lookup(codes):
        acc = jnp.zeros(codes.shape, _F32)
        for code in range(N_BUCKETS + 2):
            acc = jnp.where(codes == code, tbl_ref[head, code] * LOG2_E, acc)
        return acc

    band = lookup(band_codes_ref[...])
    tails = [lookup(tail_codes_ref[v]) for v in range(3)]
    masked = jnp.full((BLOCK, BLOCK), NEG_INF, _F32)
    for variant, valid_blocks, tail in ((0, (2,), 0), (1, (1, 2), 1), (2, (0, 1, 2), 2), (3, (0, 1), 2)):
        for blk in range(3):
            cols = slice(blk * BLOCK, (blk + 1) * BLOCK)
            out_ref[variant, 0, :, cols] = band[:, cols] if blk in valid_blocks else masked
        out_ref[variant, 0, :, 3 * BLOCK:] = tails[tail]


def _attn_bias(band_codes, tail_codes, table, cast_weights):
    casts = [_CastPlan(w, l, sc, N_HEADS, lambda h: h) for w, l, sc in cast_weights]
    cast_specs, cast_operands = _cast_operands(casts)
    bias, *cast_out = pl.pallas_call(
        functools.partial(_bias_kernel, tuple(c.scale is not None for c in casts)),
        grid=(N_HEADS,),
        in_specs=[
            pl.BlockSpec((BLOCK, 3 * BLOCK), lambda h: (0, 0)),
            pl.BlockSpec((3, BLOCK, BLOCK), lambda h: (0, 0, 0)),
            pl.BlockSpec(memory_space=pltpu.SMEM),
            *cast_specs,
        ],
        out_specs=[pl.BlockSpec((N_BIAS_VARIANTS, 1, BLOCK, KEYS), lambda h: (0, h, 0, 0)),
                   *[c.out_spec for c in casts]],
        out_shape=[jax.ShapeDtypeStruct((N_BIAS_VARIANTS, N_HEADS, BLOCK, KEYS), _F32),
                   *[c.out_shape for c in casts]],
    )(band_codes, tail_codes, table, *cast_operands)
    return bias, cast_out


def _attn_chain(q_ref, k_refs, v_refs, bias_ref, out_ref, slab, k_slabs, g, row0):
    rows = BLOCK - row0
    kv_cols = slice(g * HEAD_DIM, (g + 1) * HEAD_DIM)
    head_cols = [slice((g * GROUP + r) * HEAD_DIM, (g * GROUP + r + 1) * HEAD_DIM) for r in range(GROUP)]
    dead = jnp.zeros((BLOCK - N_META, HEAD_DIM), _BF16)
    q = jnp.concatenate([q_ref[slab, row0:, cols] for cols in head_cols], axis=0)
    k = jnp.concatenate([ref[ks, :, kv_cols] for ref, ks in zip(k_refs, k_slabs)] + [dead], axis=0)
    v = jnp.concatenate([ref[ks, :, kv_cols] for ref, ks in zip(v_refs, k_slabs)] + [dead], axis=0)
    s = lax.dot_general(q, k, (((1,), (1,)), ((), ())), preferred_element_type=_F32)
    s = s + jnp.concatenate([bias_ref[0, g * GROUP + r, row0:, :] for r in range(GROUP)], axis=0)
    e = jnp.exp2(s - jnp.max(s, axis=-1, keepdims=True))
    denom = jnp.sum(e, axis=-1, keepdims=True)
    o = _dot(e.astype(_BF16), v) / denom
    for r, cols in enumerate(head_cols):
        out_ref[slab, row0:, cols] = o[r * rows:(r + 1) * rows].astype(_BF16)


def _attn_kernel(scaled_casts, q_ref, kp_ref, kc_ref, kn_ref, km_ref, vp_ref, vc_ref, vn_ref, vm_ref, bias_ref,
                 *rest):
    cast_srcs, cast_scales, out_ref, cast_dsts, _ = _split_cast_refs(rest, scaled_casts)
    _run_casts(cast_srcs, cast_scales, cast_dsts, scaled_casts)
    n = pl.program_id(0)
    n_slabs = q_ref.shape[0]
    k_refs = (kp_ref, kc_ref, kn_ref, km_ref)
    v_refs = (vp_ref, vc_ref, vn_ref, vm_ref)

    def all_chains(first_block):
        for slab in range(n_slabs):
            if first_block:
                k_slabs = (slab, slab, (slab + 1) % n_slabs, slab)
            else:
                k_slabs = (slab, slab, slab, (slab - 1) % n_slabs)
            for g in range(N_KV_HEADS):
                _attn_chain(q_ref, k_refs, v_refs, bias_ref, out_ref, slab, k_slabs, g,
                            LEAD if first_block else 0)

    @pl.when(n == 0)
    def _():
        all_chains(True)
        out_ref[:, 0:LEAD, :] = jnp.zeros((n_slabs, LEAD, D_MODEL), _BF16)

    @pl.when(n > 0)
    def _():
        all_chains(False)


def _attention(qkv, bias, nb, cast_weights):
    bsz = qkv.shape[0]
    casts = [_CastPlan(w, l, sc, nb, lambda n: n) for w, l, sc in cast_weights]
    cast_specs, cast_operands = _cast_operands(casts)
    k_col, v_col = D_MODEL // KV_DIM, D_MODEL // KV_DIM + 1

    def slab_block(n):
        return jnp.where(n == 0, nb - 1, n - 1)

    def blk(col, shift):
        return pl.BlockSpec(
            (bsz, BLOCK, KV_DIM), lambda n: (0, slab_block(jnp.clip(n + shift, 0, nb - 1)), col))

    def meta(col):
        return pl.BlockSpec(
            (bsz, N_META, KV_DIM), lambda n: (0, (nb - 1) * (BLOCK // N_META) + LEAD // N_META, col))

    out, *cast_out = pl.pallas_call(
        functools.partial(_attn_kernel, tuple(c.scale is not None for c in casts)),
        grid=(nb,),
        in_specs=[
            pl.BlockSpec((bsz, BLOCK, D_MODEL), lambda n: (0, slab_block(n), 0)),
            blk(k_col, -1), blk(k_col, 0), blk(k_col, 1), meta(k_col),
            blk(v_col, -1), blk(v_col, 0), blk(v_col, 1), meta(v_col),
            pl.BlockSpec((1, N_HEADS, BLOCK, KEYS),
                         lambda n: (jnp.where(n == nb - 1, 3, jnp.minimum(n, 2)), 0, 0, 0)),
            *cast_specs,
        ],
        out_specs=[pl.BlockSpec((bsz, BLOCK, D_MODEL), lambda n: (0, slab_block(n), 0)),
                   *[c.out_spec for c in casts]],
        out_shape=[jax.ShapeDtypeStruct((bsz, nb * BLOCK, D_MODEL), _BF16),
                   *[c.out_shape for c in casts]],
        compiler_params=pltpu.CompilerParams(
            dimension_semantics=("arbitrary",), vmem_limit_bytes=V7X_VMEM_LIMIT_BYTES),
    )(qkv, qkv, qkv, qkv, qkv, qkv, qkv, qkv, qkv, bias, *cast_operands)
    return out, cast_out


def _out_proj_kernel(a_ref, w_ref, h_ref, out_ref):
    out_ref[...] = h_ref[...] + _dot(a_ref[...], w_ref[...])


def _out_proj(a, w, layer, h, tm):
    m, d = h.shape
    return pl.pallas_call(
        _out_proj_kernel,
        grid=(m // tm,),
        in_specs=[
            pl.BlockSpec((tm, d), lambda i: (i, 0)),
            pl.BlockSpec((None, d, d), lambda i: (layer, 0, 0), pipeline_mode=pl.Buffered(1)),
            pl.BlockSpec((tm, d), lambda i: (i, 0)),
        ],
        out_specs=pl.BlockSpec((tm, d), lambda i: (i, 0)),
        out_shape=jax.ShapeDtypeStruct((m, d), _F32),
        compiler_params=pltpu.CompilerParams(
            dimension_semantics=("arbitrary",), vmem_limit_bytes=V7X_VMEM_LIMIT_BYTES),
    )(a, w, h)


def _t5_bucket(rel):
    half = N_BUCKETS // 2
    max_exact = half // 2
    side = jnp.where(rel > 0, half, 0)
    n = jnp.abs(rel)
    nf = jnp.maximum(n, 1).astype(_F32)
    large = max_exact + (jnp.log(nf / max_exact) / math.log(MAX_DISTANCE / max_exact)
                         * (half - max_exact)).astype(jnp.int32)
    large = jnp.minimum(large, half - 1)
    return side + jnp.where(n < max_exact, n, large)


def _bias_codes():
    qi = jnp.arange(BLOCK, dtype=jnp.int32)[:, None]
    col = jnp.arange(3 * BLOCK, dtype=jnp.int32)[None, :]
    rel_band = col - BLOCK - qi
    band = jnp.where(jnp.abs(rel_band) <= WINDOW, _t5_bucket(rel_band), CODE_MASKED)
    tcol = jnp.arange(BLOCK, dtype=jnp.int32)[None, :]
    tails = []
    for n_rep in range(3):
        qpos = n_rep * BLOCK + qi
        codes = jnp.where(tcol < N_META, _t5_bucket(LEAD + tcol - qpos), CODE_MASKED)
        tails.append(jnp.where(tcol == N_META, CODE_SINK, codes))
    return band.astype(jnp.int32), jnp.stack(tails).astype(jnp.int32)


def kernel(x, meta_tokens, rel_bias_table, norm_mix, norm_ffn, conv_in_w, conv_dw, conv_out_w,
           attn_qkv, attn_q_gain, attn_k_gain, attn_sink, attn_o, ffn_up, ffn_dw, ffn_dw_b, ffn_down):
    bsz, seq, d = x.shape
    tp = seq + BLOCK
    nb = tp // BLOCK
    tm = tp // FFN_ROW_TILES_PER_SLAB
    assert tm * FFN_ROW_TILES_PER_SLAB == tp and d == D_MODEL and seq % BLOCK == 0 and nb >= 4

    table = jnp.concatenate([
        rel_bias_table.T.astype(_F32), jnp.full((N_HEADS, 1), NEG_INF, _F32),
        attn_sink[0].astype(_F32)[:, None]], axis=1)
    bias, (in_w, out_w) = _attn_bias(*_bias_codes(), table,
                                     [(conv_in_w, 0, norm_mix[0]), (conv_out_w, 0, None)])

    h, (up0, down0, qkv_w, o_w) = _mixer_from_tokens(
        x, meta_tokens.astype(x.dtype), in_w, conv_dw, out_w, 0, tp // MIXER_ROW_TILES_PER_SLAB,
        [(ffn_up, 0, norm_ffn[0]), (ffn_down, 0, None), (attn_qkv, 0, norm_mix[1]), (attn_o, 0, None)])
    h = _conv_ffn(h, up0, ffn_dw, ffn_dw_b, down0, 0, 0, tm, tp)

    q_scale = HEAD_DIM ** -0.5 * LOG2_E
    head_gain_cols = jnp.concatenate([
        jnp.tile(attn_q_gain[0] * q_scale, N_HEADS), jnp.tile(attn_k_gain[0], N_KV_HEADS),
        jnp.ones((KV_DIM,), _F32)]).reshape(1, QKV_DIM)
    h2 = h.reshape(bsz * tp, d)
    qkv = _qkv_proj(h2, qkv_w, 0, head_gain_cols, tp // QKV_ROW_TILES_PER_SLAB)
    o, (up1, down1) = _attention(qkv.reshape(bsz, tp, QKV_DIM), bias, nb,
                                 [(ffn_up, 1, norm_ffn[1]), (ffn_down, 1, None)])
    h2 = _out_proj(o.reshape(bsz * tp, d), o_w, 0, h2, tp // OUT_PROJ_ROW_TILES_PER_SLAB)
    return _conv_ffn(h2.reshape(bsz, tp, d), up1, ffn_dw, ffn_dw_b, down1, 0, 1, tm, seq)
```

```python
import functools
import math

import jax
import jax.numpy as jnp
from jax import lax
from jax.experimental import pallas as pl
from jax.experimental.pallas import tpu as pltpu

D_MODEL = 2048
N_META = 16
N_HEADS = 16
N_KV_HEADS = 4
HEAD_DIM = D_MODEL // N_HEADS
GROUP = N_HEADS // N_KV_HEADS
KV_DIM = N_KV_HEADS * HEAD_DIM
QKV_DIM = D_MODEL + 2 * KV_DIM
WINDOW = 128
BLOCK = 128
N_BUCKETS = 32
MAX_DISTANCE = 128
EPS = 1e-6
LEAD = BLOCK - N_META

F32_SUBLANES = 8
BF16_SUBLANES = 16
HALO = BF16_SUBLANES
NORM_ROWS = 32
FFN_ROW_TILES_PER_SLAB = 4
MIXER_ROW_TILES_PER_SLAB = 4
QKV_ROW_TILES_PER_SLAB = 4
MIXER_COL_TILE = 512
FFN_COL_TILE = 512
QKV_COL_TILE = 512
KEYS = 4 * BLOCK
CODE_MASKED = N_BUCKETS
CODE_SINK = N_BUCKETS + 1
N_BIAS_VARIANTS = 4
NEG_INF = -1e30
LOG2_E = math.log2(math.e)
V7X_VMEM_LIMIT_BYTES = 56 * 1024 * 1024

_BF16 = jnp.bfloat16
_F32 = jnp.float32


def _dot(a, b):
    return jnp.dot(a, b, preferred_element_type=_F32)


def _rmsnorm_rows(v, gain):
    return v * lax.rsqrt(jnp.mean(v * v, axis=-1, keepdims=True) + EPS) * gain


def _norm_rows_into(xext_ref, row0, src_ref, n_rows, gain):
    for r in range(0, n_rows, NORM_ROWS):
        rows = min(NORM_ROWS, n_rows - r)
        xext_ref[row0 + r:row0 + r + rows, :] = (
            _rmsnorm_rows(src_ref[r:r + rows, :], gain).astype(_BF16))


def _gelu_exact(v):
    return 0.5 * v * (1.0 + lax.erf(v * math.sqrt(0.5)))


def _row_tile_index(i, tiles):
    return lax.div(i, tiles), lax.rem(i, tiles)


def _tile_copy(op, hbm, acc_ref, sems, tm, tiles, tail_rows, to_hbm, t):
    b, q = _row_tile_index(t, tiles)
    slot = lax.rem(t, 2)

    def copy(rows):
        hbm_rows = hbm.at[b, pl.ds(q * tm, rows), :]
        acc_rows = acc_ref.at[slot, pl.ds(0, rows), :]
        src, dst = (acc_rows, hbm_rows) if to_hbm else (hbm_rows, acc_rows)
        op(pltpu.make_async_copy(src, dst, sems.at[slot]))

    if tail_rows == tm:
        copy(tm)
    else:
        pl.when(q < tiles - 1)(lambda: copy(tm))
        pl.when(q == tiles - 1)(lambda: copy(tail_rows))


def _stream_row_tiles(load, store, body):
    i, j = pl.program_id(0), pl.program_id(1)
    n_i, n_j = pl.num_programs(0), pl.num_programs(1)
    start, wait = (lambda c: c.start()), (lambda c: c.wait())

    pl.when(jnp.logical_and(i == 0, j == 0))(lambda: load(start, t=i))
    pl.when(j == 0)(lambda: load(wait, t=i))
    pl.when(jnp.logical_and(j == 1, i >= 1))(lambda: store(wait, t=i - 1))
    pl.when(jnp.logical_and(j == 1, i + 1 < n_i))(lambda: load(start, t=i + 1))
    body()
    pl.when(j == n_j - 1)(lambda: store(start, t=i))
    pl.when(jnp.logical_and(j == n_j - 1, i == n_i - 1))(lambda: store(wait, t=i))


def _fill_from_slab(hprev_ref, acc, hnext_ref, gain_ref, xext_ref, tm):
    gain = gain_ref[...]
    xext_ref[0:HALO, :] = _rmsnorm_rows(hprev_ref[0], gain).astype(_BF16)
    _norm_rows_into(xext_ref, HALO, acc, tm, gain)
    xext_ref[HALO + tm:, :] = _rmsnorm_rows(hnext_ref[0], gain).astype(_BF16)


def _fill_compact_from_slab(hprev_ref, acc, hnext_ref, gain_ref, xext_ref, tm, tok_rows):
    gain = gain_ref[...]
    row = HALO + tok_rows
    xext_ref[0:HALO, :] = _rmsnorm_rows(hprev_ref[0], gain).astype(_BF16)
    _norm_rows_into(xext_ref, HALO, acc, tok_rows, gain)
    xext_ref[row:row + HALO, :] = jnp.zeros((HALO, D_MODEL), _BF16)
    xext_ref[row + HALO:row + HALO + N_META, :] = (
        _rmsnorm_rows(acc[tm - N_META:tm, :], gain).astype(_BF16))
    xext_ref[row + HALO + N_META:row + 2 * HALO + N_META, :] = (
        _rmsnorm_rows(hnext_ref[0], gain).astype(_BF16))


def _fill_from_tokens(xprev_ref, acc, xnext_ref, meta_ref, gain_ref, xext_ref, tm, tiles, tail_tokens):
    q = lax.rem(pl.program_id(0), tiles)
    gain = gain_ref[...]
    meta = meta_ref[...]
    meta_n = _rmsnorm_rows(meta, gain).astype(_BF16)

    @pl.when(q == 0)
    def _():
        xext_ref[0:HALO, :] = meta_n

    @pl.when(q > 0)
    def _():
        xext_ref[0:HALO, :] = _rmsnorm_rows(xprev_ref[0], gain).astype(_BF16)

    @pl.when(q < tiles - 1)
    def _():
        _norm_rows_into(xext_ref, HALO, acc, tm, gain)
        xext_ref[HALO + tm:, :] = _rmsnorm_rows(xnext_ref[0], gain).astype(_BF16)

    @pl.when(q == tiles - 1)
    def _():
        row = HALO + tail_tokens
        _norm_rows_into(xext_ref, HALO, acc, tail_tokens, gain)
        xext_ref[row:row + HALO, :] = jnp.zeros((HALO, D_MODEL), _BF16)
        xext_ref[row + HALO:row + HALO + N_META, :] = meta_n
        xext_ref[row + HALO + N_META:row + 2 * HALO + N_META, :] = (
            _rmsnorm_rows(xnext_ref[0], gain).astype(_BF16))
        acc[tail_tokens:tail_tokens + LEAD, :] = jnp.zeros((LEAD, D_MODEL), _F32)
        acc[tail_tokens + LEAD:tm, :] = meta


class _CastPlan:
    def __init__(self, w, layer, n_steps, step_of_grid):
        _, rows, cols = w.shape
        n_blocks = next(nb for nb in range(min(n_steps, rows // BF16_SUBLANES), 0, -1)
                        if rows % nb == 0 and (rows // nb) % BF16_SUBLANES == 0)
        block_rows = rows // n_blocks

        def block(*grid_idx):
            return lax.div(step_of_grid(*grid_idx) * n_blocks, n_steps)

        self.operand = w
        self.in_spec = pl.BlockSpec((None, block_rows, cols), lambda *g: (layer, block(*g), 0))
        self.out_spec = pl.BlockSpec((None, block_rows, cols), lambda *g: (0, block(*g), 0))
        self.out_shape = jax.ShapeDtypeStruct((1, rows, cols), _BF16)


def _run_casts(refs):
    n = len(refs) // 2
    for src, dst in zip(refs[:n], refs[n:]):
        dst[...] = src[...].astype(_BF16)


def _conv3(v_ext, w, tm):
    return (v_ext[HALO - 1:HALO - 1 + tm] * w[0:1]
            + v_ext[HALO:HALO + tm] * w[1:2]
            + v_ext[HALO + 1:HALO + 1 + tm] * w[2:3])


def _mixer_kernel(tm, tiles, tail_tokens, n_casts, xprev_ref, x_hbm, xnext_ref, meta_ref, gain_ref, wb_ref,
                  wc_ref, wh_ref, cw_ref, wo_ref, *rest):
    cast_srcs, out_hbm, cast_dsts = rest[:n_casts], rest[n_casts], rest[n_casts + 1:2 * n_casts + 1]
    xext_ref, acc_ref, in_sems, out_sems = rest[2 * n_casts + 1:]
    acc = acc_ref.at[lax.rem(pl.program_id(0), 2)]
    last_tile = lax.rem(pl.program_id(0), tiles) == tiles - 1
    compact_rows = tail_tokens + HALO + N_META
    _run_casts(cast_srcs + cast_dsts)

    def add_all(r):
        acc[...] += r

    def add_compact(r):
        acc[0:tail_tokens, :] += r[0:tail_tokens]
        acc[tm - N_META:tm, :] += r[tail_tokens + HALO:compact_rows]

    def step(m, add):
        x = xext_ref[0:m + 2 * HALO, :]
        ch = _conv3(_dot(x, wc_ref[...]) * _dot(x, wh_ref[...]), cw_ref[...], m)
        y = _dot(xext_ref[HALO:HALO + m, :], wb_ref[...]) * ch
        add(_dot(y.astype(_BF16), wo_ref[...]))

    def body():
        @pl.when(pl.program_id(1) == 0)
        def _():
            _fill_from_tokens(xprev_ref, acc, xnext_ref, meta_ref, gain_ref, xext_ref, tm, tiles,
                              tail_tokens)

        pl.when(jnp.logical_not(last_tile))(functools.partial(step, tm, add_all))
        pl.when(last_tile)(functools.partial(step, compact_rows, add_compact))

    _stream_row_tiles(
        functools.partial(_tile_copy, hbm=x_hbm, acc_ref=acc_ref, sems=in_sems, tm=tm, tiles=tiles,
                          tail_rows=tail_tokens, to_hbm=False),
        functools.partial(_tile_copy, hbm=out_hbm, acc_ref=acc_ref, sems=out_sems, tm=tm, tiles=tiles,
                          tail_rows=tm, to_hbm=True),
        body)


def _ffn_kernel(tm, tiles, out_tail_rows, n_chunks, tok_rows, hprev_ref, h_hbm, hnext_ref, gain_ref,
                wg_ref, wua_ref, wub_ref, cw_ref, cb_ref, wd_ref, out_hbm, xext_ref, acc_ref, in_sems,
                out_sems):
    c = FFN_COL_TILE
    i, j = pl.program_id(0), pl.program_id(1)
    acc = acc_ref.at[lax.rem(i, 2)]
    last_tile = lax.rem(i, tiles) == tiles - 1
    n_pairs, odd = n_chunks // 2, n_chunks % 2
    compact_rows = tok_rows + HALO + N_META

    def add_all(r):
        acc[...] += r

    def add_compact(r):
        acc[0:tok_rows, :] += r[0:tok_rows]
        acc[tm - N_META:tm, :] += r[tok_rows + HALO:compact_rows]

    def steps(m, add):
        def act(g_ext, u, lo):
            g = _conv3(g_ext, cw_ref[:, lo:lo + c], m) + cb_ref[:, lo:lo + c]
            return (_gelu_exact(g) * u).astype(_BF16)

        def pair_step():
            x, xc = xext_ref[0:m + 2 * HALO, :], xext_ref[HALO:HALO + m, :]
            g2 = _dot(x, wg_ref[...])
            ua, ub = _dot(xc, wua_ref[...]), _dot(xc, wub_ref[...])
            a = jnp.concatenate([act(g2[:, :c], ua, 0), act(g2[:, c:], ub, c)], axis=1)
            add(_dot(a, wd_ref[...]))

        def single_step():
            g = _dot(xext_ref[0:m + 2 * HALO, :], wg_ref[:, 0:c])
            ua = _dot(xext_ref[HALO:HALO + m, :], wua_ref[...])
            add(_dot(act(g, ua, 0), wd_ref[0:c, :]))

        return pair_step, single_step

    def body():
        @pl.when(jnp.logical_and(j == 0, jnp.logical_not(last_tile)))
        def _():
            _fill_from_slab(hprev_ref, acc, hnext_ref, gain_ref, xext_ref, tm)

        @pl.when(jnp.logical_and(j == 0, last_tile))
        def _():
            _fill_compact_from_slab(hprev_ref, acc, hnext_ref, gain_ref, xext_ref, tm, tok_rows)

        for this_tile, (pair_step, single_step) in (
                (jnp.logical_not(last_tile), steps(tm, add_all)),
                (last_tile, steps(compact_rows, add_compact))):
            pl.when(jnp.logical_and(this_tile, j < n_pairs))(pair_step)
            if odd:
                pl.when(jnp.logical_and(this_tile, j == n_pairs))(single_step)

    _stream_row_tiles(
        functools.partial(_tile_copy, hbm=h_hbm, acc_ref=acc_ref, sems=in_sems, tm=tm, tiles=tiles,
                          tail_rows=tm, to_hbm=False),
        functools.partial(_tile_copy, hbm=out_hbm, acc_ref=acc_ref, sems=out_sems, tm=tm, tiles=tiles,
                          tail_rows=out_tail_rows, to_hbm=True),
        body)


def _fused_params():
    return pltpu.CompilerParams(dimension_semantics=("arbitrary", "arbitrary"),
                                vmem_limit_bytes=V7X_VMEM_LIMIT_BYTES)


def _fused_scratch(tm, d):
    return [pltpu.VMEM((tm + 2 * HALO, d), _BF16), pltpu.VMEM((2, tm, d), _F32),
            pltpu.SemaphoreType.DMA((2,)), pltpu.SemaphoreType.DMA((2,))]


def _weight_specs(layer, d, hidden, n_up, col_tile):
    n_j = hidden // col_tile
    ups = [pl.BlockSpec((None, d, col_tile),
                        functools.partial(lambda i, j, k: (layer, 0, j + k * n_j), k=k))
           for k in range(n_up)]
    down = pl.BlockSpec((None, col_tile, d), lambda i, j: (layer, j, 0))
    return ups, down


def _mixer_from_tokens(x, meta_tokens, gain, w_in, conv_w, w_out, layer, tm, cast_weights):
    bsz, seq, d = x.shape
    tp = seq + BLOCK
    tiles = tp // tm
    tail_tokens = seq - (tiles - 1) * tm
    assert tiles * tm == tp and 0 < tail_tokens < tm and tail_tokens % HALO == 0 and tm % HALO == 0
    halo_blocks_per_tile = tm // HALO
    col_tile = MIXER_COL_TILE

    def prev_map(i, j):
        b, q = _row_tile_index(i, tiles)
        return b, jnp.maximum(q * halo_blocks_per_tile - 1, 0), 0

    def next_map(i, j):
        b, q = _row_tile_index(i, tiles)
        wrap = q == tiles - 1
        return (jnp.where(wrap, lax.rem(b + 1, bsz), b),
                jnp.where(wrap, 0, (q + 1) * halo_blocks_per_tile), 0)

    ups, down = _weight_specs(layer, d, d, 3, col_tile)
    n_j = d // col_tile
    casts = [_CastPlan(w, l, bsz * tiles * n_j, lambda i, j: i * n_j + j) for w, l in cast_weights]
    in_specs = [
        pl.BlockSpec((1, HALO, d), prev_map),
        pl.BlockSpec(memory_space=pl.ANY),
        pl.BlockSpec((1, HALO, d), next_map),
        pl.BlockSpec((N_META, d), lambda i, j: (0, 0)),
        pl.BlockSpec((1, d), lambda i, j: (0, 0)),
        *ups,
        pl.BlockSpec((None, 3, col_tile), lambda i, j: (layer, 0, j)),
        down,
        *[c.in_spec for c in casts],
    ]
    out, *cast_out = pl.pallas_call(
        functools.partial(_mixer_kernel, tm, tiles, tail_tokens, len(casts)),
        grid=(bsz * tiles, n_j),
        in_specs=in_specs,
        out_specs=[pl.BlockSpec(memory_space=pl.ANY), *[c.out_spec for c in casts]],
        out_shape=[jax.ShapeDtypeStruct((bsz, tp, d), _F32), *[c.out_shape for c in casts]],
        scratch_shapes=_fused_scratch(tm, d),
        compiler_params=_fused_params(),
    )(x, x, x, meta_tokens, gain.reshape(1, d), w_in, w_in, w_in, conv_w, w_out,
      *[c.operand for c in casts])
    return out, cast_out


def _conv_ffn(h, gain, w_up, conv_w, conv_b, w_down, layer, conv_layer, tm, out_rows):
    bsz, tp, d = h.shape
    hidden = w_down.shape[1]
    tiles = tp // tm
    out_tail_rows = out_rows - (tiles - 1) * tm
    tok_rows = tm - BLOCK
    assert 0 < tok_rows and tok_rows % NORM_ROWS == 0 and LEAD >= HALO
    assert tiles * tm == tp and tm % HALO == 0 and 0 < out_tail_rows <= tm and out_tail_rows % F32_SUBLANES == 0
    halo_blocks_per_tile = tm // HALO
    halo_blocks_per_slab = tp // HALO
    n_halo_blocks = bsz * halo_blocks_per_slab
    col_tile = FFN_COL_TILE

    def halo_map(i, j, offset):
        blk = lax.rem(i * halo_blocks_per_tile + offset + n_halo_blocks, n_halo_blocks)
        return lax.div(blk, halo_blocks_per_slab), lax.rem(blk, halo_blocks_per_slab), 0

    n_chunks = hidden // col_tile
    assert n_chunks * col_tile == hidden and w_up.shape[2] == 2 * hidden
    in_specs = [
        pl.BlockSpec((1, HALO, d), functools.partial(halo_map, offset=-1)),
        pl.BlockSpec(memory_space=pl.ANY),
        pl.BlockSpec((1, HALO, d), functools.partial(halo_map, offset=halo_blocks_per_tile)),
        pl.BlockSpec((1, d), lambda i, j: (0, 0)),
        pl.BlockSpec((None, d, 2 * col_tile), lambda i, j: (layer, 0, j)),
        pl.BlockSpec((None, d, col_tile), lambda i, j: (layer, 0, n_chunks + 2 * j)),
        pl.BlockSpec((None, d, col_tile),
                     lambda i, j: (layer, 0, jnp.minimum(n_chunks + 2 * j + 1, 2 * n_chunks - 1))),
        pl.BlockSpec((None, 3, 2 * col_tile), lambda i, j: (conv_layer, 0, j)),
        pl.BlockSpec((None, 1, 2 * col_tile), lambda i, j: (conv_layer, 0, j)),
        pl.BlockSpec((None, 2 * col_tile, d), lambda i, j: (layer, j, 0)),
    ]
    return pl.pallas_call(
        functools.partial(_ffn_kernel, tm, tiles, out_tail_rows, n_chunks, tok_rows),
        grid=(bsz * tiles, (n_chunks + 1) // 2),
        in_specs=in_specs,
        out_specs=pl.BlockSpec(memory_space=pl.ANY),
        out_shape=jax.ShapeDtypeStruct((bsz, out_rows, d), _F32),
        scratch_shapes=_fused_scratch(tm, d),
        compiler_params=_fused_params(),
    )(h, h, h, gain.reshape(1, d), w_up, w_up, w_up, conv_w, conv_b.reshape(conv_b.shape[0], 1, hidden),
      w_down)


def _qkv_kernel(n_norm_cols, h_ref, gain_ref, w_ref, hgain_ref, out_ref, xn_ref):
    _norm_rows_into(xn_ref, 0, h_ref, h_ref.shape[0], gain_ref[...])
    for c in range(w_ref.shape[1] // QKV_COL_TILE):
        acc = _dot(xn_ref[...], w_ref[:, c * QKV_COL_TILE:(c + 1) * QKV_COL_TILE])
        for hd in range(QKV_COL_TILE // HEAD_DIM):
            lo = hd * HEAD_DIM
            cols = slice(c * QKV_COL_TILE + lo, c * QKV_COL_TILE + lo + HEAD_DIM)
            head = acc[:, lo:lo + HEAD_DIM]
            if c * QKV_COL_TILE + lo < n_norm_cols:
                head = _rmsnorm_rows(head, hgain_ref[:, cols])
            out_ref[:, cols] = head.astype(_BF16)


def _qkv_proj(h, gain, w_qkv, layer, head_gain_cols, tm):
    m, d = h.shape
    n_out = w_qkv.shape[2]
    return pl.pallas_call(
        functools.partial(_qkv_kernel, D_MODEL + KV_DIM),
        grid=(m // tm,),
        in_specs=[
            pl.BlockSpec((tm, d), lambda i: (i, 0)),
            pl.BlockSpec((1, d), lambda i: (0, 0)),
            pl.BlockSpec((None, d, n_out), lambda i: (layer, 0, 0), pipeline_mode=pl.Buffered(1)),
            pl.BlockSpec((1, n_out), lambda i: (0, 0)),
        ],
        out_specs=pl.BlockSpec((tm, n_out), lambda i: (i, 0)),
        out_shape=jax.ShapeDtypeStruct((m, n_out), _BF16),
        scratch_shapes=[pltpu.VMEM((tm, d), _BF16)],
        compiler_params=pltpu.CompilerParams(
            dimension_semantics=("arbitrary",), vmem_limit_bytes=V7X_VMEM_LIMIT_BYTES),
    )(h, gain.reshape(1, d), w_qkv, head_gain_cols)


def _bias_kernel(n_casts, band_codes_ref, tail_codes_ref, tbl_ref, *rest):
    out_ref = rest[n_casts]
    _run_casts(rest[:n_casts] + rest[n_casts + 1:])
    head = pl.program_id(0)

    def lookup(codes):
        acc = jnp.zeros(codes.shape, _F32)
        for code in range(N_BUCKETS + 2):
            acc = jnp.where(codes == code, tbl_ref[head, code] * LOG2_E, acc)
        return acc

    band = lookup(band_codes_ref[...])
    tails = [lookup(tail_codes_ref[v]) for v in range(3)]
    masked = jnp.full((BLOCK, BLOCK), NEG_INF, _F32)
    for variant, valid_blocks, tail in ((0, (2,), 0), (1, (1, 2), 1), (2, (0, 1, 2), 2), (3, (0, 1), 2)):
        for blk in range(3):
            cols = slice(blk * BLOCK, (blk + 1) * BLOCK)
            out_ref[variant, 0, :, cols] = band[:, cols] if blk in valid_blocks else masked
        out_ref[variant, 0, :, 3 * BLOCK:] = tails[tail]


def _attn_bias(band_codes, tail_codes, table, cast_weights):
    casts = [_CastPlan(w, l, N_HEADS, lambda h: h) for w, l in cast_weights]
    bias, *cast_out = pl.pallas_call(
        functools.partial(_bias_kernel, len(casts)),
        grid=(N_HEADS,),
        in_specs=[
            pl.BlockSpec((BLOCK, 3 * BLOCK), lambda h: (0, 0)),
            pl.BlockSpec((3, BLOCK, BLOCK), lambda h: (0, 0, 0)),
            pl.BlockSpec(memory_space=pltpu.SMEM),
            *[c.in_spec for c in casts],
        ],
        out_specs=[pl.BlockSpec((N_BIAS_VARIANTS, 1, BLOCK, KEYS), lambda h: (0, h, 0, 0)),
                   *[c.out_spec for c in casts]],
        out_shape=[jax.ShapeDtypeStruct((N_BIAS_VARIANTS, N_HEADS, BLOCK, KEYS), _F32),
                   *[c.out_shape for c in casts]],
    )(band_codes, tail_codes, table, *[c.operand for c in casts])
    return bias, cast_out


def _attn_chain(q_ref, k_refs, v_refs, bias_ref, out_ref, slab, k_slabs, g, row0):
    rows = BLOCK - row0
    kv_cols = slice(g * HEAD_DIM, (g + 1) * HEAD_DIM)
    head_cols = [slice((g * GROUP + r) * HEAD_DIM, (g * GROUP + r + 1) * HEAD_DIM) for r in range(GROUP)]
    dead = jnp.zeros((BLOCK - N_META, HEAD_DIM), _BF16)
    q = jnp.concatenate([q_ref[slab, row0:, cols] for cols in head_cols], axis=0)
    k = jnp.concatenate([ref[ks, :, kv_cols] for ref, ks in zip(k_refs, k_slabs)] + [dead], axis=0)
    v = jnp.concatenate([ref[ks, :, kv_cols] for ref, ks in zip(v_refs, k_slabs)] + [dead], axis=0)
    s = lax.dot_general(q, k, (((1,), (1,)), ((), ())), preferred_element_type=_F32)
    s = s + jnp.concatenate([bias_ref[0, g * GROUP + r, row0:, :] for r in range(GROUP)], axis=0)
    e = jnp.exp2(s - jnp.max(s, axis=-1, keepdims=True))
    denom = jnp.sum(e, axis=-1, keepdims=True)
    o = _dot(e.astype(_BF16), v) / denom
    for r, cols in enumerate(head_cols):
        out_ref[slab, row0:, cols] = o[r * rows:(r + 1) * rows].astype(_BF16)


def _attn_kernel(n_casts, nb, q_ref, kp_ref, kc_ref, kn_ref, km_ref, vp_ref, vc_ref, vn_ref, vm_ref,
                 bias_ref, w_ref, h_hbm, *rest):
    out_hbm = rest[n_casts]
    a0_ref, a1_ref, acc_ref, in_sems, out_sems = rest[-5:]
    a_refs = (a0_ref, a1_ref)
    _run_casts(rest[:n_casts] + rest[n_casts + 1:-5])
    n = pl.program_id(0)
    n_slabs = q_ref.shape[0]
    k_refs = (kp_ref, kc_ref, kn_ref, km_ref)
    v_refs = (vp_ref, vc_ref, vn_ref, vm_ref)

    def block_copy(hbm, sems, to_hbm, p, slot):
        row0 = pl.multiple_of(jnp.where(p == 0, nb - 1, p - 1) * BLOCK, BLOCK)
        window = hbm.at[:, pl.ds(row0, BLOCK), :]
        src, dst = (acc_ref.at[slot], window) if to_hbm else (window, acc_ref.at[slot])
        return pltpu.make_async_copy(src, dst, sems.at[slot])

    load = functools.partial(block_copy, h_hbm, in_sems, False)
    store = functools.partial(block_copy, out_hbm, out_sems, True)

    def chains(first_block, a_ref, slabs):
        for slab in slabs:
            if first_block:
                k_slabs = (slab, slab, (slab + 1) % n_slabs, slab)
            else:
                k_slabs = (slab, slab, slab, (slab - 1) % n_slabs)
            for g in range(N_KV_HEADS):
                _attn_chain(q_ref, k_refs, v_refs, bias_ref, a_ref, slab, k_slabs, g,
                            LEAD if first_block else 0)

    def project(slot):
        a = a_refs[slot][...].reshape(n_slabs * BLOCK, D_MODEL)
        acc_ref[slot] += _dot(a, w_ref[...]).reshape(n_slabs, BLOCK, D_MODEL)
        store(n - 1, slot).start()

    @pl.when(n == 0)
    def _():
        load(n, 0).start()
        chains(True, a0_ref, range(n_slabs))
        a0_ref[:, 0:LEAD, :] = jnp.zeros((n_slabs, LEAD, D_MODEL), _BF16)

    def attend_and_project(parity, store_pending):
        slot = 1 - parity
        load(n - 1, slot).wait()
        chains(False, a_refs[parity], range(n_slabs // 2))
        if store_pending:
            store(n - 2, parity).wait()
        load(n, parity).start()
        chains(False, a_refs[parity], range(n_slabs // 2, n_slabs))
        project(slot)

    @pl.when(n == 1)
    def _():
        attend_and_project(1, False)

    for parity in range(2):
        @pl.when((n > 1) & (n < nb) & (n % 2 == parity))
        def _():
            attend_and_project(parity, True)

    @pl.when(n == nb)
    def _():
        slot = (nb - 1) % 2
        load(n - 1, slot).wait()
        store(n - 2, 1 - slot).wait()
        project(slot)
        store(n - 1, slot).wait()


def _attention(qkv, bias, w_o, layer, h, nb, cast_weights):
    bsz = qkv.shape[0]
    casts = [_CastPlan(w, l, nb + 1, lambda n: n) for w, l in cast_weights]
    k_col, v_col = D_MODEL // KV_DIM, D_MODEL // KV_DIM + 1

    def slab_block(n):
        return jnp.where(n == 0, nb - 1, n - 1)

    def blk(col, shift):
        return pl.BlockSpec(
            (bsz, BLOCK, KV_DIM), lambda n: (0, slab_block(jnp.clip(n + shift, 0, nb - 1)), col))

    def meta(col):
        return pl.BlockSpec(
            (bsz, N_META, KV_DIM), lambda n: (0, (nb - 1) * (BLOCK // N_META) + LEAD // N_META, col))

    out, *cast_out = pl.pallas_call(
        functools.partial(_attn_kernel, len(casts), nb),
        grid=(nb + 1,),
        in_specs=[
            pl.BlockSpec((bsz, BLOCK, D_MODEL), lambda n: (0, slab_block(jnp.minimum(n, nb - 1)), 0)),
            blk(k_col, -1), blk(k_col, 0), blk(k_col, 1), meta(k_col),
            blk(v_col, -1), blk(v_col, 0), blk(v_col, 1), meta(v_col),
            pl.BlockSpec((1, N_HEADS, BLOCK, KEYS),
                         lambda n: (jnp.where(n >= nb - 1, 3, jnp.minimum(n, 2)), 0, 0, 0),
                         pipeline_mode=pl.Buffered(1)),
            pl.BlockSpec((None, D_MODEL, D_MODEL), lambda n: (layer, 0, 0), pipeline_mode=pl.Buffered(1)),
            pl.BlockSpec(memory_space=pl.ANY),
            *[c.in_spec for c in casts],
        ],
        out_specs=[pl.BlockSpec(memory_space=pl.ANY), *[c.out_spec for c in casts]],
        out_shape=[jax.ShapeDtypeStruct((bsz, nb * BLOCK, D_MODEL), _F32),
                   *[c.out_shape for c in casts]],
        scratch_shapes=[
            pltpu.VMEM((bsz, BLOCK, D_MODEL), _BF16), pltpu.VMEM((bsz, BLOCK, D_MODEL), _BF16),
            pltpu.VMEM((2, bsz, BLOCK, D_MODEL), _F32),
            pltpu.SemaphoreType.DMA((2,)), pltpu.SemaphoreType.DMA((2,)),
        ],
        compiler_params=pltpu.CompilerParams(
            dimension_semantics=("arbitrary",), vmem_limit_bytes=V7X_VMEM_LIMIT_BYTES),
    )(qkv, qkv, qkv, qkv, qkv, qkv, qkv, qkv, qkv, bias, w_o, h, *[c.operand for c in casts])
    return out, cast_out


def _t5_bucket(rel):
    half = N_BUCKETS // 2
    max_exact = half // 2
    side = jnp.where(rel > 0, half, 0)
    n = jnp.abs(rel)
    nf = jnp.maximum(n, 1).astype(_F32)
    large = max_exact + (jnp.log(nf / max_exact) / math.log(MAX_DISTANCE / max_exact)
                         * (half - max_exact)).astype(jnp.int32)
    large = jnp.minimum(large, half - 1)
    return side + jnp.where(n < max_exact, n, large)


def _bias_codes():
    qi = jnp.arange(BLOCK, dtype=jnp.int32)[:, None]
    col = jnp.arange(3 * BLOCK, dtype=jnp.int32)[None, :]
    rel_band = col - BLOCK - qi
    band = jnp.where(jnp.abs(rel_band) <= WINDOW, _t5_bucket(rel_band), CODE_MASKED)
    tcol = jnp.arange(BLOCK, dtype=jnp.int32)[None, :]
    tails = []
    for n_rep in range(3):
        qpos = n_rep * BLOCK + qi
        codes = jnp.where(tcol < N_META, _t5_bucket(LEAD + tcol - qpos), CODE_MASKED)
        tails.append(jnp.where(tcol == N_META, CODE_SINK, codes))
    return band.astype(jnp.int32), jnp.stack(tails).astype(jnp.int32)


def kernel(x, meta_tokens, rel_bias_table, norm_mix, norm_ffn, conv_in_w, conv_dw, conv_out_w,
           attn_qkv, attn_q_gain, attn_k_gain, attn_sink, attn_o, ffn_up, ffn_dw, ffn_dw_b, ffn_down):
    bsz, seq, d = x.shape
    tp = seq + BLOCK
    nb = tp // BLOCK
    tm = tp // FFN_ROW_TILES_PER_SLAB
    assert tm * FFN_ROW_TILES_PER_SLAB == tp and d == D_MODEL and seq % BLOCK == 0 and nb >= 4

    table = jnp.concatenate([
        rel_bias_table.T.astype(_F32), jnp.full((N_HEADS, 1), NEG_INF, _F32),
        attn_sink[0].astype(_F32)[:, None]], axis=1)
    bias, (in_w, out_w) = _attn_bias(*_bias_codes(), table, [(conv_in_w, 0), (conv_out_w, 0)])

    h, (up0, down0, qkv_w, o_w) = _mixer_from_tokens(
        x, meta_tokens.astype(x.dtype), norm_mix[0], in_w, conv_dw, out_w, 0,
        tp // MIXER_ROW_TILES_PER_SLAB, [(ffn_up, 0), (ffn_down, 0), (attn_qkv, 0), (attn_o, 0)])
    h = _conv_ffn(h, norm_ffn[0], up0, ffn_dw, ffn_dw_b, down0, 0, 0, tm, tp)

    q_scale = HEAD_DIM ** -0.5 * LOG2_E
    head_gain_cols = jnp.concatenate([
        jnp.tile(attn_q_gain[0] * q_scale, N_HEADS), jnp.tile(attn_k_gain[0], N_KV_HEADS),
        jnp.ones((KV_DIM,), _F32)]).reshape(1, QKV_DIM)
    qkv = _qkv_proj(h.reshape(bsz * tp, d), norm_mix[1], qkv_w, 0, head_gain_cols,
                    tp // QKV_ROW_TILES_PER_SLAB)
    h, (up1, down1) = _attention(qkv.reshape(bsz, tp, QKV_DIM), bias, o_w, 0, h, nb,
                                 [(ffn_up, 1), (ffn_down, 1)])
    return _conv_ffn(h, norm_ffn[1], up1, ffn_dw, ffn_dw_b, down1, 0, 1, tm, seq)
```

```python
import functools
import math

import jax
import jax.numpy as jnp
from jax import lax
from jax.experimental import pallas as pl
from jax.experimental.pallas import tpu as pltpu

D_MODEL = 2048
N_META = 16
N_HEADS = 16
N_KV_HEADS = 4
HEAD_DIM = D_MODEL // N_HEADS
GROUP = N_HEADS // N_KV_HEADS
KV_DIM = N_KV_HEADS * HEAD_DIM
QKV_DIM = D_MODEL + 2 * KV_DIM
WINDOW = 128
BLOCK = 128
N_BUCKETS = 32
MAX_DISTANCE = 128
EPS = 1e-6
LEAD = BLOCK - N_META

F32_SUBLANES = 8
BF16_SUBLANES = 16
HALO = BF16_SUBLANES
NORM_ROWS = 32
FFN_ROW_TILES_PER_SLAB = 4
MIXER_ROW_TILES_PER_SLAB = 4
QKV_ROW_TILES_PER_SLAB = 4
MIXER_COL_TILE = 512
FFN_COL_TILE = 512
QKV_COL_TILE = 512
KEYS = 4 * BLOCK
CODE_MASKED = N_BUCKETS
CODE_SINK = N_BUCKETS + 1
N_BIAS_VARIANTS = 4
NEG_INF = -1e30
LOG2_E = math.log2(math.e)
V7X_VMEM_LIMIT_BYTES = 56 * 1024 * 1024

_BF16 = jnp.bfloat16
_F32 = jnp.float32


def _dot(a, b):
    return jnp.dot(a, b, preferred_element_type=_F32)


def _rmsnorm_rows(v, gain):
    return v * lax.rsqrt(jnp.mean(v * v, axis=-1, keepdims=True) + EPS) * gain


def _norm_rows_into(xext_ref, row0, src_ref, n_rows, gain):
    for r in range(0, n_rows, NORM_ROWS):
        rows = min(NORM_ROWS, n_rows - r)
        xext_ref[row0 + r:row0 + r + rows, :] = (
            _rmsnorm_rows(src_ref[r:r + rows, :], gain).astype(_BF16))


def _gelu_exact(v):
    return 0.5 * v * (1.0 + lax.erf(v * math.sqrt(0.5)))


def _row_tile_index(i, tiles):
    return lax.div(i, tiles), lax.rem(i, tiles)


def _tile_copy(op, hbm, acc_ref, sems, tm, tiles, tail_rows, to_hbm, t):
    b, q = _row_tile_index(t, tiles)
    slot = lax.rem(t, 2)

    def copy(rows):
        hbm_rows = hbm.at[b, pl.ds(q * tm, rows), :]
        acc_rows = acc_ref.at[slot, pl.ds(0, rows), :]
        src, dst = (acc_rows, hbm_rows) if to_hbm else (hbm_rows, acc_rows)
        op(pltpu.make_async_copy(src, dst, sems.at[slot]))

    if tail_rows == tm:
        copy(tm)
    else:
        pl.when(q < tiles - 1)(lambda: copy(tm))
        pl.when(q == tiles - 1)(lambda: copy(tail_rows))


def _stream_row_tiles(load, store, body):
    i, j = pl.program_id(0), pl.program_id(1)
    n_i, n_j = pl.num_programs(0), pl.num_programs(1)
    start, wait = (lambda c: c.start()), (lambda c: c.wait())

    pl.when(jnp.logical_and(i == 0, j == 0))(lambda: load(start, t=i))
    pl.when(j == 0)(lambda: load(wait, t=i))
    pl.when(jnp.logical_and(j == 1, i >= 1))(lambda: store(wait, t=i - 1))
    pl.when(jnp.logical_and(j == 1, i + 1 < n_i))(lambda: load(start, t=i + 1))
    body()
    pl.when(j == n_j - 1)(lambda: store(start, t=i))
    pl.when(jnp.logical_and(j == n_j - 1, i == n_i - 1))(lambda: store(wait, t=i))


def _fill_from_slab(hprev_ref, acc, hnext_ref, gain_ref, xext_ref, tm):
    gain = gain_ref[...]
    xext_ref[0:HALO, :] = _rmsnorm_rows(hprev_ref[0], gain).astype(_BF16)
    _norm_rows_into(xext_ref, HALO, acc, tm, gain)
    xext_ref[HALO + tm:, :] = _rmsnorm_rows(hnext_ref[0], gain).astype(_BF16)


def _fill_compact_from_slab(hprev_ref, acc, hnext_ref, gain_ref, xext_ref, tm, tok_rows):
    gain = gain_ref[...]
    row = HALO + tok_rows
    xext_ref[0:HALO, :] = _rmsnorm_rows(hprev_ref[0], gain).astype(_BF16)
    _norm_rows_into(xext_ref, HALO, acc, tok_rows, gain)
    xext_ref[row:row + HALO, :] = jnp.zeros((HALO, D_MODEL), _BF16)
    xext_ref[row + HALO:row + HALO + N_META, :] = (
        _rmsnorm_rows(acc[tm - N_META:tm, :], gain).astype(_BF16))
    xext_ref[row + HALO + N_META:row + 2 * HALO + N_META, :] = (
        _rmsnorm_rows(hnext_ref[0], gain).astype(_BF16))


def _fill_from_tokens(xprev_ref, acc, xnext_ref, meta_ref, gain_ref, xext_ref, tm, tiles, tail_tokens):
    q = lax.rem(pl.program_id(0), tiles)
    gain = gain_ref[...]
    meta = meta_ref[...]
    meta_n = _rmsnorm_rows(meta, gain).astype(_BF16)

    @pl.when(q == 0)
    def _():
        xext_ref[0:HALO, :] = meta_n

    @pl.when(q > 0)
    def _():
        xext_ref[0:HALO, :] = _rmsnorm_rows(xprev_ref[0], gain).astype(_BF16)

    @pl.when(q < tiles - 1)
    def _():
        _norm_rows_into(xext_ref, HALO, acc, tm, gain)
        xext_ref[HALO + tm:, :] = _rmsnorm_rows(xnext_ref[0], gain).astype(_BF16)

    @pl.when(q == tiles - 1)
    def _():
        row = HALO + tail_tokens
        _norm_rows_into(xext_ref, HALO, acc, tail_tokens, gain)
        xext_ref[row:row + HALO, :] = jnp.zeros((HALO, D_MODEL), _BF16)
        xext_ref[row + HALO:row + HALO + N_META, :] = meta_n
        xext_ref[row + HALO + N_META:row + 2 * HALO + N_META, :] = (
            _rmsnorm_rows(xnext_ref[0], gain).astype(_BF16))
        acc[tail_tokens:tail_tokens + LEAD, :] = jnp.zeros((LEAD, D_MODEL), _F32)
        acc[tail_tokens + LEAD:tm, :] = meta


class _CastPlan:
    def __init__(self, w, layer, n_steps, step_of_grid):
        _, rows, cols = w.shape
        n_blocks = next(nb for nb in range(min(n_steps, rows // BF16_SUBLANES), 0, -1)
                        if rows % nb == 0 and (rows // nb) % BF16_SUBLANES == 0)
        block_rows = rows // n_blocks

        def block(*grid_idx):
            return lax.div(step_of_grid(*grid_idx) * n_blocks, n_steps)

        self.operand = w
        self.in_spec = pl.BlockSpec((None, block_rows, cols), lambda *g: (layer, block(*g), 0))
        self.out_spec = pl.BlockSpec((None, block_rows, cols), lambda *g: (0, block(*g), 0))
        self.out_shape = jax.ShapeDtypeStruct((1, rows, cols), _BF16)


def _run_casts(refs):
    n = len(refs) // 2
    for src, dst in zip(refs[:n], refs[n:]):
        dst[...] = src[...].astype(_BF16)


def _conv3(v_ext, w, tm):
    return (v_ext[HALO - 1:HALO - 1 + tm] * w[0:1]
            + v_ext[HALO:HALO + tm] * w[1:2]
            + v_ext[HALO + 1:HALO + 1 + tm] * w[2:3])


def _mixer_kernel(tm, tiles, tail_tokens, n_casts, xprev_ref, x_hbm, xnext_ref, meta_ref, gain_ref, wb_ref,
                  wc_ref, wh_ref, cw_ref, wo_ref, *rest):
    cast_srcs, out_hbm, cast_dsts = rest[:n_casts], rest[n_casts], rest[n_casts + 1:2 * n_casts + 1]
    xext_ref, acc_ref, in_sems, out_sems = rest[2 * n_casts + 1:]
    acc = acc_ref.at[lax.rem(pl.program_id(0), 2)]
    last_tile = lax.rem(pl.program_id(0), tiles) == tiles - 1
    compact_rows = tail_tokens + HALO + N_META
    _run_casts(cast_srcs + cast_dsts)

    def add_all(r):
        acc[...] += r

    def add_compact(r):
        acc[0:tail_tokens, :] += r[0:tail_tokens]
        acc[tm - N_META:tm, :] += r[tail_tokens + HALO:compact_rows]

    def step(m, add):
        x = xext_ref[0:m + 2 * HALO, :]
        ch = _conv3(_dot(x, wc_ref[...]) * _dot(x, wh_ref[...]), cw_ref[...], m)
        y = _dot(xext_ref[HALO:HALO + m, :], wb_ref[...]) * ch
        add(_dot(y.astype(_BF16), wo_ref[...]))

    def body():
        @pl.when(pl.program_id(1) == 0)
        def _():
            _fill_from_tokens(xprev_ref, acc, xnext_ref, meta_ref, gain_ref, xext_ref, tm, tiles,
                              tail_tokens)

        pl.when(jnp.logical_not(last_tile))(functools.partial(step, tm, add_all))
        pl.when(last_tile)(functools.partial(step, compact_rows, add_compact))

    _stream_row_tiles(
        functools.partial(_tile_copy, hbm=x_hbm, acc_ref=acc_ref, sems=in_sems, tm=tm, tiles=tiles,
                          tail_rows=tail_tokens, to_hbm=False),
        functools.partial(_tile_copy, hbm=out_hbm, acc_ref=acc_ref, sems=out_sems, tm=tm, tiles=tiles,
                          tail_rows=tm, to_hbm=True),
        body)


def _ffn_kernel(tm, tiles, out_tail_rows, n_chunks, tok_rows, hprev_ref, h_hbm, hnext_ref, gain_ref,
                wg_ref, wua_ref, wub_ref, cw_ref, cb_ref, wd_ref, out_hbm, xext_ref, acc_ref, in_sems,
                out_sems):
    c = FFN_COL_TILE
    i, j = pl.program_id(0), pl.program_id(1)
    acc = acc_ref.at[lax.rem(i, 2)]
    last_tile = lax.rem(i, tiles) == tiles - 1
    n_pairs, odd = n_chunks // 2, n_chunks % 2
    compact_rows = tok_rows + HALO + N_META

    def add_all(r):
        acc[...] += r

    def add_compact(r):
        acc[0:tok_rows, :] += r[0:tok_rows]
        acc[tm - N_META:tm, :] += r[tok_rows + HALO:compact_rows]

    def steps(m, add):
        def act(g_ext, u, lo):
            g = _conv3(g_ext, cw_ref[:, lo:lo + c], m) + cb_ref[:, lo:lo + c]
            return (_gelu_exact(g) * u).astype(_BF16)

        def pair_step():
            x, xc = xext_ref[0:m + 2 * HALO, :], xext_ref[HALO:HALO + m, :]
            g2 = _dot(x, wg_ref[...])
            ua, ub = _dot(xc, wua_ref[...]), _dot(xc, wub_ref[...])
            a = jnp.concatenate([act(g2[:, :c], ua, 0), act(g2[:, c:], ub, c)], axis=1)
            add(_dot(a, wd_ref[...]))

        def single_step():
            g = _dot(xext_ref[0:m + 2 * HALO, :], wg_ref[:, 0:c])
            ua = _dot(xext_ref[HALO:HALO + m, :], wua_ref[...])
            add(_dot(act(g, ua, 0), wd_ref[0:c, :]))

        return pair_step, single_step

    def body():
        @pl.when(jnp.logical_and(j == 0, jnp.logical_not(last_tile)))
        def _():
            _fill_from_slab(hprev_ref, acc, hnext_ref, gain_ref, xext_ref, tm)

        @pl.when(jnp.logical_and(j == 0, last_tile))
        def _():
            _fill_compact_from_slab(hprev_ref, acc, hnext_ref, gain_ref, xext_ref, tm, tok_rows)

        for this_tile, (pair_step, single_step) in (
                (jnp.logical_not(last_tile), steps(tm, add_all)),
                (last_tile, steps(compact_rows, add_compact))):
            pl.when(jnp.logical_and(this_tile, j < n_pairs))(pair_step)
            if odd:
                pl.when(jnp.logical_and(this_tile, j == n_pairs))(single_step)

    _stream_row_tiles(
        functools.partial(_tile_copy, hbm=h_hbm, acc_ref=acc_ref, sems=in_sems, tm=tm, tiles=tiles,
                          tail_rows=tm, to_hbm=False),
        functools.partial(_tile_copy, hbm=out_hbm, acc_ref=acc_ref, sems=out_sems, tm=tm, tiles=tiles,
                          tail_rows=out_tail_rows, to_hbm=True),
        body)


def _fused_params():
    return pltpu.CompilerParams(dimension_semantics=("arbitrary", "arbitrary"),
                                vmem_limit_bytes=V7X_VMEM_LIMIT_BYTES)


def _fused_scratch(tm, d):
    return [pltpu.VMEM((tm + 2 * HALO, d), _BF16), pltpu.VMEM((2, tm, d), _F32),
            pltpu.SemaphoreType.DMA((2,)), pltpu.SemaphoreType.DMA((2,))]


def _weight_specs(layer, d, hidden, n_up, col_tile):
    n_j = hidden // col_tile
    ups = [pl.BlockSpec((None, d, col_tile),
                        functools.partial(lambda i, j, k: (layer, 0, j + k * n_j), k=k))
           for k in range(n_up)]
    down = pl.BlockSpec((None, col_tile, d), lambda i, j: (layer, j, 0))
    return ups, down


def _mixer_from_tokens(x, meta_tokens, gain, w_in, conv_w, w_out, layer, tm, cast_weights):
    bsz, seq, d = x.shape
    tp = seq + BLOCK
    tiles = tp // tm
    tail_tokens = seq - (tiles - 1) * tm
    assert tiles * tm == tp and 0 < tail_tokens < tm and tail_tokens % HALO == 0 and tm % HALO == 0
    halo_blocks_per_tile = tm // HALO
    col_tile = MIXER_COL_TILE

    def prev_map(i, j):
        b, q = _row_tile_index(i, tiles)
        return b, jnp.maximum(q * halo_blocks_per_tile - 1, 0), 0

    def next_map(i, j):
        b, q = _row_tile_index(i, tiles)
        wrap = q == tiles - 1
        return (jnp.where(wrap, lax.rem(b + 1, bsz), b),
                jnp.where(wrap, 0, (q + 1) * halo_blocks_per_tile), 0)

    ups, down = _weight_specs(layer, d, d, 3, col_tile)
    n_j = d // col_tile
    casts = [_CastPlan(w, l, bsz * tiles * n_j, lambda i, j: i * n_j + j) for w, l in cast_weights]
    in_specs = [
        pl.BlockSpec((1, HALO, d), prev_map),
        pl.BlockSpec(memory_space=pl.ANY),
        pl.BlockSpec((1, HALO, d), next_map),
        pl.BlockSpec((N_META, d), lambda i, j: (0, 0)),
        pl.BlockSpec((1, d), lambda i, j: (0, 0)),
        *ups,
        pl.BlockSpec((None, 3, col_tile), lambda i, j: (layer, 0, j)),
        down,
        *[c.in_spec for c in casts],
    ]
    out, *cast_out = pl.pallas_call(
        functools.partial(_mixer_kernel, tm, tiles, tail_tokens, len(casts)),
        grid=(bsz * tiles, n_j),
        in_specs=in_specs,
        out_specs=[pl.BlockSpec(memory_space=pl.ANY), *[c.out_spec for c in casts]],
        out_shape=[jax.ShapeDtypeStruct((bsz, tp, d), _F32), *[c.out_shape for c in casts]],
        scratch_shapes=_fused_scratch(tm, d),
        compiler_params=_fused_params(),
    )(x, x, x, meta_tokens, gain.reshape(1, d), w_in, w_in, w_in, conv_w, w_out,
      *[c.operand for c in casts])
    return out, cast_out


def _conv_ffn(h, gain, w_up, conv_w, conv_b, w_down, layer, conv_layer, tm, out_rows):
    bsz, tp, d = h.shape
    hidden = w_down.shape[1]
    tiles = tp // tm
    out_tail_rows = out_rows - (tiles - 1) * tm
    tok_rows = tm - BLOCK
    assert 0 < tok_rows and tok_rows % NORM_ROWS == 0 and LEAD >= HALO
    assert tiles * tm == tp and tm % HALO == 0 and 0 < out_tail_rows <= tm and out_tail_rows % F32_SUBLANES == 0
    halo_blocks_per_tile = tm // HALO
    halo_blocks_per_slab = tp // HALO
    n_halo_blocks = bsz * halo_blocks_per_slab
    col_tile = FFN_COL_TILE

    def halo_map(i, j, offset):
        blk = lax.rem(i * halo_blocks_per_tile + offset + n_halo_blocks, n_halo_blocks)
        return lax.div(blk, halo_blocks_per_slab), lax.rem(blk, halo_blocks_per_slab), 0

    n_chunks = hidden // col_tile
    assert n_chunks * col_tile == hidden and w_up.shape[2] == 2 * hidden
    in_specs = [
        pl.BlockSpec((1, HALO, d), functools.partial(halo_map, offset=-1)),
        pl.BlockSpec(memory_space=pl.ANY),
        pl.BlockSpec((1, HALO, d), functools.partial(halo_map, offset=halo_blocks_per_tile)),
        pl.BlockSpec((1, d), lambda i, j: (0, 0)),
        pl.BlockSpec((None, d, 2 * col_tile), lambda i, j: (layer, 0, j)),
        pl.BlockSpec((None, d, col_tile), lambda i, j: (layer, 0, n_chunks + 2 * j)),
        pl.BlockSpec((None, d, col_tile),
                     lambda i, j: (layer, 0, jnp.minimum(n_chunks + 2 * j + 1, 2 * n_chunks - 1))),
        pl.BlockSpec((None, 3, 2 * col_tile), lambda i, j: (conv_layer, 0, j)),
        pl.BlockSpec((None, 1, 2 * col_tile), lambda i, j: (conv_layer, 0, j)),
        pl.BlockSpec((None, 2 * col_tile, d), lambda i, j: (layer, j, 0)),
    ]
    return pl.pallas_call(
        functools.partial(_ffn_kernel, tm, tiles, out_tail_rows, n_chunks, tok_rows),
        grid=(bsz * tiles, (n_chunks + 1) // 2),
        in_specs=in_specs,
        out_specs=pl.BlockSpec(memory_space=pl.ANY),
        out_shape=jax.ShapeDtypeStruct((bsz, out_rows, d), _F32),
        scratch_shapes=_fused_scratch(tm, d),
        compiler_params=_fused_params(),
    )(h, h, h, gain.reshape(1, d), w_up, w_up, w_up, conv_w, conv_b.reshape(conv_b.shape[0], 1, hidden),
      w_down)


def _qkv_kernel(n_norm_cols, h_ref, gain_ref, w_ref, hgain_ref, out_ref, xn_ref):
    _norm_rows_into(xn_ref, 0, h_ref, h_ref.shape[0], gain_ref[...])
    for c in range(w_ref.shape[1] // QKV_COL_TILE):
        acc = _dot(xn_ref[...], w_ref[:, c * QKV_COL_TILE:(c + 1) * QKV_COL_TILE])
        for hd in range(QKV_COL_TILE // HEAD_DIM):
            lo = hd * HEAD_DIM
            cols = slice(c * QKV_COL_TILE + lo, c * QKV_COL_TILE + lo + HEAD_DIM)
            head = acc[:, lo:lo + HEAD_DIM]
            if c * QKV_COL_TILE + lo < n_norm_cols:
                head = _rmsnorm_rows(head, hgain_ref[:, cols])
            out_ref[:, cols] = head.astype(_BF16)


def _qkv_proj(h, gain, w_qkv, layer, head_gain_cols, tm):
    m, d = h.shape
    n_out = w_qkv.shape[2]
    return pl.pallas_call(
        functools.partial(_qkv_kernel, D_MODEL + KV_DIM),
        grid=(m // tm,),
        in_specs=[
            pl.BlockSpec((tm, d), lambda i: (i, 0)),
            pl.BlockSpec((1, d), lambda i: (0, 0)),
            pl.BlockSpec((None, d, n_out), lambda i: (layer, 0, 0), pipeline_mode=pl.Buffered(1)),
            pl.BlockSpec((1, n_out), lambda i: (0, 0)),
        ],
        out_specs=pl.BlockSpec((tm, n_out), lambda i: (i, 0)),
        out_shape=jax.ShapeDtypeStruct((m, n_out), _BF16),
        scratch_shapes=[pltpu.VMEM((tm, d), _BF16)],
        compiler_params=pltpu.CompilerParams(
            dimension_semantics=("arbitrary",), vmem_limit_bytes=V7X_VMEM_LIMIT_BYTES),
    )(h, gain.reshape(1, d), w_qkv, head_gain_cols)


def _bias_kernel(n_casts, band_codes_ref, tail_codes_ref, tbl_ref, *rest):
    out_ref = rest[n_casts]
    _run_casts(rest[:n_casts] + rest[n_casts + 1:])
    head = pl.program_id(0)

    def lookup(codes):
        acc = jnp.zeros(codes.shape, _F32)
        for code in range(N_BUCKETS + 2):
            acc = jnp.where(codes == code, tbl_ref[head, code] * LOG2_E, acc)
        return acc

    band = lookup(band_codes_ref[...])
    tails = [lookup(tail_codes_ref[v]) for v in range(3)]
    masked = jnp.full((BLOCK, BLOCK), NEG_INF, _F32)
    for variant, valid_blocks, tail in ((0, (2,), 0), (1, (1, 2), 1), (2, (0, 1, 2), 2), (3, (0, 1), 2)):
        for blk in range(3):
            cols = slice(blk * BLOCK, (blk + 1) * BLOCK)
            out_ref[variant, 0, :, cols] = band[:, cols] if blk in valid_blocks else masked
        out_ref[variant, 0, :, 3 * BLOCK:] = tails[tail]


def _attn_bias(band_codes, tail_codes, table, cast_weights):
    casts = [_CastPlan(w, l, N_HEADS, lambda h: h) for w, l in cast_weights]
    bias, *cast_out = pl.pallas_call(
        functools.partial(_bias_kernel, len(casts)),
        grid=(N_HEADS,),
        in_specs=[
            pl.BlockSpec((BLOCK, 3 * BLOCK), lambda h: (0, 0)),
            pl.BlockSpec((3, BLOCK, BLOCK), lambda h: (0, 0, 0)),
            pl.BlockSpec(memory_space=pltpu.SMEM),
            *[c.in_spec for c in casts],
        ],
        out_specs=[pl.BlockSpec((N_BIAS_VARIANTS, 1, BLOCK, KEYS), lambda h: (0, h, 0, 0)),
                   *[c.out_spec for c in casts]],
        out_shape=[jax.ShapeDtypeStruct((N_BIAS_VARIANTS, N_HEADS, BLOCK, KEYS), _F32),
                   *[c.out_shape for c in casts]],
    )(band_codes, tail_codes, table, *[c.operand for c in casts])
    return bias, cast_out


def _attn_chain(q_ref, k_refs, v_refs, bias_ref, out_ref, slab, k_slabs, g, row0):
    rows = BLOCK - row0
    kv_cols = slice(g * HEAD_DIM, (g + 1) * HEAD_DIM)
    head_cols = [slice((g * GROUP + r) * HEAD_DIM, (g * GROUP + r + 1) * HEAD_DIM) for r in range(GROUP)]
    dead = jnp.zeros((BLOCK - N_META, HEAD_DIM), _BF16)
    q = jnp.concatenate([q_ref[slab, row0:, cols] for cols in head_cols], axis=0)
    k = jnp.concatenate([ref[ks, :, kv_cols] for ref, ks in zip(k_refs, k_slabs)] + [dead], axis=0)
    v = jnp.concatenate([ref[ks, :, kv_cols] for ref, ks in zip(v_refs, k_slabs)] + [dead], axis=0)
    s = lax.dot_general(q, k, (((1,), (1,)), ((), ())), preferred_element_type=_F32)
    s = s + jnp.concatenate([bias_ref[0, g * GROUP + r, row0:, :] for r in range(GROUP)], axis=0)
    e = jnp.exp2(s - jnp.max(s, axis=-1, keepdims=True))
    denom = jnp.sum(e, axis=-1, keepdims=True)
    o = _dot(e.astype(_BF16), v) / denom
    for r, cols in enumerate(head_cols):
        out_ref[slab, row0:, cols] = o[r * rows:(r + 1) * rows].astype(_BF16)


def _attn_kernel(n_casts, nb, q_ref, kp_ref, kc_ref, kn_ref, km_ref, vp_ref, vc_ref, vn_ref, vm_ref,
                 bias_ref, w_ref, h_hbm, *rest):
    out_hbm = rest[n_casts]
    a0_ref, a1_ref, acc_ref, in_sems, out_sems = rest[-5:]
    a_refs = (a0_ref, a1_ref)
    _run_casts(rest[:n_casts] + rest[n_casts + 1:-5])
    n = pl.program_id(0)
    n_slabs = q_ref.shape[0]
    k_refs = (kp_ref, kc_ref, kn_ref, km_ref)
    v_refs = (vp_ref, vc_ref, vn_ref, vm_ref)

    def block_copy(hbm, sems, to_hbm, p, slot):
        row0 = pl.multiple_of(jnp.where(p == 0, nb - 1, p - 1) * BLOCK, BLOCK)
        window = hbm.at[:, pl.ds(row0, BLOCK), :]
        src, dst = (acc_ref.at[slot], window) if to_hbm else (window, acc_ref.at[slot])
        return pltpu.make_async_copy(src, dst, sems.at[slot])

    load = functools.partial(block_copy, h_hbm, in_sems, False)
    store = functools.partial(block_copy, out_hbm, out_sems, True)

    def chains(first_block, a_ref, slabs):
        for slab in slabs:
            if first_block:
                k_slabs = (slab, slab, (slab + 1) % n_slabs, slab)
            else:
                k_slabs = (slab, slab, slab, (slab - 1) % n_slabs)
            for g in range(N_KV_HEADS):
                _attn_chain(q_ref, k_refs, v_refs, bias_ref, a_ref, slab, k_slabs, g,
                            LEAD if first_block else 0)

    def project(slot):
        a = a_refs[slot][...].reshape(n_slabs * BLOCK, D_MODEL)
        acc_ref[slot] += _dot(a, w_ref[...]).reshape(n_slabs, BLOCK, D_MODEL)
        store(n - 1, slot).start()

    @pl.when(n == 0)
    def _():
        load(n, 0).start()
        chains(True, a0_ref, range(n_slabs))
        a0_ref[:, 0:LEAD, :] = jnp.zeros((n_slabs, LEAD, D_MODEL), _BF16)

    def attend_and_project(parity, store_pending):
        slot = 1 - parity
        load(n - 1, slot).wait()
        chains(False, a_refs[parity], range(n_slabs))
        if store_pending:
            store(n - 2, parity).wait()
        load(n, parity).start()
        project(slot)

    @pl.when(n == 1)
    def _():
        attend_and_project(1, False)

    for parity in range(2):
        @pl.when((n > 1) & (n < nb) & (n % 2 == parity))
        def _():
            attend_and_project(parity, True)

    @pl.when(n == nb)
    def _():
        slot = (nb - 1) % 2
        load(n - 1, slot).wait()
        store(n - 2, 1 - slot).wait()
        project(slot)
        store(n - 1, slot).wait()


def _attention(qkv, bias, w_o, layer, h, nb, cast_weights):
    bsz = qkv.shape[0]
    casts = [_CastPlan(w, l, nb + 1, lambda n: n) for w, l in cast_weights]
    k_col, v_col = D_MODEL // KV_DIM, D_MODEL // KV_DIM + 1

    def slab_block(n):
        return jnp.where(n == 0, nb - 1, n - 1)

    def blk(col, shift):
        return pl.BlockSpec(
            (bsz, BLOCK, KV_DIM), lambda n: (0, slab_block(jnp.clip(n + shift, 0, nb - 1)), col))

    def meta(col):
        return pl.BlockSpec(
            (bsz, N_META, KV_DIM), lambda n: (0, (nb - 1) * (BLOCK // N_META) + LEAD // N_META, col))

    out, *cast_out = pl.pallas_call(
        functools.partial(_attn_kernel, len(casts), nb),
        grid=(nb + 1,),
        in_specs=[
            pl.BlockSpec((bsz, BLOCK, D_MODEL), lambda n: (0, slab_block(jnp.minimum(n, nb - 1)), 0)),
            blk(k_col, -1), blk(k_col, 0), blk(k_col, 1), meta(k_col),
            blk(v_col, -1), blk(v_col, 0), blk(v_col, 1), meta(v_col),
            pl.BlockSpec((1, N_HEADS, BLOCK, KEYS),
                         lambda n: (jnp.where(n >= nb - 1, 3, jnp.minimum(n, 2)), 0, 0, 0),
                         pipeline_mode=pl.Buffered(1)),
            pl.BlockSpec((None, D_MODEL, D_MODEL), lambda n: (layer, 0, 0), pipeline_mode=pl.Buffered(1)),
            pl.BlockSpec(memory_space=pl.ANY),
            *[c.in_spec for c in casts],
        ],
        out_specs=[pl.BlockSpec(memory_space=pl.ANY), *[c.out_spec for c in casts]],
        out_shape=[jax.ShapeDtypeStruct((bsz, nb * BLOCK, D_MODEL), _F32),
                   *[c.out_shape for c in casts]],
        scratch_shapes=[
            pltpu.VMEM((bsz, BLOCK, D_MODEL), _BF16), pltpu.VMEM((bsz, BLOCK, D_MODEL), _BF16),
            pltpu.VMEM((2, bsz, BLOCK, D_MODEL), _F32),
            pltpu.SemaphoreType.DMA((2,)), pltpu.SemaphoreType.DMA((2,)),
        ],
        compiler_params=pltpu.CompilerParams(
            dimension_semantics=("arbitrary",), vmem_limit_bytes=V7X_VMEM_LIMIT_BYTES),
    )(qkv, qkv, qkv, qkv, qkv, qkv, qkv, qkv, qkv, bias, w_o, h, *[c.operand for c in casts])
    return out, cast_out


def _t5_bucket(rel):
    half = N_BUCKETS // 2
    max_exact = half // 2
    side = jnp.where(rel > 0, half, 0)
    n = jnp.abs(rel)
    nf = jnp.maximum(n, 1).astype(_F32)
    large = max_exact + (jnp.log(nf / max_exact) / math.log(MAX_DISTANCE / max_exact)
                         * (half - max_exact)).astype(jnp.int32)
    large = jnp.minimum(large, half - 1)
    return side + jnp.where(n < max_exact, n, large)


def _bias_codes():
    qi = jnp.arange(BLOCK, dtype=jnp.int32)[:, None]
    col = jnp.arange(3 * BLOCK, dtype=jnp.int32)[None, :]
    rel_band = col - BLOCK - qi
    band = jnp.where(jnp.abs(rel_band) <= WINDOW, _t5_bucket(rel_band), CODE_MASKED)
    tcol = jnp.arange(BLOCK, dtype=jnp.int32)[None, :]
    tails = []
    for n_rep in range(3):
        qpos = n_rep * BLOCK + qi
        codes = jnp.where(tcol < N_META, _t5_bucket(LEAD + tcol - qpos), CODE_MASKED)
        tails.append(jnp.where(tcol == N_META, CODE_SINK, codes))
    return band.astype(jnp.int32), jnp.stack(tails).astype(jnp.int32)


def kernel(x, meta_tokens, rel_bias_table, norm_mix, norm_ffn, conv_in_w, conv_dw, conv_out_w,
           attn_qkv, attn_q_gain, attn_k_gain, attn_sink, attn_o, ffn_up, ffn_dw, ffn_dw_b, ffn_down):
    bsz, seq, d = x.shape
    tp = seq + BLOCK
    nb = tp // BLOCK
    tm = tp // FFN_ROW_TILES_PER_SLAB
    assert tm * FFN_ROW_TILES_PER_SLAB == tp and d == D_MODEL and seq % BLOCK == 0 and nb >= 4

    table = jnp.concatenate([
        rel_bias_table.T.astype(_F32), jnp.full((N_HEADS, 1), NEG_INF, _F32),
        attn_sink[0].astype(_F32)[:, None]], axis=1)
    bias, (in_w, out_w) = _attn_bias(*_bias_codes(), table, [(conv_in_w, 0), (conv_out_w, 0)])

    h, (up0, down0, qkv_w, o_w) = _mixer_from_tokens(
        x, meta_tokens.astype(x.dtype), norm_mix[0], in_w, conv_dw, out_w, 0,
        tp // MIXER_ROW_TILES_PER_SLAB, [(ffn_up, 0), (ffn_down, 0), (attn_qkv, 0), (attn_o, 0)])
    h = _conv_ffn(h, norm_ffn[0], up0, ffn_dw, ffn_dw_b, down0, 0, 0, tm, tp)

    q_scale = HEAD_DIM ** -0.5 * LOG2_E
    head_gain_cols = jnp.concatenate([
        jnp.tile(attn_q_gain[0] * q_scale, N_HEADS), jnp.tile(attn_k_gain[0], N_KV_HEADS),
        jnp.ones((KV_DIM,), _F32)]).reshape(1, QKV_DIM)
    qkv = _qkv_proj(h.reshape(bsz * tp, d), norm_mix[1], qkv_w, 0, head_gain_cols,
                    tp // QKV_ROW_TILES_PER_SLAB)
    h, (up1, down1) = _attention(qkv.reshape(bsz, tp, QKV_DIM), bias, o_w, 0, h, nb,
                                 [(ffn_up, 1), (ffn_down, 1)])
    return _conv_ffn(h, norm_ffn[1], up1, ffn_dw, ffn_dw_b, down1, 0, 1, tm, seq)
```

```python
import functools
import math

import jax
import jax.numpy as jnp
from jax import lax
from jax.experimental import pallas as pl
from jax.experimental.pallas import tpu as pltpu

D_MODEL = 2048
N_META = 16
N_HEADS = 16
N_KV_HEADS = 4
HEAD_DIM = D_MODEL // N_HEADS
GROUP = N_HEADS // N_KV_HEADS
KV_DIM = N_KV_HEADS * HEAD_DIM
QKV_DIM = D_MODEL + 2 * KV_DIM
WINDOW = 128
BLOCK = 128
N_BUCKETS = 32
MAX_DISTANCE = 128
EPS = 1e-6
LEAD = BLOCK - N_META

F32_SUBLANES = 8
BF16_SUBLANES = 16
HALO = BF16_SUBLANES
NORM_ROWS = 32
FFN_ROW_TILES_PER_SLAB = 4
MIXER_ROW_TILES_PER_SLAB = 4
QKV_ROW_TILES_PER_SLAB = 4
OUT_PROJ_ROW_TILES_PER_SLAB = 8
MIXER_COL_TILE = 512
FFN_COL_TILE = 512
QKV_COL_TILE = 512
KEYS = 4 * BLOCK
CODE_MASKED = N_BUCKETS
CODE_SINK = N_BUCKETS + 1
N_BIAS_VARIANTS = 4
NEG_INF = -1e30
LOG2_E = math.log2(math.e)
V7X_VMEM_LIMIT_BYTES = 56 * 1024 * 1024

_BF16 = jnp.bfloat16
_F32 = jnp.float32


def _dot(a, b):
    return jnp.dot(a, b, preferred_element_type=_F32)


def _rmsnorm_rows(v, gain):
    return v * lax.rsqrt(jnp.mean(v * v, axis=-1, keepdims=True) + EPS) * gain


def _norm_rows_into(xext_ref, row0, src_ref, n_rows, gain):
    for r in range(0, n_rows, NORM_ROWS):
        rows = min(NORM_ROWS, n_rows - r)
        xext_ref[row0 + r:row0 + r + rows, :] = (
            _rmsnorm_rows(src_ref[r:r + rows, :], gain).astype(_BF16))


def _gelu_exact(v):
    return 0.5 * v * (1.0 + lax.erf(v * math.sqrt(0.5)))


def _row_tile_index(i, tiles):
    return lax.div(i, tiles), lax.rem(i, tiles)


def _tile_copy(op, hbm, acc_ref, sems, tm, tiles, tail_rows, to_hbm, t):
    b, q = _row_tile_index(t, tiles)
    slot = lax.rem(t, 2)

    def copy(rows):
        hbm_rows = hbm.at[b, pl.ds(q * tm, rows), :]
        acc_rows = acc_ref.at[slot, pl.ds(0, rows), :]
        src, dst = (acc_rows, hbm_rows) if to_hbm else (hbm_rows, acc_rows)
        op(pltpu.make_async_copy(src, dst, sems.at[slot]))

    if tail_rows == tm:
        copy(tm)
    else:
        pl.when(q < tiles - 1)(lambda: copy(tm))
        pl.when(q == tiles - 1)(lambda: copy(tail_rows))


def _stream_row_tiles(load, store, body):
    i, j = pl.program_id(0), pl.program_id(1)
    n_i, n_j = pl.num_programs(0), pl.num_programs(1)
    start, wait = (lambda c: c.start()), (lambda c: c.wait())

    pl.when(jnp.logical_and(i == 0, j == 0))(lambda: load(start, t=i))
    pl.when(j == 0)(lambda: load(wait, t=i))
    pl.when(jnp.logical_and(j == 1, i >= 1))(lambda: store(wait, t=i - 1))
    pl.when(jnp.logical_and(j == 1, i + 1 < n_i))(lambda: load(start, t=i + 1))
    body()
    pl.when(j == n_j - 1)(lambda: store(start, t=i))
    pl.when(jnp.logical_and(j == n_j - 1, i == n_i - 1))(lambda: store(wait, t=i))


def _fill_from_slab(hprev_ref, acc, hnext_ref, gain_ref, xext_ref, tm):
    gain = gain_ref[...]
    xext_ref[0:HALO, :] = _rmsnorm_rows(hprev_ref[0], gain).astype(_BF16)
    _norm_rows_into(xext_ref, HALO, acc, tm, gain)
    xext_ref[HALO + tm:, :] = _rmsnorm_rows(hnext_ref[0], gain).astype(_BF16)


def _fill_compact_from_slab(hprev_ref, acc, hnext_ref, gain_ref, xext_ref, tm, tok_rows):
    gain = gain_ref[...]
    row = HALO + tok_rows
    xext_ref[0:HALO, :] = _rmsnorm_rows(hprev_ref[0], gain).astype(_BF16)
    _norm_rows_into(xext_ref, HALO, acc, tok_rows, gain)
    xext_ref[row:row + HALO, :] = jnp.zeros((HALO, D_MODEL), _BF16)
    xext_ref[row + HALO:row + HALO + N_META, :] = (
        _rmsnorm_rows(acc[tm - N_META:tm, :], gain).astype(_BF16))
    xext_ref[row + HALO + N_META:row + 2 * HALO + N_META, :] = (
        _rmsnorm_rows(hnext_ref[0], gain).astype(_BF16))


def _fill_from_tokens(xprev_ref, acc, xnext_ref, meta_ref, gain_ref, xext_ref, tm, tiles, tail_tokens):
    q = lax.rem(pl.program_id(0), tiles)
    gain = gain_ref[...]
    meta = meta_ref[...]
    meta_n = _rmsnorm_rows(meta, gain).astype(_BF16)

    @pl.when(q == 0)
    def _():
        xext_ref[0:HALO, :] = meta_n

    @pl.when(q > 0)
    def _():
        xext_ref[0:HALO, :] = _rmsnorm_rows(xprev_ref[0], gain).astype(_BF16)

    @pl.when(q < tiles - 1)
    def _():
        _norm_rows_into(xext_ref, HALO, acc, tm, gain)
        xext_ref[HALO + tm:, :] = _rmsnorm_rows(xnext_ref[0], gain).astype(_BF16)

    @pl.when(q == tiles - 1)
    def _():
        row = HALO + tail_tokens
        _norm_rows_into(xext_ref, HALO, acc, tail_tokens, gain)
        xext_ref[row:row + HALO, :] = jnp.zeros((HALO, D_MODEL), _BF16)
        xext_ref[row + HALO:row + HALO + N_META, :] = meta_n
        xext_ref[row + HALO + N_META:row + 2 * HALO + N_META, :] = (
            _rmsnorm_rows(xnext_ref[0], gain).astype(_BF16))
        acc[tail_tokens:tail_tokens + LEAD, :] = jnp.zeros((LEAD, D_MODEL), _F32)
        acc[tail_tokens + LEAD:tm, :] = meta


class _CastPlan:
    def __init__(self, w, layer, n_steps, step_of_grid):
        _, rows, cols = w.shape
        n_blocks = next(nb for nb in range(min(n_steps, rows // BF16_SUBLANES), 0, -1)
                        if rows % nb == 0 and (rows // nb) % BF16_SUBLANES == 0)
        block_rows = rows // n_blocks

        def block(*grid_idx):
            return lax.div(step_of_grid(*grid_idx) * n_blocks, n_steps)

        self.operand = w
        self.in_spec = pl.BlockSpec((None, block_rows, cols), lambda *g: (layer, block(*g), 0))
        self.out_spec = pl.BlockSpec((None, block_rows, cols), lambda *g: (0, block(*g), 0))
        self.out_shape = jax.ShapeDtypeStruct((1, rows, cols), _BF16)


def _run_casts(refs):
    n = len(refs) // 2
    for src, dst in zip(refs[:n], refs[n:]):
        dst[...] = src[...].astype(_BF16)


def _conv3(v_ext, w, tm):
    return (v_ext[HALO - 1:HALO - 1 + tm] * w[0:1]
            + v_ext[HALO:HALO + tm] * w[1:2]
            + v_ext[HALO + 1:HALO + 1 + tm] * w[2:3])


def _mixer_kernel(tm, tiles, tail_tokens, n_casts, xprev_ref, x_hbm, xnext_ref, meta_ref, gain_ref, wb_ref,
                  wc_ref, wh_ref, cw_ref, wo_ref, *rest):
    cast_srcs, out_hbm, cast_dsts = rest[:n_casts], rest[n_casts], rest[n_casts + 1:2 * n_casts + 1]
    xext_ref, acc_ref, in_sems, out_sems = rest[2 * n_casts + 1:]
    acc = acc_ref.at[lax.rem(pl.program_id(0), 2)]
    last_tile = lax.rem(pl.program_id(0), tiles) == tiles - 1
    compact_rows = tail_tokens + HALO + N_META
    _run_casts(cast_srcs + cast_dsts)

    def add_all(r):
        acc[...] += r

    def add_compact(r):
        acc[0:tail_tokens, :] += r[0:tail_tokens]
        acc[tm - N_META:tm, :] += r[tail_tokens + HALO:compact_rows]

    def step(m, add):
        x = xext_ref[0:m + 2 * HALO, :]
        ch = _conv3(_dot(x, wc_ref[...]) * _dot(x, wh_ref[...]), cw_ref[...], m)
        y = _dot(xext_ref[HALO:HALO + m, :], wb_ref[...]) * ch
        add(_dot(y.astype(_BF16), wo_ref[...]))

    def body():
        @pl.when(pl.program_id(1) == 0)
        def _():
            _fill_from_tokens(xprev_ref, acc, xnext_ref, meta_ref, gain_ref, xext_ref, tm, tiles,
                              tail_tokens)

        pl.when(jnp.logical_not(last_tile))(functools.partial(step, tm, add_all))
        pl.when(last_tile)(functools.partial(step, compact_rows, add_compact))

    _stream_row_tiles(
        functools.partial(_tile_copy, hbm=x_hbm, acc_ref=acc_ref, sems=in_sems, tm=tm, tiles=tiles,
                          tail_rows=tail_tokens, to_hbm=False),
        functools.partial(_tile_copy, hbm=out_hbm, acc_ref=acc_ref, sems=out_sems, tm=tm, tiles=tiles,
                          tail_rows=tm, to_hbm=True),
        body)


def _ffn_kernel(tm, tiles, out_tail_rows, n_chunks, tok_rows, hprev_ref, h_hbm, hnext_ref, gain_ref,
                wg_ref, wua_ref, wub_ref, cw_ref, cb_ref, wd_ref, out_hbm, xext_ref, acc_ref, in_sems,
                out_sems):
    c = FFN_COL_TILE
    i, j = pl.program_id(0), pl.program_id(1)
    acc = acc_ref.at[lax.rem(i, 2)]
    last_tile = lax.rem(i, tiles) == tiles - 1
    n_pairs, odd = n_chunks // 2, n_chunks % 2
    compact_rows = tok_rows + HALO + N_META

    def add_all(r):
        acc[...] += r

    def add_compact(r):
        acc[0:tok_rows, :] += r[0:tok_rows]
        acc[tm - N_META:tm, :] += r[tok_rows + HALO:compact_rows]

    def steps(m, add):
        def act(g_ext, u, lo):
            g = _conv3(g_ext, cw_ref[:, lo:lo + c], m) + cb_ref[:, lo:lo + c]
            return (_gelu_exact(g) * u).astype(_BF16)

        def pair_step():
            x, xc = xext_ref[0:m + 2 * HALO, :], xext_ref[HALO:HALO + m, :]
            g2 = _dot(x, wg_ref[...])
            ua, ub = _dot(xc, wua_ref[...]), _dot(xc, wub_ref[...])
            a = jnp.concatenate([act(g2[:, :c], ua, 0), act(g2[:, c:], ub, c)], axis=1)
            add(_dot(a, wd_ref[...]))

        def single_step():
            g = _dot(xext_ref[0:m + 2 * HALO, :], wg_ref[:, 0:c])
            ua = _dot(xext_ref[HALO:HALO + m, :], wua_ref[...])
            add(_dot(act(g, ua, 0), wd_ref[0:c, :]))

        return pair_step, single_step

    def body():
        @pl.when(jnp.logical_and(j == 0, jnp.logical_not(last_tile)))
        def _():
            _fill_from_slab(hprev_ref, acc, hnext_ref, gain_ref, xext_ref, tm)

        @pl.when(jnp.logical_and(j == 0, last_tile))
        def _():
            _fill_compact_from_slab(hprev_ref, acc, hnext_ref, gain_ref, xext_ref, tm, tok_rows)

        for this_tile, (pair_step, single_step) in (
                (jnp.logical_not(last_tile), steps(tm, add_all)),
                (last_tile, steps(compact_rows, add_compact))):
            pl.when(jnp.logical_and(this_tile, j < n_pairs))(pair_step)
            if odd:
                pl.when(jnp.logical_and(this_tile, j == n_pairs))(single_step)

    _stream_row_tiles(
        functools.partial(_tile_copy, hbm=h_hbm, acc_ref=acc_ref, sems=in_sems, tm=tm, tiles=tiles,
                          tail_rows=tm, to_hbm=False),
        functools.partial(_tile_copy, hbm=out_hbm, acc_ref=acc_ref, sems=out_sems, tm=tm, tiles=tiles,
                          tail_rows=out_tail_rows, to_hbm=True),
        body)


def _fused_params():
    return pltpu.CompilerParams(dimension_semantics=("arbitrary", "arbitrary"),
                                vmem_limit_bytes=V7X_VMEM_LIMIT_BYTES)


def _fused_scratch(tm, d):
    return [pltpu.VMEM((tm + 2 * HALO, d), _BF16), pltpu.VMEM((2, tm, d), _F32),
            pltpu.SemaphoreType.DMA((2,)), pltpu.SemaphoreType.DMA((2,))]


def _weight_specs(layer, d, hidden, n_up, col_tile):
    n_j = hidden // col_tile
    ups = [pl.BlockSpec((None, d, col_tile),
                        functools.partial(lambda i, j, k: (layer, 0, j + k * n_j), k=k))
           for k in range(n_up)]
    down = pl.BlockSpec((None, col_tile, d), lambda i, j: (layer, j, 0))
    return ups, down


def _mixer_from_tokens(x, meta_tokens, gain, w_in, conv_w, w_out, layer, tm, cast_weights):
    bsz, seq, d = x.shape
    tp = seq + BLOCK
    tiles = tp // tm
    tail_tokens = seq - (tiles - 1) * tm
    assert tiles * tm == tp and 0 < tail_tokens < tm and tail_tokens % HALO == 0 and tm % HALO == 0
    halo_blocks_per_tile = tm // HALO
    col_tile = MIXER_COL_TILE

    def prev_map(i, j):
        b, q = _row_tile_index(i, tiles)
        return b, jnp.maximum(q * halo_blocks_per_tile - 1, 0), 0

    def next_map(i, j):
        b, q = _row_tile_index(i, tiles)
        wrap = q == tiles - 1
        return (jnp.where(wrap, lax.rem(b + 1, bsz), b),
                jnp.where(wrap, 0, (q + 1) * halo_blocks_per_tile), 0)

    ups, down = _weight_specs(layer, d, d, 3, col_tile)
    n_j = d // col_tile
    casts = [_CastPlan(w, l, bsz * tiles * n_j, lambda i, j: i * n_j + j) for w, l in cast_weights]
    in_specs = [
        pl.BlockSpec((1, HALO, d), prev_map),
        pl.BlockSpec(memory_space=pl.ANY),
        pl.BlockSpec((1, HALO, d), next_map),
        pl.BlockSpec((N_META, d), lambda i, j: (0, 0)),
        pl.BlockSpec((1, d), lambda i, j: (0, 0)),
        *ups,
        pl.BlockSpec((None, 3, col_tile), lambda i, j: (layer, 0, j)),
        down,
        *[c.in_spec for c in casts],
    ]
    out, *cast_out = pl.pallas_call(
        functools.partial(_mixer_kernel, tm, tiles, tail_tokens, len(casts)),
        grid=(bsz * tiles, n_j),
        in_specs=in_specs,
        out_specs=[pl.BlockSpec(memory_space=pl.ANY), *[c.out_spec for c in casts]],
        out_shape=[jax.ShapeDtypeStruct((bsz, tp, d), _F32), *[c.out_shape for c in casts]],
        scratch_shapes=_fused_scratch(tm, d),
        compiler_params=_fused_params(),
    )(x, x, x, meta_tokens, gain.reshape(1, d), w_in, w_in, w_in, conv_w, w_out,
      *[c.operand for c in casts])
    return out, cast_out


def _conv_ffn(h, gain, w_up, conv_w, conv_b, w_down, layer, conv_layer, tm, out_rows):
    bsz, tp, d = h.shape
    hidden = w_down.shape[1]
    tiles = tp // tm
    out_tail_rows = out_rows - (tiles - 1) * tm
    tok_rows = tm - BLOCK
    assert 0 < tok_rows and tok_rows % NORM_ROWS == 0 and LEAD >= HALO
    assert tiles * tm == tp and tm % HALO == 0 and 0 < out_tail_rows <= tm and out_tail_rows % F32_SUBLANES == 0
    halo_blocks_per_tile = tm // HALO
    halo_blocks_per_slab = tp // HALO
    n_halo_blocks = bsz * halo_blocks_per_slab
    col_tile = FFN_COL_TILE

    def halo_map(i, j, offset):
        blk = lax.rem(i * halo_blocks_per_tile + offset + n_halo_blocks, n_halo_blocks)
        return lax.div(blk, halo_blocks_per_slab), lax.rem(blk, halo_blocks_per_slab), 0

    n_chunks = hidden // col_tile
    assert n_chunks * col_tile == hidden and w_up.shape[2] == 2 * hidden
    in_specs = [
        pl.BlockSpec((1, HALO, d), functools.partial(halo_map, offset=-1)),
        pl.BlockSpec(memory_space=pl.ANY),
        pl.BlockSpec((1, HALO, d), functools.partial(halo_map, offset=halo_blocks_per_tile)),
        pl.BlockSpec((1, d), lambda i, j: (0, 0)),
        pl.BlockSpec((None, d, 2 * col_tile), lambda i, j: (layer, 0, j)),
        pl.BlockSpec((None, d, col_tile), lambda i, j: (layer, 0, n_chunks + 2 * j)),
        pl.BlockSpec((None, d, col_tile),
                     lambda i, j: (layer, 0, jnp.minimum(n_chunks + 2 * j + 1, 2 * n_chunks - 1))),
        pl.BlockSpec((None, 3, 2 * col_tile), lambda i, j: (conv_layer, 0, j)),
        pl.BlockSpec((None, 1, 2 * col_tile), lambda i, j: (conv_layer, 0, j)),
        pl.BlockSpec((None, 2 * col_tile, d), lambda i, j: (layer, j, 0)),
    ]
    return pl.pallas_call(
        functools.partial(_ffn_kernel, tm, tiles, out_tail_rows, n_chunks, tok_rows),
        grid=(bsz * tiles, (n_chunks + 1) // 2),
        in_specs=in_specs,
        out_specs=pl.BlockSpec(memory_space=pl.ANY),
        out_shape=jax.ShapeDtypeStruct((bsz, out_rows, d), _F32),
        scratch_shapes=_fused_scratch(tm, d),
        compiler_params=_fused_params(),
    )(h, h, h, gain.reshape(1, d), w_up, w_up, w_up, conv_w, conv_b.reshape(conv_b.shape[0], 1, hidden),
      w_down)


def _qkv_kernel(n_norm_cols, h_ref, gain_ref, w_ref, hgain_ref, out_ref, xn_ref):
    _norm_rows_into(xn_ref, 0, h_ref, h_ref.shape[0], gain_ref[...])
    for c in range(w_ref.shape[1] // QKV_COL_TILE):
        acc = _dot(xn_ref[...], w_ref[:, c * QKV_COL_TILE:(c + 1) * QKV_COL_TILE])
        for hd in range(QKV_COL_TILE // HEAD_DIM):
            lo = hd * HEAD_DIM
            cols = slice(c * QKV_COL_TILE + lo, c * QKV_COL_TILE + lo + HEAD_DIM)
            head = acc[:, lo:lo + HEAD_DIM]
            if c * QKV_COL_TILE + lo < n_norm_cols:
                head = _rmsnorm_rows(head, hgain_ref[:, cols])
            out_ref[:, cols] = head.astype(_BF16)


def _qkv_proj(h, gain, w_qkv, layer, head_gain_cols, tm):
    m, d = h.shape
    n_out = w_qkv.shape[2]
    return pl.pallas_call(
        functools.partial(_qkv_kernel, D_MODEL + KV_DIM),
        grid=(m // tm,),
        in_specs=[
            pl.BlockSpec((tm, d), lambda i: (i, 0)),
            pl.BlockSpec((1, d), lambda i: (0, 0)),
            pl.BlockSpec((None, d, n_out), lambda i: (layer, 0, 0), pipeline_mode=pl.Buffered(1)),
            pl.BlockSpec((1, n_out), lambda i: (0, 0)),
        ],
        out_specs=pl.BlockSpec((tm, n_out), lambda i: (i, 0)),
        out_shape=jax.ShapeDtypeStruct((m, n_out), _BF16),
        scratch_shapes=[pltpu.VMEM((tm, d), _BF16)],
        compiler_params=pltpu.CompilerParams(
            dimension_semantics=("arbitrary",), vmem_limit_bytes=V7X_VMEM_LIMIT_BYTES),
    )(h, gain.reshape(1, d), w_qkv, head_gain_cols)


def _bias_kernel(n_casts, band_codes_ref, tail_codes_ref, tbl_ref, *rest):
    out_ref = rest[n_casts]
    _run_casts(rest[:n_casts] + rest[n_casts + 1:])
    head = pl.program_id(0)

    def lookup(codes):
        acc = jnp.zeros(codes.shape, _F32)
        for code in range(N_BUCKETS + 2):
            acc = jnp.where(codes == code, tbl_ref[head, code] * LOG2_E, acc)
        return acc

    band = lookup(band_codes_ref[...])
    tails = [lookup(tail_codes_ref[v]) for v in range(3)]
    masked = jnp.full((BLOCK, BLOCK), NEG_INF, _F32)
    for variant, valid_blocks, tail in ((0, (2,), 0), (1, (1, 2), 1), (2, (0, 1, 2), 2), (3, (0, 1), 2)):
        for blk in range(3):
            cols = slice(blk * BLOCK, (blk + 1) * BLOCK)
            out_ref[variant, 0, :, cols] = band[:, cols] if blk in valid_blocks else masked
        out_ref[variant, 0, :, 3 * BLOCK:] = tails[tail]


def _attn_bias(band_codes, tail_codes, table, cast_weights):
    casts = [_CastPlan(w, l, N_HEADS, lambda h: h) for w, l in cast_weights]
    bias, *cast_out = pl.pallas_call(
        functools.partial(_bias_kernel, len(casts)),
        grid=(N_HEADS,),
        in_specs=[
            pl.BlockSpec((BLOCK, 3 * BLOCK), lambda h: (0, 0)),
            pl.BlockSpec((3, BLOCK, BLOCK), lambda h: (0, 0, 0)),
            pl.BlockSpec(memory_space=pltpu.SMEM),
            *[c.in_spec for c in casts],
        ],
        out_specs=[pl.BlockSpec((N_BIAS_VARIANTS, 1, BLOCK, KEYS), lambda h: (0, h, 0, 0)),
                   *[c.out_spec for c in casts]],
        out_shape=[jax.ShapeDtypeStruct((N_BIAS_VARIANTS, N_HEADS, BLOCK, KEYS), _F32),
                   *[c.out_shape for c in casts]],
    )(band_codes, tail_codes, table, *[c.operand for c in casts])
    return bias, cast_out


def _attn_chain(q_ref, k_refs, v_refs, bias_ref, out_ref, slab, k_slabs, g, row0):
    rows = BLOCK - row0
    kv_cols = slice(g * HEAD_DIM, (g + 1) * HEAD_DIM)
    head_cols = [slice((g * GROUP + r) * HEAD_DIM, (g * GROUP + r + 1) * HEAD_DIM) for r in range(GROUP)]
    dead = jnp.zeros((BLOCK - N_META, HEAD_DIM), _BF16)
    q = jnp.concatenate([q_ref[slab, row0:, cols] for cols in head_cols], axis=0)
    k = jnp.concatenate([ref[ks, :, kv_cols] for ref, ks in zip(k_refs, k_slabs)] + [dead], axis=0)
    v = jnp.concatenate([ref[ks, :, kv_cols] for ref, ks in zip(v_refs, k_slabs)] + [dead], axis=0)
    s = lax.dot_general(q, k, (((1,), (1,)), ((), ())), preferred_element_type=_F32)
    s = s + jnp.concatenate([bias_ref[0, g * GROUP + r, row0:, :] for r in range(GROUP)], axis=0)
    e = jnp.exp2(s - jnp.max(s, axis=-1, keepdims=True))
    denom = jnp.sum(e, axis=-1, keepdims=True)
    o = _dot(e.astype(_BF16), v) / denom
    for r, cols in enumerate(head_cols):
        out_ref[slab, row0:, cols] = o[r * rows:(r + 1) * rows].astype(_BF16)


def _attn_kernel(n_casts, q_ref, kp_ref, kc_ref, kn_ref, km_ref, vp_ref, vc_ref, vn_ref, vm_ref, bias_ref,
                 *rest):
    out_ref = rest[n_casts]
    _run_casts(rest[:n_casts] + rest[n_casts + 1:])
    n = pl.program_id(0)
    n_slabs = q_ref.shape[0]
    k_refs = (kp_ref, kc_ref, kn_ref, km_ref)
    v_refs = (vp_ref, vc_ref, vn_ref, vm_ref)

    def all_chains(first_block):
        for slab in range(n_slabs):
            if first_block:
                k_slabs = (slab, slab, (slab + 1) % n_slabs, slab)
            else:
                k_slabs = (slab, slab, slab, (slab - 1) % n_slabs)
            for g in range(N_KV_HEADS):
                _attn_chain(q_ref, k_refs, v_refs, bias_ref, out_ref, slab, k_slabs, g,
                            LEAD if first_block else 0)

    @pl.when(n == 0)
    def _():
        all_chains(True)
        out_ref[:, 0:LEAD, :] = jnp.zeros((n_slabs, LEAD, D_MODEL), _BF16)

    @pl.when(n > 0)
    def _():
        all_chains(False)


def _attention(qkv, bias, nb, cast_weights):
    bsz = qkv.shape[0]
    casts = [_CastPlan(w, l, nb, lambda n: n) for w, l in cast_weights]
    k_col, v_col = D_MODEL // KV_DIM, D_MODEL // KV_DIM + 1

    def slab_block(n):
        return jnp.where(n == 0, nb - 1, n - 1)

    def blk(col, shift):
        return pl.BlockSpec(
            (bsz, BLOCK, KV_DIM), lambda n: (0, slab_block(jnp.clip(n + shift, 0, nb - 1)), col))

    def meta(col):
        return pl.BlockSpec(
            (bsz, N_META, KV_DIM), lambda n: (0, (nb - 1) * (BLOCK // N_META) + LEAD // N_META, col))

    out, *cast_out = pl.pallas_call(
        functools.partial(_attn_kernel, len(casts)),
        grid=(nb,),
        in_specs=[
            pl.BlockSpec((bsz, BLOCK, D_MODEL), lambda n: (0, slab_block(n), 0)),
            blk(k_col, -1), blk(k_col, 0), blk(k_col, 1), meta(k_col),
            blk(v_col, -1), blk(v_col, 0), blk(v_col, 1), meta(v_col),
            pl.BlockSpec((1, N_HEADS, BLOCK, KEYS),
                         lambda n: (jnp.where(n == nb - 1, 3, jnp.minimum(n, 2)), 0, 0, 0)),
            *[c.in_spec for c in casts],
        ],
        out_specs=[pl.BlockSpec((bsz, BLOCK, D_MODEL), lambda n: (0, slab_block(n), 0)),
                   *[c.out_spec for c in casts]],
        out_shape=[jax.ShapeDtypeStruct((bsz, nb * BLOCK, D_MODEL), _BF16),
                   *[c.out_shape for c in casts]],
        compiler_params=pltpu.CompilerParams(
            dimension_semantics=("arbitrary",), vmem_limit_bytes=V7X_VMEM_LIMIT_BYTES),
    )(qkv, qkv, qkv, qkv, qkv, qkv, qkv, qkv, qkv, bias, *[c.operand for c in casts])
    return out, cast_out


def _out_proj_kernel(a_ref, w_ref, h_ref, out_ref):
    out_ref[...] = h_ref[...] + _dot(a_ref[...], w_ref[...])


def _out_proj(a, w, layer, h, tm):
    m, d = h.shape
    return pl.pallas_call(
        _out_proj_kernel,
        grid=(m // tm,),
        in_specs=[
            pl.BlockSpec((tm, d), lambda i: (i, 0)),
            pl.BlockSpec((None, d, d), lambda i: (layer, 0, 0), pipeline_mode=pl.Buffered(1)),
            pl.BlockSpec((tm, d), lambda i: (i, 0)),
        ],
        out_specs=pl.BlockSpec((tm, d), lambda i: (i, 0)),
        out_shape=jax.ShapeDtypeStruct((m, d), _F32),
        input_output_aliases={2: 0},
        compiler_params=pltpu.CompilerParams(
            dimension_semantics=("arbitrary",), vmem_limit_bytes=V7X_VMEM_LIMIT_BYTES),
    )(a, w, h)


def _t5_bucket(rel):
    half = N_BUCKETS // 2
    max_exact = half // 2
    side = jnp.where(rel > 0, half, 0)
    n = jnp.abs(rel)
    nf = jnp.maximum(n, 1).astype(_F32)
    large = max_exact + (jnp.log(nf / max_exact) / math.log(MAX_DISTANCE / max_exact)
                         * (half - max_exact)).astype(jnp.int32)
    large = jnp.minimum(large, half - 1)
    return side + jnp.where(n < max_exact, n, large)


def _bias_codes():
    qi = jnp.arange(BLOCK, dtype=jnp.int32)[:, None]
    col = jnp.arange(3 * BLOCK, dtype=jnp.int32)[None, :]
    rel_band = col - BLOCK - qi
    band = jnp.where(jnp.abs(rel_band) <= WINDOW, _t5_bucket(rel_band), CODE_MASKED)
    tcol = jnp.arange(BLOCK, dtype=jnp.int32)[None, :]
    tails = []
    for n_rep in range(3):
        qpos = n_rep * BLOCK + qi
        codes = jnp.where(tcol < N_META, _t5_bucket(LEAD + tcol - qpos), CODE_MASKED)
        tails.append(jnp.where(tcol == N_META, CODE_SINK, codes))
    return band.astype(jnp.int32), jnp.stack(tails).astype(jnp.int32)


def kernel(x, meta_tokens, rel_bias_table, norm_mix, norm_ffn, conv_in_w, conv_dw, conv_out_w,
           attn_qkv, attn_q_gain, attn_k_gain, attn_sink, attn_o, ffn_up, ffn_dw, ffn_dw_b, ffn_down):
    bsz, seq, d = x.shape
    tp = seq + BLOCK
    nb = tp // BLOCK
    tm = tp // FFN_ROW_TILES_PER_SLAB
    assert tm * FFN_ROW_TILES_PER_SLAB == tp and d == D_MODEL and seq % BLOCK == 0 and nb >= 4

    table = jnp.concatenate([
        rel_bias_table.T.astype(_F32), jnp.full((N_HEADS, 1), NEG_INF, _F32),
        attn_sink[0].astype(_F32)[:, None]], axis=1)
    bias, (in_w, out_w) = _attn_bias(*_bias_codes(), table, [(conv_in_w, 0), (conv_out_w, 0)])

    h, (up0, down0, qkv_w, o_w) = _mixer_from_tokens(
        x, meta_tokens.astype(x.dtype), norm_mix[0], in_w, conv_dw, out_w, 0,
        tp // MIXER_ROW_TILES_PER_SLAB, [(ffn_up, 0), (ffn_down, 0), (attn_qkv, 0), (attn_o, 0)])
    h = _conv_ffn(h, norm_ffn[0], up0, ffn_dw, ffn_dw_b, down0, 0, 0, tm, tp)

    q_scale = HEAD_DIM ** -0.5 * LOG2_E
    head_gain_cols = jnp.concatenate([
        jnp.tile(attn_q_gain[0] * q_scale, N_HEADS), jnp.tile(attn_k_gain[0], N_KV_HEADS),
        jnp.ones((KV_DIM,), _F32)]).reshape(1, QKV_DIM)
    h2 = h.reshape(bsz * tp, d)
    qkv = _qkv_proj(h2, norm_mix[1], qkv_w, 0, head_gain_cols, tp // QKV_ROW_TILES_PER_SLAB)
    o, (up1, down1) = _attention(qkv.reshape(bsz, tp, QKV_DIM), bias, nb, [(ffn_up, 1), (ffn_down, 1)])
    h2 = _out_proj(o.reshape(bsz * tp, d), o_w, 0, h2, tp // OUT_PROJ_ROW_TILES_PER_SLAB)
    return _conv_ffn(h2.reshape(bsz, tp, d), norm_ffn[1], up1, ffn_dw, ffn_dw_b, down1, 0, 1, tm, seq)
```
